```python
import math
import jax, jax.numpy as jnp
from jax import lax
import numpy as np

D_MODEL = 1024
BATCH = 2
SEQ = 8192
DEPTH = 2

N_A_LAYERS = DEPTH // 2
N_B_LAYERS = DEPTH - N_A_LAYERS
N_DENSE = (DEPTH + 1) // 2
N_MOE = DEPTH // 2

HGRN_HEADS = 8
HGRN_HEAD_DIM = D_MODEL // HGRN_HEADS
HGRN_CHUNK = 64

NSA_HEADS = 16
NSA_KV_HEADS = 4
NSA_HPG = NSA_HEADS // NSA_KV_HEADS
NSA_HEAD_DIM = D_MODEL // NSA_HEADS
N_BRANCH = 3
CMP_BLOCK = 32
CMP_STRIDE = 16
CMP_HIDDEN = 256
SLC_BLOCK = 64
SLC_TOP = 16
WINDOW = 512
Q_BLOCK = 128

REL_BUCKETS = 32
REL_MAX_DIST = 2048

FFN_DIM = 2816
N_EXPERTS = 8
TOP_K = 2
EXPERT_DIM = 3584

NORM_EPS = 1e-6
NEG_INF = -1e30
FORCE_BONUS = 1e4

kernel_name = "hybrid_hgrn2_nsa_yoco_moe"


def rmsnorm(x, g):
    xf = x.astype(jnp.float32)
    y = xf * lax.rsqrt(jnp.mean(xf * xf, axis=-1, keepdims=True) + NORM_EPS)
    return (y * g.astype(jnp.float32)).astype(x.dtype)


def rel_bucket(dist):
    max_exact = REL_BUCKETS // 2
    d = jnp.maximum(dist, 0)
    large = max_exact + (jnp.log(jnp.maximum(d, 1).astype(jnp.float32) / max_exact)
                         / math.log(REL_MAX_DIST / max_exact) * (REL_BUCKETS - max_exact)).astype(jnp.int32)
    large = jnp.minimum(large, REL_BUCKETS - 1)
    return jnp.where(d < max_exact, d, large)


def masked_softmax(logits, mask):
    p = jax.nn.softmax(jnp.where(mask, logits, NEG_INF), axis=-1)
    return p * mask


def swiglu(x, w_gu, w_down):
    gate, up = jnp.split(x @ w_gu, 2, axis=-1)
    return (jax.nn.silu(gate) * up) @ w_down


def moe_swiglu(xn, w_router, w_gu, w_down):
    logits = (xn @ w_router).astype(jnp.float32)
    top_v, top_i = lax.top_k(logits, TOP_K)
    top_w = jax.nn.softmax(top_v, axis=-1)
    gates = jnp.sum(jax.nn.one_hot(top_i, N_EXPERTS, dtype=jnp.float32) * top_w[..., None], axis=-2)
    out = jnp.zeros(xn.shape, jnp.float32)
    for e in range(N_EXPERTS):
        out = out + gates[..., e:e + 1] * swiglu(xn, w_gu[e], w_down[e]).astype(jnp.float32)
    return out.astype(xn.dtype)


def hgrn2_mixer(xn, w_in, w_out, o_norm_g, lb):
    B, T, D = xn.shape
    H, dh, C = HGRN_HEADS, HGRN_HEAD_DIM, HGRN_CHUNK
    nC = T // C
    q, f, i, g = jnp.split(xn @ w_in, 4, axis=-1)
    fgate = lb + (1.0 - lb) * jax.nn.sigmoid(f.astype(jnp.float32))
    k = 1.0 - fgate

    def chunks(t):
        return t.astype(jnp.float32).reshape(B, nC, C, H, dh).transpose(1, 0, 3, 2, 4)

    qc, kc, vc = chunks(q), chunks(k), chunks(i)
    bc = jnp.cumsum(chunks(jnp.log(fgate)), axis=3)
    tril = jnp.tril(jnp.ones((C, C), bool))

    def step(S, inp):
        qj, kj, vj, bj = inp
        rel = bj[:, :, :, None, :] - bj[:, :, None, :, :]
        dec = jnp.where(tril[:, :, None], jnp.exp(jnp.minimum(rel, 0.0)), 0.0)
        attn = jnp.einsum('bhsud,bhud->bhsu', qj[:, :, :, None, :] * dec, kj)
        o = (jnp.einsum('bhsu,bhue->bhse', attn, vj)
             + jnp.einsum('bhsd,bhde->bhse', qj * jnp.exp(bj), S))
        b_last = bj[:, :, -1, :]
        S = (jnp.exp(b_last)[..., None] * S
             + jnp.einsum('bhud,bhue->bhde', kj * jnp.exp(b_last[:, :, None, :] - bj), vj))
        return S, o

    S0 = jnp.zeros((B, H, dh, dh), jnp.float32)
    _, o = lax.scan(step, S0, (qc, kc, vc, bc))
    o = o.transpose(1, 0, 3, 2, 4).reshape(B, T, H, dh)
    o = rmsnorm(o, o_norm_g).reshape(B, T, D) * jax.nn.silu(g.astype(jnp.float32))
    return o.astype(xn.dtype) @ w_out


def compress_blocks(t, pos, w1, w2):
    B, T, Hk, dh = t.shape
    ch = t.reshape(B, T // CMP_STRIDE, CMP_STRIDE, Hk, dh)
    blocks = jnp.concatenate([ch[:, :-1], ch[:, 1:]], axis=2) + pos[None, None, :, None, :]
    flat = blocks.transpose(0, 1, 3, 2, 4).reshape(B, -1, Hk, CMP_BLOCK * dh)
    return jax.nn.gelu(flat @ w1) @ w2


def nsa_shared_kv(h, kv_norm_g, w_kv, k_norm_g, cmp_pos_k, cmp_w1_k, cmp_w2_k,
                  cmp_pos_v, cmp_w1_v, cmp_w2_v):
    B, T, _ = h.shape
    G, dh = NSA_KV_HEADS, NSA_HEAD_DIM
    kv = (rmsnorm(h, kv_norm_g) @ w_kv).reshape(B, T, N_BRANCH, 2, G, dh)
    k, v = kv[:, :, :, 0], kv[:, :, :, 1]
    k_cmp = rmsnorm(compress_blocks(k[:, :, 0], cmp_pos_k, cmp_w1_k, cmp_w2_k), k_norm_g[0])
    v_cmp = compress_blocks(v[:, :, 0], cmp_pos_v, cmp_w1_v, cmp_w2_v)
    k_slc = rmsnorm(k[:, :, 1], k_norm_g[1])
    k_win = rmsnorm(k[:, :, 2], k_norm_g[2])
    n_sel = T // SLC_BLOCK

    def heads(t):
        return t.transpose(0, 2, 1, 3)

    def blocked(t):
        return heads(t).reshape(B, G, n_sel, SLC_BLOCK, dh)

    def padded(t):
        return jnp.pad(heads(t), ((0, 0), (0, 0), (WINDOW, 0), (0, 0)))

    return (heads(k_cmp), heads(v_cmp), blocked(k_slc), blocked(v[:, :, 1]),
            padded(k_win), padded(v[:, :, 2]))


def nsa_mixer(xn, w_in, w_out, q_norm_g, rel_bias, k_cmp, v_cmp, k_slc, v_slc, k_win, v_win):
    B, T, _ = xn.shape
    G, HP, dh = NSA_KV_HEADS, NSA_HPG, NSA_HEAD_DIM
    n_cmp = k_cmp.shape[2]
    n_sel = T // SLC_BLOCK
    n_top = min(SLC_TOP, n_sel)
    n_qb = T // Q_BLOCK
    f32 = jnp.float32

    proj = xn @ w_in
    q = rmsnorm(proj[..., :NSA_HEADS * dh].reshape(B, T, G, HP, dh), q_norm_g) * (dh ** -0.5)
    q = q.transpose(0, 2, 3, 1, 4)
    gates = jax.nn.sigmoid(proj[..., NSA_HEADS * dh:].astype(f32)).reshape(
        B, T, G, HP, N_BRANCH).transpose(0, 2, 3, 1, 4)
    tb = rel_bias.T.reshape(G, HP, REL_BUCKETS)

    cs = np.arange(n_cmp)[:, None] * CMP_STRIDE
    ss = np.arange(n_sel)[None, :] * SLC_BLOCK
    ov = np.clip(np.minimum(cs + CMP_BLOCK, ss + SLC_BLOCK) - np.maximum(cs, ss), 0, None)
    overlap = jnp.asarray(ov / CMP_BLOCK, dtype=f32)

    cmp_end = jnp.arange(n_cmp, dtype=jnp.int32) * CMP_STRIDE + (CMP_BLOCK - 1)
    blk = jnp.arange(n_sel, dtype=jnp.int32)
    dist_w_np = np.arange(Q_BLOCK)[:, None] + WINDOW - np.arange(Q_BLOCK + WINDOW)[None, :]
    win_mask = jnp.asarray((dist_w_np >= 0) & (dist_w_np < WINDOW))
    bias_w = tb[:, :, rel_bucket(jnp.asarray(dist_w_np, jnp.int32))]
    bi = jnp.arange(B)[:, None, None, None]
    gi = jnp.arange(G)[None, :, None, None]
    gi6 = jnp.arange(G)[None, :, None, None, None, None]
    hi6 = jnp.arange(HP)[None, None, :, None, None, None]

    def block(i):
        s0 = i * Q_BLOCK
        t = s0 + jnp.arange(Q_BLOCK, dtype=jnp.int32)
        qi = lax.dynamic_slice_in_dim(q, s0, Q_BLOCK, axis=3)
        gq = lax.dynamic_slice_in_dim(gates, s0, Q_BLOCK, axis=3)
        dist_c = t[:, None] - cmp_end[None, :]
        lc = jnp.einsum('bghqd,bgnd->bghqn', qi, k_cmp).astype(f32) + tb[:, :, rel_bucket(dist_c)]
        pc = masked_softmax(lc, dist_c >= 0)
        o_cmp = jnp.einsum('bghqn,bgnd->bghqd', pc, v_cmp.astype(f32))
        imp = jnp.einsum('bghqn,ns->bgqs', pc, overlap)
        cur = (t // SLC_BLOCK)[:, None]
        forced = (blk[None, :] == 0) | (blk[None, :] == cur) | (blk[None, :] == cur - 1)
        causal = blk[None, :] * SLC_BLOCK <= t[:, None]
        score = jnp.where(causal, imp + jnp.where(forced, FORCE_BONUS, 0.0), NEG_INF)
        _, top_i = lax.top_k(score, n_top)
        ks = k_slc[bi, gi, top_i]
        vs = v_slc[bi, gi, top_i]
        pos = top_i[..., None] * SLC_BLOCK + jnp.arange(SLC_BLOCK, dtype=jnp.int32)
        dist_s = t[None, None, :, None, None] - pos
        ls = (jnp.einsum('bghqd,bgqnsd->bghqns', qi, ks).astype(f32)
              + tb[gi6, hi6, rel_bucket(dist_s)[:, :, None]])
        ps = masked_softmax(ls.reshape(B, G, HP, Q_BLOCK, n_top * SLC_BLOCK),
                            (dist_s >= 0).reshape(B, G, 1, Q_BLOCK, n_top * SLC_BLOCK))
        o_slc = jnp.einsum('bghqm,bgqmd->bghqd', ps,
                           vs.reshape(B, G, Q_BLOCK, n_top * SLC_BLOCK, dh).astype(f32))
        kw = lax.dynamic_slice_in_dim(k_win, s0, Q_BLOCK + WINDOW, axis=2)
        vw = lax.dynamic_slice_in_dim(v_win, s0, Q_BLOCK + WINDOW, axis=2)
        kpos = s0 - WINDOW + jnp.arange(Q_BLOCK + WINDOW, dtype=jnp.int32)
        lw = jnp.einsum('bghqd,bgkd->bghqk', qi, kw).astype(f32) + bias_w
        pw = masked_softmax(lw, win_mask & (kpos >= 0)[None, :])
        o_win = jnp.einsum('bghqk,bgkd->bghqd', pw, vw.astype(f32))
        return gq[..., 0:1] * o_cmp + gq[..., 1:2] * o_slc + gq[..., 2:3] * o_win

    outs = lax.map(block, jnp.arange(n_qb, dtype=jnp.int32))
    o = outs.transpose(1, 0, 4, 2, 3, 5).reshape(B, T, NSA_HEADS * dh)
    return o.astype(xn.dtype) @ w_out


def setup_inputs(seed: int = 0) -> dict:
    key = jax.random.key(seed)
    ks = jax.random.split(key, 32)
    f32 = jnp.float32
    H, dh, G = NSA_HEADS, NSA_HEAD_DIM, NSA_KV_HEADS

    def nrm(k, shape, fan_in):
        return jax.random.normal(k, shape, f32) * fan_in ** -0.5

    def gain(k, shape):
        return 1.0 + 0.02 * jax.random.normal(k, shape, f32)

    return {
        'x': jax.random.normal(ks[0], (BATCH, SEQ, D_MODEL), f32),
        'norm_mix_g': gain(ks[1], (DEPTH, D_MODEL)),
        'norm_ffn_g': gain(ks[2], (DEPTH, D_MODEL)),
        'a_w_in': nrm(ks[3], (N_A_LAYERS, D_MODEL, 4 * D_MODEL), D_MODEL),
        'a_w_out': nrm(ks[4], (N_A_LAYERS, D_MODEL, D_MODEL), D_MODEL),
        'a_onorm_g': gain(ks[5], (N_A_LAYERS, HGRN_HEAD_DIM)),
        'lb_param': 0.1 * jax.random.normal(ks[6], (DEPTH, D_MODEL), f32),
        'kv_norm_g': gain(ks[7], (D_MODEL,)),
        'w_kv': nrm(ks[8], (D_MODEL, N_BRANCH * 2 * G * dh), D_MODEL),
        'k_norm_g': gain(ks[9], (N_BRANCH, dh)),
        'cmp_pos_k': 0.02 * jax.random.normal(ks[10], (CMP_BLOCK, dh), f32),
        'cmp_w1_k': nrm(ks[11], (CMP_BLOCK * dh, CMP_HIDDEN), CMP_BLOCK * dh),
        'cmp_w2_k': nrm(ks[12], (CMP_HIDDEN, dh), CMP_HIDDEN),
        'cmp_pos_v': 0.02 * jax.random.normal(ks[13], (CMP_BLOCK, dh), f32),
        'cmp_w1_v': nrm(ks[14], (CMP_BLOCK * dh, CMP_HIDDEN), CMP_BLOCK * dh),
        'cmp_w2_v': nrm(ks[15], (CMP_HIDDEN, dh), CMP_HIDDEN),
        'b_w_in': nrm(ks[16], (N_B_LAYERS, D_MODEL, H * dh + N_BRANCH * H), D_MODEL),
        'b_w_out': nrm(ks[17], (N_B_LAYERS, H * dh, D_MODEL), H * dh),
        'b_qnorm_g': gain(ks[18], (N_B_LAYERS, dh)),
        'rel_bias': 0.2 * jax.random.normal(ks[19], (REL_BUCKETS, H), f32),
        'ffn_w_gu': nrm(ks[20], (N_DENSE, D_MODEL, 2 * FFN_DIM), D_MODEL),
        'ffn_w_down': nrm(ks[21], (N_DENSE, FFN_DIM, D_MODEL), FFN_DIM),
        'moe_router': nrm(ks[22], (N_MOE, D_MODEL, N_EXPERTS), D_MODEL),
        'moe_w_gu': nrm(ks[23], (N_MOE, N_EXPERTS, D_MODEL, 2 * EXPERT_DIM), D_MODEL),
        'moe_w_down': nrm(ks[24], (N_MOE, N_EXPERTS, EXPERT_DIM, D_MODEL), EXPERT_DIM),
    }


def reference(x, norm_mix_g, norm_ffn_g, a_w_in, a_w_out, a_onorm_g, lb_param,
              kv_norm_g, w_kv, k_norm_g, cmp_pos_k, cmp_w1_k, cmp_w2_k,
              cmp_pos_v, cmp_w1_v, cmp_w2_v, b_w_in, b_w_out, b_qnorm_g, rel_bias,
              ffn_w_gu, ffn_w_down, moe_router, moe_w_gu, moe_w_down):
    lower_bounds = jnp.cumsum(jax.nn.softmax(lb_param.astype(jnp.float32), axis=0), axis=0)
    h = x
    shared = None
    for l in range(DEPTH):
        hn = rmsnorm(h, norm_mix_g[l])
        if l < N_A_LAYERS:
            mix = hgrn2_mixer(hn, a_w_in[l], a_w_out[l], a_onorm_g[l], lower_bounds[l])
        else:
            if l == N_A_LAYERS:
                shared = nsa_shared_kv(h, kv_norm_g, w_kv, k_norm_g, cmp_pos_k, cmp_w1_k, cmp_w2_k,
                                       cmp_pos_v, cmp_w1_v, cmp_w2_v)
            j = l - N_A_LAYERS
            mix = nsa_mixer(hn, b_w_in[j], b_w_out[j], b_qnorm_g[j], rel_bias, *shared)
        h = h + mix
        hn = rmsnorm(h, norm_ffn_g[l])
        if l % 2 == 0:
            h = h + swiglu(hn, ffn_w_gu[l // 2], ffn_w_down[l // 2])
        else:
            h = h + moe_swiglu(hn, moe_router[l // 2], moe_w_gu[l // 2], moe_w_down[l // 2])
    return h
```

```python
import functools
import math

import jax
import jax.numpy as jnp
import numpy as np
from jax import lax
from jax.experimental import pallas as pl
from jax.experimental.pallas import tpu as pltpu

F32 = jnp.float32
BF16 = jnp.bfloat16

NORM_EPS = 1e-6
NEG_INF = -1e30
FORCE_BONUS = 1e4
HGRN_HEAD_DIM = 128
NSA_HEADS = 16
NSA_KV_HEADS = 4
NSA_HPG = NSA_HEADS // NSA_KV_HEADS
NSA_HEAD_DIM = 64
N_BRANCH = 3
CMP_BLOCK = 32
CMP_STRIDE = 16
SLC_BLOCK = 64
SLC_TOP = 16
WINDOW = 512
Q_BLOCK = 128
REL_BUCKETS = 32
REL_MAX_DIST = 2048
N_EXPERTS = 8

LANES = 128
SUBLANES = 8
VMEM_LIMIT_BYTES = 56 * 1024 * 1024

HGRN_CHUNK = 128
HGRN_ROWS_PER_STEP = 512
SLC_KEY_TILE = 512
SLC_BLOCKS_PER_TILE = SLC_KEY_TILE // SLC_BLOCK
CMP_LOOKBACK = Q_BLOCK // CMP_STRIDE


def _cparams(sem):
    return pltpu.CompilerParams(dimension_semantics=sem, vmem_limit_bytes=VMEM_LIMIT_BYTES)


def _dot(a, b):
    return jnp.dot(a, b, preferred_element_type=F32)


def _dot_nt(a, b):
    return lax.dot_general(a, b, (((1,), (1,)), ((), ())), preferred_element_type=F32)


def _split_bf16(x):
    hi = x.astype(BF16)
    lo = (x - hi.astype(F32)).astype(BF16)
    return hi, lo


def _norm_matmul_kernel(x_ref, g_ref, w_ref, *rest, epilogue):
    if epilogue == "headnorm":
        gain_ref, flag_ref, bd_ref, o_ref, xn_ref = rest
    else:
        o_ref, xn_ref = rest

    @pl.when(pl.program_id(1) == 0)
    def _():
        x = x_ref[...]
        ms = jnp.mean(x * x, axis=-1, keepdims=True)
        xn_ref[...] = (x * lax.rsqrt(ms + NORM_EPS) * g_ref[...]).astype(BF16)

    acc = _dot(xn_ref[...], w_ref[...])
    if epilogue == "headnorm":
        ss = _dot((acc * acc).astype(BF16), bd_ref[...])
        normed = acc * lax.rsqrt(ss * (1.0 / NSA_HEAD_DIM) + NORM_EPS) * gain_ref[...]
        acc = jnp.where(flag_ref[...] > 0.0, normed, acc)
    elif epilogue == "sigmoid":
        acc = jax.nn.sigmoid(acc)
    o_ref[...] = acc.astype(o_ref.dtype)


def _norm_matmul(x, g, w, *, epilogue="none", gain=None, flag=None, out_dtype=F32, tm=512, tn=512,
                 name="norm_matmul"):
    n, d = x.shape
    m = w.shape[1]
    tm = min(tm, n)
    tn = min(tn, m)
    assert n % tm == 0 and m % tn == 0, (n, tm, m, tn)
    in_specs = [
        pl.BlockSpec((tm, d), lambda i, j: (i, 0)),
        pl.BlockSpec((1, d), lambda i, j: (0, 0)),
        pl.BlockSpec((d, tn), lambda i, j: (0, j)),
    ]
    args = [x, g.reshape(1, d).astype(F32), w]
    if epilogue == "headnorm":
        assert tn % NSA_HEAD_DIM == 0
        grp = np.arange(tn) // NSA_HEAD_DIM
        bd = jnp.asarray(grp[:, None] == grp[None, :], BF16)
        in_specs += [
            pl.BlockSpec((1, tn), lambda i, j: (0, j)),
            pl.BlockSpec((1, tn), lambda i, j: (0, j)),
            pl.BlockSpec((tn, tn), lambda i, j: (0, 0)),
        ]
        args += [gain.reshape(1, m).astype(F32), flag.reshape(1, m).astype(F32), bd]
    return pl.pallas_call(
        functools.partial(_norm_matmul_kernel, epilogue=epilogue),
        out_shape=jax.ShapeDtypeStruct((n, m), out_dtype),
        grid=(n // tm, m // tn),
        in_specs=in_specs,
        out_specs=pl.BlockSpec((tm, tn), lambda i, j: (i, j)),
        scratch_shapes=[pltpu.VMEM((tm, d), BF16)],
        compiler_params=_cparams(("parallel", "arbitrary")),
        name=name,
    )(*args)


def _matmul_res_kernel(a_ref, w_ref, r_ref, o_ref):
    o_ref[...] = r_ref[...] + _dot(a_ref[...], w_ref[...])


def _matmul_res(a, w, res, *, tm=512, tn=512, name="matmul_res"):
    n, k = a.shape
    m = w.shape[1]
    tm = min(tm, n)
    tn = min(tn, m)
    assert n % tm == 0 and m % tn == 0
    return pl.pallas_call(
        _matmul_res_kernel,
        out_shape=jax.ShapeDtypeStruct((n, m), F32),
        grid=(n // tm, m // tn),
        in_specs=[
            pl.BlockSpec((tm, k), lambda i, j: (i, 0)),
            pl.BlockSpec((k, tn), lambda i, j: (0, j)),
            pl.BlockSpec((tm, tn), lambda i, j: (i, j)),
        ],
        out_specs=pl.BlockSpec((tm, tn), lambda i, j: (i, j)),
        compiler_params=_cparams(("parallel", "parallel")),
        name=name,
    )(a, w, res)


def _ffn_kernel(x_ref, g_ref, wg_ref, wu_ref, wd_ref, s_ref, r_ref, o_ref, xn_ref, acc_ref):
    f = pl.program_id(1)

    @pl.when(f == 0)
    def _():
        x = x_ref[...]
        ms = jnp.mean(x * x, axis=-1, keepdims=True)
        xn_ref[...] = (x * lax.rsqrt(ms + NORM_EPS) * g_ref[...]).astype(BF16)
        acc_ref[...] = jnp.zeros_like(acc_ref)

    xn = xn_ref[...]
    gate = _dot(xn, wg_ref[...])
    up = _dot(xn, wu_ref[...])
    act = (gate * jax.nn.sigmoid(gate) * up).astype(BF16)
    acc_ref[...] += _dot(act, wd_ref[...])

    @pl.when(f == pl.num_programs(1) - 1)
    def _():
        o_ref[...] = r_ref[...] + s_ref[...] * acc_ref[...]


def _ffn(x, g, w_gu, w_down, scale, res, *, tm=512, tf=None, name="ffn"):
    n, d = x.shape
    ff = w_down.shape[0]
    if tf is None:
        tf = ff // 2 if (ff // 2) % LANES == 0 else ff
    tm = min(tm, n)
    assert n % tm == 0 and ff % tf == 0 and tf % LANES == 0
    nf = ff // tf
    return pl.pallas_call(
        _ffn_kernel,
        out_shape=jax.ShapeDtypeStruct((n, d), F32),
        grid=(n // tm, nf),
        in_specs=[
            pl.BlockSpec((tm, d), lambda i, f: (i, 0)),
            pl.BlockSpec((1, d), lambda i, f: (0, 0)),
            pl.BlockSpec((d, tf), lambda i, f: (0, f)),
            pl.BlockSpec((d, tf), lambda i, f: (0, nf + f)),
            pl.BlockSpec((tf, d), lambda i, f: (f, 0)),
            pl.BlockSpec((tm, 1), lambda i, f: (i, 0)),
            pl.BlockSpec((tm, d), lambda i, f: (i, 0)),
        ],
        out_specs=pl.BlockSpec((tm, d), lambda i, f: (i, 0)),
        scratch_shapes=[pltpu.VMEM((tm, d), BF16), pltpu.VMEM((tm, d), F32)],
        compiler_params=_cparams(("parallel", "arbitrary")),
        name=name,
    )(x, g.reshape(1, d).astype(F32), w_gu, w_gu, w_down, scale, res)


def _hgrn_decay_matrix(c):
    levels = int(math.log2(c))
    out = np.zeros(((levels + 2) * c, c), np.float32)
    for l in range(levels):
        m = c >> (l + 1)
        for r in range(c):
            mid = (r // (2 * m)) * 2 * m + m - 1
            if r % (2 * m) >= m:
                out[l * c + r, mid + 1:r + 1] = 1.0
            else:
                out[l * c + r, r + 1:mid + 1] = 1.0
    for r in range(c):
        out[levels * c + r, :r + 1] = 1.0
        out[(levels + 1) * c + r, r + 1:] = 1.0
    return out


def _hgrn_kernel(q_ref, f_ref, v_ref, g_ref, lb_ref, gn_ref, m_ref, o_ref, st_ref, *, chunk, nchunk):
    c = chunk
    dh = HGRN_HEAD_DIM
    levels = int(math.log2(c))

    @pl.when(pl.program_id(2) == 0)
    def _():
        st_ref[...] = jnp.zeros_like(st_ref)

    lb = lb_ref[...]
    gn = gn_ref[...]
    row = lax.broadcasted_iota(jnp.int32, (c, c), 0)
    col = lax.broadcasted_iota(jnp.int32, (c, c), 1)
    rowv = lax.broadcasted_iota(jnp.int32, (c, dh), 0)

    def body(ci, carry):
        sl = pl.ds(pl.multiple_of(ci * c, c), c)
        q = q_ref[sl, :]
        v = v_ref[sl, :]
        fg = lb + (1.0 - lb) * jax.nn.sigmoid(f_ref[sl, :])
        k = 1.0 - fg
        lf = jnp.log(fg)
        hi, lo = _split_bf16(lf)
        d2 = _dot(m_ref[...], jnp.concatenate([hi, lo], axis=1))
        dall = d2[:, :dh] + d2[:, dh:]

        a = jnp.where(row == col, _dot_nt(q.astype(BF16), k.astype(BF16)), 0.0)
        for l in range(levels):
            sh = levels - 1 - l
            e = jnp.exp(dall[l * c:(l + 1) * c])
            upper = ((rowv >> sh) & 1) == 1
            qe = jnp.where(upper, q * e, 0.0).astype(BF16)
            ke = jnp.where(upper, 0.0, k * e).astype(BF16)
            same = (row >> (sh + 1)) == (col >> (sh + 1))
            a = a + jnp.where(same, _dot_nt(qe, ke), 0.0)

        b = dall[levels * c:(levels + 1) * c]
        r = dall[(levels + 1) * c:]
        st = st_ref[...]
        o = _dot(a.astype(BF16), v.astype(BF16)) + _dot_nt((q * jnp.exp(b)).astype(BF16), st.astype(BF16))
        kr = (k * jnp.exp(r)).astype(BF16)
        st_ref[...] = st * jnp.exp(b[c - 1:c, :]) + _dot(v.T.astype(BF16), kr)

        ms = jnp.mean(o * o, axis=-1, keepdims=True)
        on = o * lax.rsqrt(ms + NORM_EPS) * gn
        gg = g_ref[sl, :]
        o_ref[sl, :] = (on * (gg * jax.nn.sigmoid(gg))).astype(o_ref.dtype)
        return carry

    lax.fori_loop(0, nchunk, body, 0)


def _hgrn_mixer(proj, lb, gn, batch, seq):
    n, d4 = proj.shape
    d = d4 // 4
    heads = d // HGRN_HEAD_DIM
    rows = min(HGRN_ROWS_PER_STEP, seq)
    chunk = min(HGRN_CHUNK, rows)
    assert seq % rows == 0 and rows % chunk == 0
    nt = seq // rows
    m = jnp.asarray(_hgrn_decay_matrix(chunk), BF16)

    def col_spec(s):
        return pl.BlockSpec((rows, HGRN_HEAD_DIM), lambda b, h, t, s=s: (b * nt + t, s * heads + h))

    return pl.pallas_call(
        functools.partial(_hgrn_kernel, chunk=chunk, nchunk=rows // chunk),
        out_shape=jax.ShapeDtypeStruct((n, d), BF16),
        grid=(batch, heads, nt),
        in_specs=[
            col_spec(0), col_spec(1), col_spec(2), col_spec(3),
            pl.BlockSpec((1, HGRN_HEAD_DIM), lambda b, h, t: (0, h)),
            pl.BlockSpec((1, HGRN_HEAD_DIM), lambda b, h, t: (0, 0)),
            pl.BlockSpec(m.shape, lambda b, h, t: (0, 0)),
        ],
        out_specs=pl.BlockSpec((rows, HGRN_HEAD_DIM), lambda b, h, t: (b * nt + t, h)),
        scratch_shapes=[pltpu.VMEM((HGRN_HEAD_DIM, HGRN_HEAD_DIM), F32)],
        compiler_params=_cparams(("parallel", "parallel", "arbitrary")),
        name="hgrn2_recurrence",
    )(proj, proj, proj, proj, lb.reshape(1, d).astype(F32), gn.reshape(1, HGRN_HEAD_DIM).astype(F32), m)


def _compress_kernel(x_ref, w1c_ref, pos_ref, w1_ref, w2_ref, gain_ref, o_ref, *, apply_norm):
    hid = w2_ref.shape[0]
    nc = x_ref.shape[0]
    uv = _dot(x_ref[...], w1c_ref[...])
    posb = _dot(pos_ref[...], w1_ref[...])[0:1, :]
    pre = uv[:, :hid] + pltpu.roll(uv[:, hid:], nc - 1, 0) + posb
    out = _dot(jax.nn.gelu(pre).astype(BF16), w2_ref[...])
    if apply_norm:
        ms = jnp.mean(out * out, axis=-1, keepdims=True)
        out = out * lax.rsqrt(ms + NORM_EPS) * gain_ref[...]
    o_ref[...] = out


def _compress(x, w1, pos, w2, gain, *, apply_norm, name):
    b, g, nc, half = x.shape
    hid = w1.shape[1]
    dh = w2.shape[1]
    w1c = jnp.concatenate([w1[:half], w1[half:]], axis=1).astype(BF16)
    posr = jnp.broadcast_to(pos.reshape(1, -1), (SUBLANES, pos.size)).astype(BF16)
    return pl.pallas_call(
        functools.partial(_compress_kernel, apply_norm=apply_norm),
        out_shape=jax.ShapeDtypeStruct((b, g, nc, dh), F32),
        grid=(b, g),
        in_specs=[
            pl.BlockSpec((None, None, nc, half), lambda i, j: (i, j, 0, 0)),
            pl.BlockSpec((half, 2 * hid), lambda i, j: (0, 0)),
            pl.BlockSpec((SUBLANES, 2 * half), lambda i, j: (0, 0)),
            pl.BlockSpec((2 * half, hid), lambda i, j: (0, 0)),
            pl.BlockSpec((hid, dh), lambda i, j: (0, 0)),
            pl.BlockSpec((1, dh), lambda i, j: (0, 0)),
        ],
        out_specs=pl.BlockSpec((None, None, nc, dh), lambda i, j: (i, j, 0, 0)),
        compiler_params=_cparams(("parallel", "parallel")),
        name=name,
    )(x, w1c, posr, w1.astype(BF16), w2.astype(BF16), gain.reshape(1, dh).astype(F32))


def _pack_kv_kernel(k_ref, v_ref, gain_ref, kv_ref, vk_ref):
    kp = k_ref[...]
    vp = v_ref[...]
    lane = lax.broadcasted_iota(jnp.int32, kp.shape, 1)
    low = lane < NSA_HEAD_DIM
    sq = kp * kp
    ss_lo = jnp.sum(jnp.where(low, sq, 0.0), axis=-1, keepdims=True)
    ss_hi = jnp.sum(jnp.where(low, 0.0, sq), axis=-1, keepdims=True)
    ms = jnp.where(low, ss_lo, ss_hi) * (1.0 / NSA_HEAD_DIM)
    kn = kp * lax.rsqrt(ms + NORM_EPS) * gain_ref[...]
    kr = pltpu.roll(kn, NSA_HEAD_DIM, 1)
    vr = pltpu.roll(vp, NSA_HEAD_DIM, 1)
    kv_ref[0] = jnp.where(low, kn, vr).astype(kv_ref.dtype)
    vk_ref[0] = jnp.where(low, vp, kr).astype(vk_ref.dtype)
    kv_ref[1] = jnp.where(low, kr, vp).astype(kv_ref.dtype)
    vk_ref[1] = jnp.where(low, vr, kn).astype(vk_ref.dtype)


def _pack_kv(kvproj, gain, branch, batch, seq, *, tm=512, name="pack_kv"):
    tm = min(tm, seq)
    nt = seq // tm
    pairs = NSA_KV_HEADS // 2
    kcol = branch * 2 * pairs
    vcol = kcol + pairs
    gain2 = jnp.tile(gain.reshape(1, NSA_HEAD_DIM), (1, 2)).astype(F32)
    out = jax.ShapeDtypeStruct((batch, NSA_KV_HEADS, seq, LANES), BF16)
    ospec = pl.BlockSpec((None, 2, tm, LANES), lambda b, t, p: (b, p, t, 0))
    return pl.pallas_call(
        _pack_kv_kernel,
        out_shape=(out, out),
        grid=(batch, nt, pairs),
        in_specs=[
            pl.BlockSpec((tm, LANES), lambda b, t, p: (b * nt + t, kcol + p)),
            pl.BlockSpec((tm, LANES), lambda b, t, p: (b * nt + t, vcol + p)),
            pl.BlockSpec((1, LANES), lambda b, t, p: (0, 0)),
        ],
        out_specs=(ospec, ospec),
        compiler_params=_cparams(("parallel", "parallel", "parallel")),
        name=name,
    )(kvproj, kvproj, gain2)


def _rel_bucket_np(dist):
    max_exact = REL_BUCKETS // 2
    d = np.maximum(dist, 0)
    large = max_exact + (np.log(np.maximum(d, 1).astype(np.float32) / max_exact)
                         / math.log(REL_MAX_DIST / max_exact) * (REL_BUCKETS - max_exact)).astype(np.int32)
    large = np.minimum(large, REL_BUCKETS - 1)
    return np.where(d < max_exact, d, large).astype(np.int32)


def _bias_table(rel_bias, dist, valid):
    tb = rel_bias.astype(F32).T
    vals = tb[:, _rel_bucket_np(dist)]
    vals = jnp.where(jnp.asarray(valid)[None], vals, NEG_INF)
    q, k = dist.shape
    return vals.reshape(NSA_KV_HEADS, NSA_HPG * q, k)


def _head_masks():
    lane = lax.broadcasted_iota(jnp.int32, (Q_BLOCK, LANES), 1)
    return lane < NSA_HEAD_DIM


def _nsa_cmp_kernel(q_ref, kv_ref, vk_ref, bias_ref, ov_ref, o_ref, imp_ref, *, nc):
    qb = pl.program_id(2)
    start = pl.multiple_of(qb * CMP_LOOKBACK, SUBLANES)
    kv = kv_ref[pl.ds(start, nc), :].astype(BF16)
    vk = vk_ref[pl.ds(start, nc), :].astype(BF16)
    low = _head_masks()
    jcol = lax.broadcasted_iota(jnp.int32, (Q_BLOCK, nc), 1)
    exists = jcol >= (nc - CMP_LOOKBACK - CMP_LOOKBACK * qb)
    psum = jnp.zeros((Q_BLOCK, nc), F32)
    for pair in range(NSA_HPG // 2):
        qp = q_ref[:, pair * LANES:(pair + 1) * LANES].astype(F32)
        outs = []
        for sub in range(2):
            hp = pair * 2 + sub
            qm = (jnp.where(low, qp, 0.0) if sub == 0 else jnp.where(low, 0.0, qp)).astype(BF16)
            s = _dot_nt(qm, kv if sub == 0 else vk) + bias_ref[hp * Q_BLOCK:(hp + 1) * Q_BLOCK, :]
            s = jnp.where(exists, s, NEG_INF)
            keep = s > 0.5 * NEG_INF
            mx = jnp.max(s, axis=-1, keepdims=True)
            p = jnp.where(keep, jnp.exp(s - mx), 0.0)
            den = jnp.sum(p, axis=-1, keepdims=True)
            p = p * jnp.where(den > 0.0, 1.0 / den, 0.0)
            psum = psum + p
            outs.append(_dot(p.astype(BF16), vk if sub == 0 else kv))
        o_ref[:, pair * LANES:(pair + 1) * LANES] = jnp.where(low, outs[0], outs[1]).astype(o_ref.dtype)
    hi, lo = _split_bf16(psum)
    imp_ref[...] = _dot(jnp.concatenate([hi, lo], axis=1), ov_ref[...])


def _nsa_cmp(q, kvc, vkc, bias_c, ov2, batch, seq):
    nc = seq // CMP_STRIDE
    ns = seq // SLC_BLOCK
    nqb = seq // Q_BLOCK
    rows = kvc.shape[2]
    gw = NSA_HPG * NSA_HEAD_DIM
    return pl.pallas_call(
        functools.partial(_nsa_cmp_kernel, nc=nc),
        out_shape=(jax.ShapeDtypeStruct((batch, seq, NSA_HEADS * NSA_HEAD_DIM), BF16),
                   jax.ShapeDtypeStruct((batch, NSA_KV_HEADS, seq, ns), F32)),
        grid=(batch, NSA_KV_HEADS, nqb),
        in_specs=[
            pl.BlockSpec((None, Q_BLOCK, gw), lambda b, g, i: (b, i, g)),
            pl.BlockSpec((None, None, rows, LANES), lambda b, g, i: (b, g, 0, 0)),
            pl.BlockSpec((None, None, rows, LANES), lambda b, g, i: (b, g, 0, 0)),
            pl.BlockSpec((None, NSA_HPG * Q_BLOCK, nc), lambda b, g, i: (g, 0, 0)),
            pl.BlockSpec((2 * nc, ns), lambda b, g, i: (0, 0)),
        ],
        out_specs=(pl.BlockSpec((None, Q_BLOCK, gw), lambda b, g, i: (b, i, g)),
                   pl.BlockSpec((None, None, Q_BLOCK, ns), lambda b, g, i: (b, g, i, 0))),
        compiler_params=_cparams(("parallel", "parallel", "parallel")),
        name="nsa_compressed",
    )(q, kvc, vkc, bias_c, ov2)


def _topk_kernel(imp_ref, sel_ref, *, seq, ns, ntop):
    rows = imp_ref.shape[0]
    r0 = pl.program_id(0) * rows
    rid = r0 + lax.broadcasted_iota(jnp.int32, (rows, ns), 0)
    t = rid & (seq - 1)
    qb = t >> int(math.log2(Q_BLOCK))
    js = lax.broadcasted_iota(jnp.int32, (rows, ns), 1)
    js_first = ns - 2 - 2 * qb
    js_cur = ns - 2 + ((t & (Q_BLOCK - 1)) >> int(math.log2(SLC_BLOCK)))
    causal = (js >= js_first) & (js <= js_cur)
    forced = (js == js_first) | (js == js_cur) | (js == js_cur - 1)
    ninf = -jnp.inf
    score = jnp.where(causal, imp_ref[...] + jnp.where(forced, FORCE_BONUS, 0.0), ninf)
    sel = jnp.zeros((rows, ns), F32)
    jsf = js.astype(F32)
    for _ in range(ntop):
        mx = jnp.max(score, axis=-1, keepdims=True)
        first = jnp.min(jnp.where(score == mx, jsf, float(ns)), axis=-1, keepdims=True)
        hit = (jsf == first) & (mx > ninf)
        sel = jnp.where(hit, 1.0, sel)
        score = jnp.where(hit, ninf, score)
    sel_ref[...] = sel.astype(sel_ref.dtype)


def _topk_select(imp, seq, *, rows=1024):
    b, g, t, ns = imp.shape
    n = b * g * t
    rows = min(rows, n)
    assert n % rows == 0
    sel = pl.pallas_call(
        functools.partial(_topk_kernel, seq=seq, ns=ns, ntop=min(SLC_TOP, ns)),
        out_shape=jax.ShapeDtypeStruct((n, ns), BF16),
        grid=(n // rows,),
        in_specs=[pl.BlockSpec((rows, ns), lambda i: (i, 0))],
        out_specs=pl.BlockSpec((rows, ns), lambda i: (i, 0)),
        compiler_params=_cparams(("parallel",)),
        name="nsa_topk",
    )(imp.reshape(n, ns))
    return sel.reshape(b, g, t, ns)


def _nsa_slc_win_kernel(q_ref, sel_ref, kvs_ref, vks_ref, kvw_ref, vkw_ref, far_ref, near_ref, bwin_ref,
                        eexp_ref, gates_ref, egate_ref, ocmp_ref, o_ref, m_ref, l_ref, acc_ref,
                        *, ns, kt_near, nkt):
    qb = pl.program_id(2)
    low = _head_masks()
    kw = SLC_KEY_TILE
    pad_rows = (ns - 2) * SLC_BLOCK
    kt0 = (ns - 2 - 2 * qb) // SLC_BLOCKS_PER_TILE

    m_ref[...] = jnp.full_like(m_ref, NEG_INF)
    l_ref[...] = jnp.zeros_like(l_ref)
    acc_ref[...] = jnp.zeros_like(acc_ref)

    qms = []
    for pair in range(NSA_HPG // 2):
        qp = q_ref[:, pair * LANES:(pair + 1) * LANES].astype(F32)
        qms.append(jnp.where(low, qp, 0.0).astype(BF16))
        qms.append(jnp.where(low, 0.0, qp).astype(BF16))

    irow = lax.broadcasted_iota(jnp.int32, (Q_BLOCK, kw), 0)
    ccol = lax.broadcasted_iota(jnp.int32, (Q_BLOCK, kw), 1)
    selb = sel_ref[...]

    def tile(kt, near):
        rs = pl.multiple_of(Q_BLOCK * qb + kw * kt, Q_BLOCK)
        kv_t = kvs_ref[pl.ds(rs, kw), :]
        vk_t = vks_ref[pl.ds(rs, kw), :]
        chosen = _dot(selb, eexp_ref[kt]) > 0.5
        mask = chosen & ((irow + pad_rows - kt * kw - ccol) >= 0)
        for hp in range(NSA_HPG):
            sub = hp % 2
            s = _dot_nt(qms[hp], kv_t if sub == 0 else vk_t)
            if near:
                s = s + near_ref[kt - kt_near, hp * Q_BLOCK:(hp + 1) * Q_BLOCK, :]
            else:
                s = s + far_ref[hp][0:1, 0:1]
            s = jnp.where(mask, s, NEG_INF)
            m_old = m_ref[hp][:, 0:1]
            m_new = jnp.maximum(m_old, jnp.max(s, axis=-1, keepdims=True))
            alpha = jnp.exp(m_old - m_new)
            p = jnp.where(mask, jnp.exp(s - m_new), 0.0)
            l_ref[hp] = alpha * l_ref[hp] + jnp.sum(p, axis=-1, keepdims=True)
            acc_ref[hp] = alpha * acc_ref[hp] + _dot(p.astype(BF16), vk_t if sub == 0 else kv_t)
            m_ref[hp] = jnp.broadcast_to(m_new, (Q_BLOCK, LANES))

    def far_body(kt, carry):
        tile(kt, False)
        return carry

    def near_body(kt, carry):
        tile(kt, True)
        return carry

    lax.fori_loop(kt0, kt_near, far_body, 0)
    lax.fori_loop(jnp.maximum(kt0, kt_near), nkt, near_body, 0)

    wk = WINDOW + Q_BLOCK
    ws = pl.multiple_of(Q_BLOCK * qb, Q_BLOCK)
    kv_w = kvw_ref[pl.ds(ws, wk), :]
    vk_w = vkw_ref[pl.ds(ws, wk), :]
    wcol = lax.broadcasted_iota(jnp.int32, (Q_BLOCK, wk), 1)
    in_seq = wcol >= (WINDOW - Q_BLOCK * qb)

    gh, gl = _split_bf16(gates_ref[...])
    g2 = jnp.concatenate([gh, gl], axis=1)
    gexp = [_dot(g2, egate_ref[br]) for br in range(N_BRANCH)]

    for pair in range(NSA_HPG // 2):
        o_slc, o_win = [], []
        for sub in range(2):
            hp = pair * 2 + sub
            den = l_ref[hp]
            o_slc.append(acc_ref[hp] * jnp.where(den > 0.0, 1.0 / den, 0.0))
            s = _dot_nt(qms[hp], kv_w if sub == 0 else vk_w) + bwin_ref[hp * Q_BLOCK:(hp + 1) * Q_BLOCK, :]
            s = jnp.where(in_seq, s, NEG_INF)
            keep = s > 0.5 * NEG_INF
            mx = jnp.max(s, axis=-1, keepdims=True)
            p = jnp.where(keep, jnp.exp(s - mx), 0.0)
            dw = jnp.sum(p, axis=-1, keepdims=True)
            p = p * jnp.where(dw > 0.0, 1.0 / dw, 0.0)
            o_win.append(_dot(p.astype(BF16), vk_w if sub == 0 else kv_w))
        cols = slice(pair * LANES, (pair + 1) * LANES)
        o = (gexp[0][:, cols] * ocmp_ref[:, cols].astype(F32)
             + gexp[1][:, cols] * jnp.where(low, o_slc[0], o_slc[1])
             + gexp[2][:, cols] * jnp.where(low, o_win[0], o_win[1]))
        o_ref[:, cols] = o.astype(o_ref.dtype)


def _nsa_slc_win(q, sel, kvs, vks, kvw, vkw, far, near, bwin, eexp, gates, egate, ocmp, batch, seq):
    ns = seq // SLC_BLOCK
    nqb = seq // Q_BLOCK
    nkt = (ns * SLC_BLOCK) // SLC_KEY_TILE
    kt_near = nkt - near.shape[1]
    gw = NSA_HPG * NSA_HEAD_DIM
    srows = kvs.shape[2]
    wrows = kvw.shape[2]
    slab = lambda r: pl.BlockSpec((None, None, r, LANES), lambda b, g, i: (b, g, 0, 0))
    return pl.pallas_call(
        functools.partial(_nsa_slc_win_kernel, ns=ns, kt_near=kt_near, nkt=nkt),
        out_shape=jax.ShapeDtypeStruct((batch, seq, NSA_HEADS * NSA_HEAD_DIM), BF16),
        grid=(batch, NSA_KV_HEADS, nqb),
        in_specs=[
            pl.BlockSpec((None, Q_BLOCK, gw), lambda b, g, i: (b, i, g)),
            pl.BlockSpec((None, None, Q_BLOCK, ns), lambda b, g, i: (b, g, i, 0)),
            slab(srows), slab(srows), slab(wrows), slab(wrows),
            pl.BlockSpec((None, NSA_HPG, SUBLANES, LANES), lambda b, g, i: (g, 0, 0, 0)),
            pl.BlockSpec((None,) + near.shape[1:], lambda b, g, i: (g, 0, 0, 0)),
            pl.BlockSpec((None,) + bwin.shape[1:], lambda b, g, i: (g, 0, 0)),
            pl.BlockSpec(eexp.shape, lambda b, g, i: (0, 0, 0)),
            pl.BlockSpec((None, Q_BLOCK, LANES), lambda b, g, i: (b, i, 0)),
            pl.BlockSpec((N_BRANCH, 2 * LANES, gw), lambda b, g, i: (0, 0, g)),
            pl.BlockSpec((None, Q_BLOCK, gw), lambda b, g, i: (b, i, g)),
        ],
        out_specs=pl.BlockSpec((None, Q_BLOCK, gw), lambda b, g, i: (b, i, g)),
        scratch_shapes=[pltpu.VMEM((NSA_HPG, Q_BLOCK, LANES), F32)] * 3,
        compiler_params=_cparams(("parallel", "parallel", "arbitrary")),
        name="nsa_selected_window",
    )(q, sel, kvs, vks, kvw, vkw, far, near, bwin, eexp, gates, egate, ocmp)


def _nsa_tables(rel_bias, seq):
    nc = seq // CMP_STRIDE
    ns = seq // SLC_BLOCK
    i = np.arange(Q_BLOCK)[:, None]
    j = np.arange(nc)[None, :]
    dist_c = i - (CMP_BLOCK - 1) - CMP_STRIDE * (j - (nc - CMP_LOOKBACK))
    bias_c = _bias_table(rel_bias, dist_c, dist_c >= 0)
    cs = np.arange(nc)[:, None] * CMP_STRIDE
    ss = np.arange(ns)[None, :] * SLC_BLOCK
    ov = np.clip(np.minimum(cs + CMP_BLOCK, ss + SLC_BLOCK) - np.maximum(cs, ss), 0, None) / CMP_BLOCK
    ov2 = jnp.asarray(np.concatenate([ov, ov], axis=0), BF16)
    nkeys = ns * SLC_BLOCK
    nkt = nkeys // SLC_KEY_TILE
    pad_rows = (ns - 2) * SLC_BLOCK
    kt_near = max(0, pad_rows - REL_MAX_DIST) // SLC_KEY_TILE
    kr = np.arange(kt_near * SLC_KEY_TILE, nkeys)[None, :]
    dist_s = i + pad_rows - kr
    near = _bias_table(rel_bias, dist_s, dist_s >= 0)
    near = near.reshape(NSA_KV_HEADS, NSA_HPG * Q_BLOCK, nkt - kt_near, SLC_KEY_TILE).transpose(0, 2, 1, 3)
    far = jnp.broadcast_to(rel_bias.astype(F32)[REL_BUCKETS - 1].reshape(NSA_KV_HEADS, NSA_HPG, 1, 1),
                           (NSA_KV_HEADS, NSA_HPG, SUBLANES, LANES))
    c = np.arange(WINDOW + Q_BLOCK)[None, :]
    dist_w = i + WINDOW - c
    bwin = _bias_table(rel_bias, dist_w, (dist_w >= 0) & (dist_w < WINDOW))
    blk = np.arange(ns)[None, :, None]
    key = np.arange(SLC_KEY_TILE)[None, None, :]
    ktile = np.arange(nkt)[:, None, None]
    eexp = jnp.asarray(blk == ktile * SLC_BLOCKS_PER_TILE + key // SLC_BLOCK, BF16)
    col = np.arange(LANES)[:, None]
    head = (np.arange(NSA_HEADS * NSA_HEAD_DIM) // NSA_HEAD_DIM)[None, :]
    eg = np.stack([col == head * N_BRANCH + br for br in range(N_BRANCH)])
    egate = jnp.asarray(np.concatenate([eg, eg], axis=1), BF16)
    return bias_c, ov2, far, near, bwin, eexp, egate


def _router_kernel(x_ref, g_ref, w_ref, o_ref):
    x = x_ref[...]
    ms = jnp.mean(x * x, axis=-1, keepdims=True)
    xn = x * lax.rsqrt(ms + NORM_EPS) * g_ref[...]
    logits = jnp.dot(xn, w_ref[...], preferred_element_type=F32, precision=lax.Precision.HIGHEST)
    lane = lax.broadcasted_iota(jnp.int32, logits.shape, 1).astype(F32)
    ninf = -jnp.inf
    s = jnp.where(lane < N_EXPERTS, logits, ninf)
    m1 = jnp.max(s, axis=-1, keepdims=True)
    i1 = jnp.min(jnp.where(s == m1, lane, float(LANES)), axis=-1, keepdims=True)
    s2 = jnp.where(lane == i1, ninf, s)
    m2 = jnp.max(s2, axis=-1, keepdims=True)
    i2 = jnp.min(jnp.where(s2 == m2, lane, float(LANES)), axis=-1, keepdims=True)
    e2 = jnp.exp(m2 - m1)
    w1 = 1.0 / (1.0 + e2)
    w2 = e2 * w1
    o_ref[...] = jnp.where(lane == i1, w1, 0.0) + jnp.where(lane == i2, w2, 0.0)


def _router(x, g, w_router, *, tm=512):
    n, d = x.shape
    tm = min(tm, n)
    wpad = jnp.zeros((d, LANES), F32).at[:, :N_EXPERTS].set(w_router.astype(F32))
    return pl.pallas_call(
        _router_kernel,
        out_shape=jax.ShapeDtypeStruct((n, LANES), F32),
        grid=(n // tm,),
        in_specs=[
            pl.BlockSpec((tm, d), lambda i: (i, 0)),
            pl.BlockSpec((1, d), lambda i: (0, 0)),
            pl.BlockSpec((d, LANES), lambda i: (0, 0)),
        ],
        out_specs=pl.BlockSpec((tm, LANES), lambda i: (i, 0)),
        compiler_params=_cparams(("parallel",)),
        name="moe_router",
    )(x, g.reshape(1, d).astype(F32), wpad)


def _nsa_layer(h, batch, seq, norm_g, kv_norm_g, w_kv, k_norm_g, cmp_pos_k, cmp_w1_k, cmp_w2_k,
               cmp_pos_v, cmp_w1_v, cmp_w2_v, w_in, w_out, q_norm_g, rel_bias):
    n, d = h.shape
    g_heads, dh = NSA_KV_HEADS, NSA_HEAD_DIM
    nc = seq // CMP_STRIDE
    ns = seq // SLC_BLOCK
    hq = NSA_HEADS * dh

    kvproj = _norm_matmul(h, kv_norm_g, w_kv.astype(BF16), name="nsa_kv_proj")

    kv6 = kvproj.reshape(batch, seq, N_BRANCH, 2, g_heads, dh)

    def chunks(t):
        return t.transpose(0, 2, 1, 3).reshape(batch, g_heads, nc, CMP_STRIDE * dh).astype(BF16)

    k_cmp = _compress(chunks(kv6[:, :, 0, 0]), cmp_w1_k, cmp_pos_k, cmp_w2_k, k_norm_g[0],
                      apply_norm=True, name="nsa_compress_k")
    v_cmp = _compress(chunks(kv6[:, :, 0, 1]), cmp_w1_v, cmp_pos_v, cmp_w2_v, k_norm_g[0],
                      apply_norm=False, name="nsa_compress_v")
    cpad = nc - CMP_LOOKBACK
    padc = lambda t: jnp.pad(t, ((0, 0), (0, 0), (cpad, 0), (0, 0)))
    kvc = padc(jnp.concatenate([k_cmp, v_cmp], axis=-1))
    vkc = padc(jnp.concatenate([v_cmp, k_cmp], axis=-1))

    kvs, vks = _pack_kv(kvproj, k_norm_g[1], 1, batch, seq, name="nsa_pack_selected")
    kvw, vkw = _pack_kv(kvproj, k_norm_g[2], 2, batch, seq, name="nsa_pack_window")
    spad = (ns - 2) * SLC_BLOCK
    pads = lambda t, r: jnp.pad(t, ((0, 0), (0, 0), (r, 0), (0, 0)))
    kvs, vks = pads(kvs, spad), pads(vks, spad)
    kvw, vkw = pads(kvw, WINDOW), pads(vkw, WINDOW)

    qgain = jnp.tile(q_norm_g.astype(F32), NSA_HEADS) * (dh ** -0.5)
    q = _norm_matmul(h, norm_g, w_in[:, :hq].astype(BF16), epilogue="headnorm", gain=qgain,
                     flag=jnp.ones((hq,), F32), out_dtype=BF16, name="nsa_q_proj")
    ngate = N_BRANCH * NSA_HEADS
    wg = jnp.zeros((d, LANES), BF16).at[:, :ngate].set(w_in[:, hq:].astype(BF16))
    gates = _norm_matmul(h, norm_g, wg, epilogue="sigmoid", name="nsa_gate_proj")

    bias_c, ov2, far, near, bwin, eexp, egate = _nsa_tables(rel_bias, seq)
    q3 = q.reshape(batch, seq, hq)
    o_cmp, imp = _nsa_cmp(q3, kvc, vkc, bias_c, ov2, batch, seq)
    sel = _topk_select(imp, seq)
    o = _nsa_slc_win(q3, sel, kvs, vks, kvw, vkw, far, near, bwin, eexp,
                     gates.reshape(batch, seq, LANES), egate, o_cmp, batch, seq)
    return _matmul_res(o.reshape(n, hq), w_out.astype(BF16), h, name="nsa_out_proj")


def kernel(x, norm_mix_g, norm_ffn_g, a_w_in, a_w_out, a_onorm_g, lb_param, kv_norm_g, w_kv, k_norm_g,
           cmp_pos_k, cmp_w1_k, cmp_w2_k, cmp_pos_v, cmp_w1_v, cmp_w2_v, b_w_in, b_w_out, b_qnorm_g,
           rel_bias, ffn_w_gu, ffn_w_down, moe_router, moe_w_gu, moe_w_down):
    batch, seq, d = x.shape
    n = batch * seq
    h = x.reshape(n, d).astype(F32)
    ones = jnp.ones((n, 1), F32)

    lower = jnp.cumsum(jax.nn.softmax(lb_param.astype(F32), axis=0), axis=0)[0]
    proj = _norm_matmul(h, norm_mix_g[0], a_w_in[0].astype(BF16), name="hgrn_in_proj")
    og = _hgrn_mixer(proj, lower, a_onorm_g[0], batch, seq)
    h = _matmul_res(og, a_w_out[0].astype(BF16), h, name="hgrn_out_proj")
    h = _ffn(h, norm_ffn_g[0], ffn_w_gu[0].astype(BF16), ffn_w_down[0].astype(BF16), ones, h, name="dense_ffn")

    h = _nsa_layer(h, batch, seq, norm_mix_g[1], kv_norm_g, w_kv, k_norm_g, cmp_pos_k, cmp_w1_k, cmp_w2_k,
                   cmp_pos_v, cmp_w1_v, cmp_w2_v, b_w_in[0], b_w_out[0], b_qnorm_g[0], rel_bias)
    gates = _router(h, norm_ffn_g[1], moe_router[0])
    out = h
    for e in range(N_EXPERTS):
        out = _ffn(h, norm_ffn_g[1], moe_w_gu[0, e].astype(BF16), moe_w_down[0, e].astype(BF16),
                   gates[:, e:e + 1], out, name=f"expert_ffn_{e}")
    return out.reshape(batch, seq, d).astype(x.dtype)
```

```python
import functools
import math

import jax
import jax.numpy as jnp
import numpy as np
from jax import lax
from jax.experimental import pallas as pl
from jax.experimental.pallas import tpu as pltpu

F32 = jnp.float32
BF16 = jnp.bfloat16

NORM_EPS = 1e-6
NEG_INF = -1e30
FORCE_BONUS = 1e4
HGRN_HEAD_DIM = 128
NSA_HEADS = 16
NSA_KV_HEADS = 4
NSA_HPG = NSA_HEADS // NSA_KV_HEADS
NSA_HEAD_DIM = 64
N_BRANCH = 3
CMP_BLOCK = 32
CMP_STRIDE = 16
SLC_BLOCK = 64
SLC_TOP = 16
WINDOW = 512
Q_BLOCK = 128
REL_BUCKETS = 32
REL_MAX_DIST = 2048
N_EXPERTS = 8

LANES = 128
SUBLANES = 8
VMEM_LIMIT_BYTES = 56 * 1024 * 1024

HGRN_CHUNK = 128
HGRN_ROWS_PER_STEP = 512
SLC_KEY_TILE = 512
SLC_BLOCKS_PER_TILE = SLC_KEY_TILE // SLC_BLOCK
CMP_LOOKBACK = Q_BLOCK // CMP_STRIDE


def _cparams(sem):
    return pltpu.CompilerParams(dimension_semantics=sem, vmem_limit_bytes=VMEM_LIMIT_BYTES)


def _dot(a, b):
    return jnp.dot(a, b, preferred_element_type=F32)


def _dot_nt(a, b):
    return lax.dot_general(a, b, (((1,), (1,)), ((), ())), preferred_element_type=F32)


def _split_bf16(x):
    hi = x.astype(BF16)
    lo = (x - hi.astype(F32)).astype(BF16)
    return hi, lo


def _norm_matmul_kernel(x_ref, g_ref, w_ref, *rest, epilogue):
    if epilogue == "headnorm":
        gain_ref, flag_ref, bd_ref, o_ref, xn_ref = rest
    else:
        o_ref, xn_ref = rest

    @pl.when(pl.program_id(1) == 0)
    def _():
        x = x_ref[...]
        ms = jnp.mean(x * x, axis=-1, keepdims=True)
        xn_ref[...] = (x * lax.rsqrt(ms + NORM_EPS) * g_ref[...]).astype(BF16)

    acc = _dot(xn_ref[...], w_ref[...])
    if epilogue == "headnorm":
        ss = _dot((acc * acc).astype(BF16), bd_ref[...])
        normed = acc * lax.rsqrt(ss * (1.0 / NSA_HEAD_DIM) + NORM_EPS) * gain_ref[...]
        acc = jnp.where(flag_ref[...] > 0.0, normed, acc)
    elif epilogue == "sigmoid":
        acc = jax.nn.sigmoid(acc)
    o_ref[...] = acc.astype(o_ref.dtype)


def _norm_matmul(x, g, w, *, epilogue="none", gain=None, flag=None, out_dtype=F32, tm=512, tn=512,
                 name="norm_matmul"):
    n, d = x.shape
    m = w.shape[1]
    tm = min(tm, n)
    tn = min(tn, m)
    assert n % tm == 0 and m % tn == 0, (n, tm, m, tn)
    in_specs = [
        pl.BlockSpec((tm, d), lambda i, j: (i, 0)),
        pl.BlockSpec((1, d), lambda i, j: (0, 0)),
        pl.BlockSpec((d, tn), lambda i, j: (0, j)),
    ]
    args = [x, g.reshape(1, d).astype(F32), w]
    if epilogue == "headnorm":
        assert tn % NSA_HEAD_DIM == 0
        grp = np.arange(tn) // NSA_HEAD_DIM
        bd = jnp.asarray(grp[:, None] == grp[None, :], BF16)
        in_specs += [
            pl.BlockSpec((1, tn), lambda i, j: (0, j)),
            pl.BlockSpec((1, tn), lambda i, j: (0, j)),
            pl.BlockSpec((tn, tn), lambda i, j: (0, 0)),
        ]
        args += [gain.reshape(1, m).astype(F32), flag.reshape(1, m).astype(F32), bd]
    return pl.pallas_call(
        functools.partial(_norm_matmul_kernel, epilogue=epilogue),
        out_shape=jax.ShapeDtypeStruct((n, m), out_dtype),
        grid=(n // tm, m // tn),
        in_specs=in_specs,
        out_specs=pl.BlockSpec((tm, tn), lambda i, j: (i, j)),
        scratch_shapes=[pltpu.VMEM((tm, d), BF16)],
        compiler_params=_cparams(("parallel", "arbitrary")),
        name=name,
    )(*args)


def _matmul_res_kernel(a_ref, w_ref, r_ref, o_ref):
    o_ref[...] = r_ref[...] + _dot(a_ref[...], w_ref[...])


def _matmul_res(a, w, res, *, tm=512, tn=512, name="matmul_res"):
    n, k = a.shape
    m = w.shape[1]
    tm = min(tm, n)
    tn = min(tn, m)
    assert n % tm == 0 and m % tn == 0
    return pl.pallas_call(
        _matmul_res_kernel,
        out_shape=jax.ShapeDtypeStruct((n, m), F32),
        grid=(n // tm, m // tn),
        in_specs=[
            pl.BlockSpec((tm, k), lambda i, j: (i, 0)),
            pl.BlockSpec((k, tn), lambda i, j: (0, j)),
            pl.BlockSpec((tm, tn), lambda i, j: (i, j)),
        ],
        out_specs=pl.BlockSpec((tm, tn), lambda i, j: (i, j)),
        compiler_params=_cparams(("parallel", "parallel")),
        name=name,
    )(a, w, res)


def _ffn_kernel(x_ref, g_ref, wg_ref, wu_ref, wd_ref, s_ref, r_ref, o_ref, xn_ref, acc_ref):
    f = pl.program_id(1)

    @pl.when(f == 0)
    def _():
        x = x_ref[...]
        ms = jnp.mean(x * x, axis=-1, keepdims=True)
        xn_ref[...] = (x * lax.rsqrt(ms + NORM_EPS) * g_ref[...]).astype(BF16)
        acc_ref[...] = jnp.zeros_like(acc_ref)

    xn = xn_ref[...]
    gate = _dot(xn, wg_ref[...])
    up = _dot(xn, wu_ref[...])
    act = (gate * jax.nn.sigmoid(gate) * up).astype(BF16)
    acc_ref[...] += _dot(act, wd_ref[...])

    @pl.when(f == pl.num_programs(1) - 1)
    def _():
        o_ref[...] = r_ref[...] + s_ref[...] * acc_ref[...]


def _ffn(x, g, w_gu, w_down, scale, res, *, tm=512, tf=None, name="ffn"):
    n, d = x.shape
    ff = w_down.shape[0]
    if tf is None:
        tf = ff // 2 if (ff // 2) % LANES == 0 else ff
    tm = min(tm, n)
    assert n % tm == 0 and ff % tf == 0 and tf % LANES == 0
    nf = ff // tf
    return pl.pallas_call(
        _ffn_kernel,
        out_shape=jax.ShapeDtypeStruct((n, d), F32),
        grid=(n // tm, nf),
        in_specs=[
            pl.BlockSpec((tm, d), lambda i, f: (i, 0)),
            pl.BlockSpec((1, d), lambda i, f: (0, 0)),
            pl.BlockSpec((d, tf), lambda i, f: (0, f)),
            pl.BlockSpec((d, tf), lambda i, f: (0, nf + f)),
            pl.BlockSpec((tf, d), lambda i, f: (f, 0)),
            pl.BlockSpec((tm, 1), lambda i, f: (i, 0)),
            pl.BlockSpec((tm, d), lambda i, f: (i, 0)),
        ],
        out_specs=pl.BlockSpec((tm, d), lambda i, f: (i, 0)),
        scratch_shapes=[pltpu.VMEM((tm, d), BF16), pltpu.VMEM((tm, d), F32)],
        compiler_params=_cparams(("parallel", "arbitrary")),
        name=name,
    )(x, g.reshape(1, d).astype(F32), w_gu, w_gu, w_down, scale, res)


def _hgrn_decay_matrix(c):
    levels = int(math.log2(c))
    out = np.zeros(((levels + 2) * c, c), np.float32)
    for l in range(levels):
        m = c >> (l + 1)
        for r in range(c):
            mid = (r // (2 * m)) * 2 * m + m - 1
            if r % (2 * m) >= m:
                out[l * c + r, mid + 1:r + 1] = 1.0
            else:
                out[l * c + r, r + 1:mid + 1] = 1.0
    for r in range(c):
        out[levels * c + r, :r + 1] = 1.0
        out[(levels + 1) * c + r, r + 1:] = 1.0
    return out


def _hgrn_kernel(q_ref, f_ref, v_ref, g_ref, lb_ref, gn_ref, m_ref, o_ref, st_ref, *, chunk, nchunk):
    c = chunk
    dh = HGRN_HEAD_DIM
    levels = int(math.log2(c))

    @pl.when(pl.program_id(2) == 0)
    def _():
        st_ref[...] = jnp.zeros_like(st_ref)

    lb = lb_ref[...]
    gn = gn_ref[...]
    row = lax.broadcasted_iota(jnp.int32, (c, c), 0)
    col = lax.broadcasted_iota(jnp.int32, (c, c), 1)
    rowv = lax.broadcasted_iota(jnp.int32, (c, dh), 0)

    def body(ci, carry):
        sl = pl.ds(pl.multiple_of(ci * c, c), c)
        q = q_ref[sl, :]
        v = v_ref[sl, :]
        fg = lb + (1.0 - lb) * jax.nn.sigmoid(f_ref[sl, :])
        k = 1.0 - fg
        lf = jnp.log(fg)
        hi, lo = _split_bf16(lf)
        d2 = _dot(m_ref[...], jnp.concatenate([hi, lo], axis=1))
        dall = d2[:, :dh] + d2[:, dh:]

        a = jnp.where(row == col, _dot_nt(q.astype(BF16), k.astype(BF16)), 0.0)
        for l in range(levels):
            sh = levels - 1 - l
            e = jnp.exp(dall[l * c:(l + 1) * c])
            upper = ((rowv >> sh) & 1) == 1
            qe = jnp.where(upper, q * e, 0.0).astype(BF16)
            ke = jnp.where(upper, 0.0, k * e).astype(BF16)
            same = (row >> (sh + 1)) == (col >> (sh + 1))
            a = a + jnp.where(same, _dot_nt(qe, ke), 0.0)

        b = dall[levels * c:(levels + 1) * c]
        r = dall[(levels + 1) * c:]
        st = st_ref[...]
        o = _dot(a.astype(BF16), v.astype(BF16)) + _dot_nt((q * jnp.exp(b)).astype(BF16), st.astype(BF16))
        kr = (k * jnp.exp(r)).astype(BF16)
        st_ref[...] = st * jnp.exp(b[c - 1:c, :]) + _dot(v.T.astype(BF16), kr)

        ms = jnp.mean(o * o, axis=-1, keepdims=True)
        on = o * lax.rsqrt(ms + NORM_EPS) * gn
        gg = g_ref[sl, :]
        o_ref[sl, :] = (on * (gg * jax.nn.sigmoid(gg))).astype(o_ref.dtype)
        return carry

    lax.fori_loop(0, nchunk, body, 0)


def _hgrn_mixer(proj, lb, gn, batch, seq):
    n, d4 = proj.shape
    d = d4 // 4
    heads = d // HGRN_HEAD_DIM
    rows = min(HGRN_ROWS_PER_STEP, seq)
    chunk = min(HGRN_CHUNK, rows)
    assert seq % rows == 0 and rows % chunk == 0
    nt = seq // rows
    m = jnp.asarray(_hgrn_decay_matrix(chunk), BF16)

    def col_spec(s):
        return pl.BlockSpec((rows, HGRN_HEAD_DIM), lambda b, h, t, s=s: (b * nt + t, s * heads + h))

    return pl.pallas_call(
        functools.partial(_hgrn_kernel, chunk=chunk, nchunk=rows // chunk),
        out_shape=jax.ShapeDtypeStruct((n, d), BF16),
        grid=(batch, heads, nt),
        in_specs=[
            col_spec(0), col_spec(1), col_spec(2), col_spec(3),
            pl.BlockSpec((1, HGRN_HEAD_DIM), lambda b, h, t: (0, h)),
            pl.BlockSpec((1, HGRN_HEAD_DIM), lambda b, h, t: (0, 0)),
            pl.BlockSpec(m.shape, lambda b, h, t: (0, 0)),
        ],
        out_specs=pl.BlockSpec((rows, HGRN_HEAD_DIM), lambda b, h, t: (b * nt + t, h)),
        scratch_shapes=[pltpu.VMEM((HGRN_HEAD_DIM, HGRN_HEAD_DIM), F32)],
        compiler_params=_cparams(("parallel", "parallel", "arbitrary")),
        name="hgrn2_recurrence",
    )(proj, proj, proj, proj, lb.reshape(1, d).astype(F32), gn.reshape(1, HGRN_HEAD_DIM).astype(F32), m)


def _compress_kernel(x_ref, w1c_ref, pos_ref, w1_ref, w2_ref, gain_ref, o_ref, *, apply_norm):
    hid = w2_ref.shape[0]
    nc = x_ref.shape[0]
    uv = _dot(x_ref[...], w1c_ref[...])
    posb = _dot(pos_ref[...], w1_ref[...])[0:1, :]
    pre = uv[:, :hid] + pltpu.roll(uv[:, hid:], nc - 1, 0) + posb
    out = _dot(jax.nn.gelu(pre).astype(BF16), w2_ref[...])
    if apply_norm:
        ms = jnp.mean(out * out, axis=-1, keepdims=True)
        out = out * lax.rsqrt(ms + NORM_EPS) * gain_ref[...]
    o_ref[...] = out


def _compress(x, w1, pos, w2, gain, *, apply_norm, name):
    b, g, nc, half = x.shape
    hid = w1.shape[1]
    dh = w2.shape[1]
    w1c = jnp.concatenate([w1[:half], w1[half:]], axis=1).astype(BF16)
    posr = jnp.broadcast_to(pos.reshape(1, -1), (SUBLANES, pos.size)).astype(BF16)
    return pl.pallas_call(
        functools.partial(_compress_kernel, apply_norm=apply_norm),
        out_shape=jax.ShapeDtypeStruct((b, g, nc, dh), F32),
        grid=(b, g),
        in_specs=[
            pl.BlockSpec((None, None, nc, half), lambda i, j: (i, j, 0, 0)),
            pl.BlockSpec((half, 2 * hid), lambda i, j: (0, 0)),
            pl.BlockSpec((SUBLANES, 2 * half), lambda i, j: (0, 0)),
            pl.BlockSpec((2 * half, hid), lambda i, j: (0, 0)),
            pl.BlockSpec((hid, dh), lambda i, j: (0, 0)),
            pl.BlockSpec((1, dh), lambda i, j: (0, 0)),
        ],
        out_specs=pl.BlockSpec((None, None, nc, dh), lambda i, j: (i, j, 0, 0)),
        compiler_params=_cparams(("parallel", "parallel")),
        name=name,
    )(x, w1c, posr, w1.astype(BF16), w2.astype(BF16), gain.reshape(1, dh).astype(F32))


def _pack_kv_kernel(k_ref, v_ref, gain_ref, kk_ref, v1_ref):
    kp = k_ref[...]
    vp = v_ref[...]
    lane = lax.broadcasted_iota(jnp.int32, kp.shape, 1)
    low = lane < NSA_HEAD_DIM
    sq = kp * kp
    ss_lo = jnp.sum(jnp.where(low, sq, 0.0), axis=-1, keepdims=True)
    ss_hi = jnp.sum(jnp.where(low, 0.0, sq), axis=-1, keepdims=True)
    ms = jnp.where(low, ss_lo, ss_hi) * (1.0 / NSA_HEAD_DIM)
    kn = kp * lax.rsqrt(ms + NORM_EPS) * gain_ref[...]
    kr = pltpu.roll(kn, NSA_HEAD_DIM, 1)
    vr = pltpu.roll(vp, NSA_HEAD_DIM, 1)
    kk_ref[0] = jnp.where(low, kn, kr).astype(kk_ref.dtype)
    kk_ref[1] = jnp.where(low, kr, kn).astype(kk_ref.dtype)
    v1_ref[0] = jnp.where(low, vp, 1.0).astype(v1_ref.dtype)
    v1_ref[1] = jnp.where(low, vr, 1.0).astype(v1_ref.dtype)


def _pack_kv(kvproj, gain, branch, batch, seq, *, tm=512, name="pack_kv"):
    tm = min(tm, seq)
    nt = seq // tm
    pairs = NSA_KV_HEADS // 2
    kcol = branch * 2 * pairs
    vcol = kcol + pairs
    gain2 = jnp.tile(gain.reshape(1, NSA_HEAD_DIM), (1, 2)).astype(F32)
    out = jax.ShapeDtypeStruct((batch, NSA_KV_HEADS, seq, LANES), BF16)
    ospec = pl.BlockSpec((None, 2, tm, LANES), lambda b, t, p: (b, p, t, 0))
    return pl.pallas_call(
        _pack_kv_kernel,
        out_shape=(out, out),
        grid=(batch, nt, pairs),
        in_specs=[
            pl.BlockSpec((tm, LANES), lambda b, t, p: (b * nt + t, kcol + p)),
            pl.BlockSpec((tm, LANES), lambda b, t, p: (b * nt + t, vcol + p)),
            pl.BlockSpec((1, LANES), lambda b, t, p: (0, 0)),
        ],
        out_specs=(ospec, ospec),
        compiler_params=_cparams(("parallel", "parallel", "parallel")),
        name=name,
    )(kvproj, kvproj, gain2)


def _rel_bucket_np(dist):
    max_exact = REL_BUCKETS // 2
    d = np.maximum(dist, 0)
    large = max_exact + (np.log(np.maximum(d, 1).astype(np.float32) / max_exact)
                         / math.log(REL_MAX_DIST / max_exact) * (REL_BUCKETS - max_exact)).astype(np.int32)
    large = np.minimum(large, REL_BUCKETS - 1)
    return np.where(d < max_exact, d, large).astype(np.int32)


def _bias_by_distance(rel_bias, dist, valid):
    tb = rel_bias.astype(F32).T
    vals = tb[:, _rel_bucket_np(dist)]
    return jnp.where(jnp.asarray(valid)[None], vals, NEG_INF)


def _hankel(vec, nrow, ncol):
    lg = nrow + ncol - 1
    g = vec[..., :lg]
    flat = jnp.tile(g, (1,) * (g.ndim - 1) + (nrow + 1,))[..., :nrow * (lg + 1)]
    return flat.reshape(g.shape[:-1] + (nrow, lg + 1))[..., :ncol]


def _toeplitz_bias(rel_bias, ncol, max_valid):
    d = np.arange(Q_BLOCK + ncol - 1) - (Q_BLOCK - 1)
    vec = _bias_by_distance(rel_bias, d, (d >= 0) & (d < max_valid))
    tab = _hankel(vec, Q_BLOCK, ncol)[..., ::-1]
    return tab.reshape(NSA_KV_HEADS, NSA_HPG * Q_BLOCK, ncol)


def _head_masks():
    lane = lax.broadcasted_iota(jnp.int32, (Q_BLOCK, LANES), 1)
    return lane < NSA_HEAD_DIM


def _stack_heads(q_ref, low):
    parts = []
    for pair in range(NSA_HPG // 2):
        qp = q_ref[:, pair * LANES:(pair + 1) * LANES].astype(F32)
        parts.append(jnp.where(low, qp, 0.0))
        parts.append(jnp.where(low, 0.0, qp))
    return jnp.concatenate(parts, axis=0).astype(BF16)


def _unstack_heads(x, low):
    parts = []
    for pair in range(NSA_HPG // 2):
        a = x[(2 * pair) * Q_BLOCK:(2 * pair + 1) * Q_BLOCK]
        b = x[(2 * pair + 1) * Q_BLOCK:(2 * pair + 2) * Q_BLOCK]
        parts.append(jnp.where(low, a, pltpu.roll(b, NSA_HEAD_DIM, 1)))
    return jnp.concatenate(parts, axis=1)


def _normalize_pv(pv):
    den = pltpu.roll(pv, NSA_HEAD_DIM, 1)
    return pv * jnp.where(den > 0.0, 1.0 / den, 0.0)


def _nsa_cmp_kernel(q_ref, kk_ref, v1_ref, bias_ref, ov_ref, o_ref, imp_ref, *, nc):
    qb = pl.program_id(2)
    start = pl.multiple_of(qb * CMP_LOOKBACK, SUBLANES)
    kk = kk_ref[pl.ds(start, nc), :].astype(BF16)
    v1 = v1_ref[pl.ds(start, nc), :].astype(BF16)
    low = _head_masks()
    jcol = lax.broadcasted_iota(jnp.int32, (Q_BLOCK, nc), 1)
    exists = jcol >= (nc - CMP_LOOKBACK - CMP_LOOKBACK * qb)
    s4 = _dot_nt(_stack_heads(q_ref, low), kk)
    psum = jnp.zeros((Q_BLOCK, nc), F32)
    ps = []
    for hp in range(NSA_HPG):
        rows = slice(hp * Q_BLOCK, (hp + 1) * Q_BLOCK)
        s = jnp.where(exists, s4[rows] + bias_ref[rows, :], NEG_INF)
        keep = s > 0.5 * NEG_INF
        mx = jnp.max(s, axis=-1, keepdims=True)
        p = jnp.where(keep, jnp.exp(s - mx), 0.0)
        den = jnp.sum(p, axis=-1, keepdims=True)
        p = p * jnp.where(den > 0.0, 1.0 / den, 0.0)
        psum = psum + p
        ps.append(p.astype(BF16))
    pv = _dot(jnp.concatenate(ps, axis=0), v1)
    o_ref[...] = _unstack_heads(pv, low).astype(o_ref.dtype)
    hi, lo = _split_bf16(psum)
    imp_ref[...] = _dot(jnp.concatenate([hi, lo], axis=1), ov_ref[...])


def _nsa_cmp(q, kvc, vkc, bias_c, ov2, batch, seq):
    nc = seq // CMP_STRIDE
    ns = seq // SLC_BLOCK
    nqb = seq // Q_BLOCK
    rows = kvc.shape[2]
    gw = NSA_HPG * NSA_HEAD_DIM
    return pl.pallas_call(
        functools.partial(_nsa_cmp_kernel, nc=nc),
        out_shape=(jax.ShapeDtypeStruct((batch, seq, NSA_HEADS * NSA_HEAD_DIM), BF16),
                   jax.ShapeDtypeStruct((batch, NSA_KV_HEADS, seq, ns), F32)),
        grid=(batch, NSA_KV_HEADS, nqb),
        in_specs=[
            pl.BlockSpec((None, Q_BLOCK, gw), lambda b, g, i: (b, i, g)),
            pl.BlockSpec((None, None, rows, LANES), lambda b, g, i: (b, g, 0, 0)),
            pl.BlockSpec((None, None, rows, LANES), lambda b, g, i: (b, g, 0, 0)),
            pl.BlockSpec((None, NSA_HPG * Q_BLOCK, nc), lambda b, g, i: (g, 0, 0)),
            pl.BlockSpec((2 * nc, ns), lambda b, g, i: (0, 0)),
        ],
        out_specs=(pl.BlockSpec((None, Q_BLOCK, gw), lambda b, g, i: (b, i, g)),
                   pl.BlockSpec((None, None, Q_BLOCK, ns), lambda b, g, i: (b, g, i, 0))),
        compiler_params=_cparams(("parallel", "parallel", "parallel")),
        name="nsa_compressed",
    )(q, kvc, vkc, bias_c, ov2)


def _topk_kernel(imp_ref, sel_ref, *, seq, ns, ntop):
    rows = imp_ref.shape[0]
    r0 = pl.program_id(0) * rows
    rid = r0 + lax.broadcasted_iota(jnp.int32, (rows, ns), 0)
    t = rid & (seq - 1)
    qb = t >> int(math.log2(Q_BLOCK))
    js = lax.broadcasted_iota(jnp.int32, (rows, ns), 1)
    js_first = ns - 2 - 2 * qb
    js_cur = ns - 2 + ((t & (Q_BLOCK - 1)) >> int(math.log2(SLC_BLOCK)))
    causal = (js >= js_first) & (js <= js_cur)
    forced = (js == js_first) | (js == js_cur) | (js == js_cur - 1)
    ninf = -jnp.inf
    score = jnp.where(causal, imp_ref[...] + jnp.where(forced, FORCE_BONUS, 0.0), ninf)
    sel = jnp.zeros((rows, ns), F32)
    jsf = js.astype(F32)
    for _ in range(ntop):
        mx = jnp.max(score, axis=-1, keepdims=True)
        first = jnp.min(jnp.where(score == mx, jsf, float(ns)), axis=-1, keepdims=True)
        hit = (jsf == first) & (mx > ninf)
        sel = jnp.where(hit, 1.0, sel)
        score = jnp.where(hit, ninf, score)
    sel_ref[...] = sel.astype(sel_ref.dtype)


def _topk_select(imp, seq, *, rows=1024):
    b, g, t, ns = imp.shape
    n = b * g * t
    rows = min(rows, n)
    assert n % rows == 0
    sel = pl.pallas_call(
        functools.partial(_topk_kernel, seq=seq, ns=ns, ntop=min(SLC_TOP, ns)),
        out_shape=jax.ShapeDtypeStruct((n, ns), BF16),
        grid=(n // rows,),
        in_specs=[pl.BlockSpec((rows, ns), lambda i: (i, 0))],
        out_specs=pl.BlockSpec((rows, ns), lambda i: (i, 0)),
        compiler_params=_cparams(("parallel",)),
        name="nsa_topk",
    )(imp.reshape(n, ns))
    return sel.reshape(b, g, t, ns)


def _nsa_slc_win_kernel(q_ref, sel_ref, kks_ref, v1s_ref, kkw_ref, v1w_ref, bias_ref, bwin_ref,
                        eexp_ref, gates_ref, egate_ref, ocmp_ref, o_ref, m_ref, acc_ref, p_ref, alpha_ref,
                        *, ns, kt_near, nkt):
    qb = pl.program_id(2)
    low = _head_masks()
    kw = SLC_KEY_TILE
    pad_rows = (ns - 2) * SLC_BLOCK
    kt0 = (ns - 2 - 2 * qb) // SLC_BLOCKS_PER_TILE

    m_ref[...] = jnp.full_like(m_ref, NEG_INF)
    acc_ref[...] = jnp.zeros_like(acc_ref)
    p_ref[...] = jnp.zeros_like(p_ref)
    alpha_ref[...] = jnp.ones_like(alpha_ref)
    q4 = _stack_heads(q_ref, low)
    irow = lax.broadcasted_iota(jnp.int32, (Q_BLOCK, kw), 0)
    ccol = lax.broadcasted_iota(jnp.int32, (Q_BLOCK, kw), 1)
    selb = sel_ref[...]

    def accumulate(kt):
        rs = pl.multiple_of(jnp.maximum(Q_BLOCK * qb + kw * kt, 0), Q_BLOCK)
        pv = _dot(p_ref[...], v1s_ref[pl.ds(rs, kw), :])
        acc_ref[...] = alpha_ref[...] * acc_ref[...] + pv

    def body(kt, carry):
        rs = pl.multiple_of(Q_BLOCK * qb + kw * kt, Q_BLOCK)
        s4 = _dot_nt(q4, kks_ref[pl.ds(rs, kw), :])
        mb = (_dot(selb, eexp_ref[kt]) - 1.0) * (-NEG_INF)
        mb = jnp.where((irow + pad_rows - kt * kw - ccol) >= 0, mb, NEG_INF)
        accumulate(kt - 1)
        bt = jnp.maximum(kt - kt_near + 1, 0)
        for hp in range(NSA_HPG):
            rows = slice(hp * Q_BLOCK, (hp + 1) * Q_BLOCK)
            s = s4[rows] + bias_ref[bt, rows, :] + mb
            m_old = m_ref[rows, :]
            m_new = jnp.maximum(m_old, jnp.max(s, axis=-1, keepdims=True))
            alpha_ref[rows, :] = jnp.exp(m_old - m_new)
            p_ref[rows, :] = jnp.exp(s - m_new[:, 0:1]).astype(BF16)
            m_ref[rows, :] = m_new
        return carry

    lax.fori_loop(kt0, nkt, body, 0)
    accumulate(nkt - 1)
    o_slc = _unstack_heads(_normalize_pv(acc_ref[...]), low)

    wk = WINDOW + Q_BLOCK
    ws = pl.multiple_of(Q_BLOCK * qb, Q_BLOCK)
    sw = _dot_nt(q4, kkw_ref[pl.ds(ws, wk), :])
    wcol = lax.broadcasted_iota(jnp.int32, (Q_BLOCK, wk), 1)
    in_seq = wcol >= (WINDOW - Q_BLOCK * qb)
    ps = []
    for hp in range(NSA_HPG):
        rows = slice(hp * Q_BLOCK, (hp + 1) * Q_BLOCK)
        s = jnp.where(in_seq, sw[rows] + bwin_ref[rows, :], NEG_INF)
        ps.append(jnp.exp(s - jnp.max(s, axis=-1, keepdims=True)).astype(BF16))
    pvw = _dot(jnp.concatenate(ps, axis=0), v1w_ref[pl.ds(ws, wk), :])
    o_win = _unstack_heads(_normalize_pv(pvw), low)

    gh, gl = _split_bf16(gates_ref[...])
    g2 = jnp.concatenate([gh, gl], axis=1)
    o = (_dot(g2, egate_ref[0]) * ocmp_ref[...].astype(F32)
         + _dot(g2, egate_ref[1]) * o_slc
         + _dot(g2, egate_ref[2]) * o_win)
    o_ref[...] = o.astype(o_ref.dtype)


def _nsa_slc_win(q, sel, kvs, vks, kvw, vkw, bias_s, bwin, eexp, gates, egate, ocmp, batch, seq):
    ns = seq // SLC_BLOCK
    nqb = seq // Q_BLOCK
    nkt = (ns * SLC_BLOCK) // SLC_KEY_TILE
    kt_near = nkt - (bias_s.shape[1] - 1)
    gw = NSA_HPG * NSA_HEAD_DIM
    srows = kvs.shape[2]
    wrows = kvw.shape[2]
    slab = lambda r: pl.BlockSpec((None, None, r, LANES), lambda b, g, i: (b, g, 0, 0))
    return pl.pallas_call(
        functools.partial(_nsa_slc_win_kernel, ns=ns, kt_near=kt_near, nkt=nkt),
        out_shape=jax.ShapeDtypeStruct((batch, seq, NSA_HEADS * NSA_HEAD_DIM), BF16),
        grid=(batch, NSA_KV_HEADS, nqb),
        in_specs=[
            pl.BlockSpec((None, Q_BLOCK, gw), lambda b, g, i: (b, i, g)),
            pl.BlockSpec((None, None, Q_BLOCK, ns), lambda b, g, i: (b, g, i, 0)),
            slab(srows), slab(srows), slab(wrows), slab(wrows),
            pl.BlockSpec((None,) + bias_s.shape[1:], lambda b, g, i: (g, 0, 0, 0)),
            pl.BlockSpec((None,) + bwin.shape[1:], lambda b, g, i: (g, 0, 0)),
            pl.BlockSpec(eexp.shape, lambda b, g, i: (0, 0, 0)),
            pl.BlockSpec((None, Q_BLOCK, LANES), lambda b, g, i: (b, i, 0)),
            pl.BlockSpec((N_BRANCH, 2 * LANES, gw), lambda b, g, i: (0, 0, g)),
            pl.BlockSpec((None, Q_BLOCK, gw), lambda b, g, i: (b, i, g)),
        ],
        out_specs=pl.BlockSpec((None, Q_BLOCK, gw), lambda b, g, i: (b, i, g)),
        scratch_shapes=[pltpu.VMEM((NSA_HPG * Q_BLOCK, LANES), F32),
                        pltpu.VMEM((NSA_HPG * Q_BLOCK, LANES), F32),
                        pltpu.VMEM((NSA_HPG * Q_BLOCK, SLC_KEY_TILE), BF16),
                        pltpu.VMEM((NSA_HPG * Q_BLOCK, LANES), F32)],
        compiler_params=_cparams(("parallel", "parallel", "arbitrary")),
        name="nsa_selected_window",
    )(q, sel, kvs, vks, kvw, vkw, bias_s, bwin, eexp, gates, egate, ocmp)


def _nsa_tables(rel_bias, seq):
    nc = seq // CMP_STRIDE
    ns = seq // SLC_BLOCK
    off = CMP_STRIDE * (CMP_LOOKBACK - 1) + CMP_BLOCK - 1
    x = np.arange(CMP_LOOKBACK + nc - 1)[None, :]
    d = CMP_STRIDE * x + np.arange(CMP_STRIDE)[:, None] - off
    vec = _bias_by_distance(rel_bias, d, d >= 0)
    bias_c = _hankel(vec, CMP_LOOKBACK, nc)[..., ::-1]
    bias_c = bias_c.transpose(0, 2, 1, 3).reshape(NSA_KV_HEADS, NSA_HPG * Q_BLOCK, nc)
    cs = np.arange(nc)[:, None] * CMP_STRIDE
    ss = np.arange(ns)[None, :] * SLC_BLOCK
    ov = np.clip(np.minimum(cs + CMP_BLOCK, ss + SLC_BLOCK) - np.maximum(cs, ss), 0, None) / CMP_BLOCK
    ov2 = jnp.asarray(np.concatenate([ov, ov], axis=0), BF16)
    nkeys = ns * SLC_BLOCK
    nkt = nkeys // SLC_KEY_TILE
    pad_rows = (ns - 2) * SLC_BLOCK
    kt_near = max(0, pad_rows - REL_MAX_DIST) // SLC_KEY_TILE
    near = _toeplitz_bias(rel_bias, nkeys - kt_near * SLC_KEY_TILE, 2 * seq)
    near = near.reshape(NSA_KV_HEADS, NSA_HPG * Q_BLOCK, nkt - kt_near, SLC_KEY_TILE).transpose(0, 2, 1, 3)
    far = jnp.broadcast_to(rel_bias.astype(F32)[REL_BUCKETS - 1].reshape(NSA_KV_HEADS, 1, NSA_HPG, 1, 1),
                           (NSA_KV_HEADS, 1, NSA_HPG, Q_BLOCK, SLC_KEY_TILE))
    bias_s = jnp.concatenate([far.reshape(NSA_KV_HEADS, 1, NSA_HPG * Q_BLOCK, SLC_KEY_TILE), near], axis=1)
    bwin = _toeplitz_bias(rel_bias, WINDOW + Q_BLOCK, WINDOW)
    blk = np.arange(ns)[None, :, None]
    key = np.arange(SLC_KEY_TILE)[None, None, :]
    ktile = np.arange(nkt)[:, None, None]
    eexp = jnp.asarray(blk == ktile * SLC_BLOCKS_PER_TILE + key // SLC_BLOCK, BF16)
    col = np.arange(LANES)[:, None]
    head = (np.arange(NSA_HEADS * NSA_HEAD_DIM) // NSA_HEAD_DIM)[None, :]
    eg = np.stack([col == head * N_BRANCH + br for br in range(N_BRANCH)])
    egate = jnp.asarray(np.concatenate([eg, eg], axis=1), BF16)
    return bias_c, ov2, bias_s, bwin, eexp, egate


def _router_kernel(x_ref, g_ref, w_ref, o_ref):
    x = x_ref[...]
    ms = jnp.mean(x * x, axis=-1, keepdims=True)
    xn = x * lax.rsqrt(ms + NORM_EPS) * g_ref[...]
    logits = jnp.dot(xn, w_ref[...], preferred_element_type=F32, precision=lax.Precision.HIGHEST)
    lane = lax.broadcasted_iota(jnp.int32, logits.shape, 1).astype(F32)
    ninf = -jnp.inf
    s = jnp.where(lane < N_EXPERTS, logits, ninf)
    m1 = jnp.max(s, axis=-1, keepdims=True)
    i1 = jnp.min(jnp.where(s == m1, lane, float(LANES)), axis=-1, keepdims=True)
    s2 = jnp.where(lane == i1, ninf, s)
    m2 = jnp.max(s2, axis=-1, keepdims=True)
    i2 = jnp.min(jnp.where(s2 == m2, lane, float(LANES)), axis=-1, keepdims=True)
    e2 = jnp.exp(m2 - m1)
    w1 = 1.0 / (1.0 + e2)
    w2 = e2 * w1
    o_ref[...] = jnp.where(lane == i1, w1, 0.0) + jnp.where(lane == i2, w2, 0.0)


def _router(x, g, w_router, *, tm=512):
    n, d = x.shape
    tm = min(tm, n)
    wpad = jnp.zeros((d, LANES), F32).at[:, :N_EXPERTS].set(w_router.astype(F32))
    return pl.pallas_call(
        _router_kernel,
        out_shape=jax.ShapeDtypeStruct((n, LANES), F32),
        grid=(n // tm,),
        in_specs=[
            pl.BlockSpec((tm, d), lambda i: (i, 0)),
            pl.BlockSpec((1, d), lambda i: (0, 0)),
            pl.BlockSpec((d, LANES), lambda i: (0, 0)),
        ],
        out_specs=pl.BlockSpec((tm, LANES), lambda i: (i, 0)),
        compiler_params=_cparams(("parallel",)),
        name="moe_router",
    )(x, g.reshape(1, d).astype(F32), wpad)


def _nsa_layer(h, batch, seq, norm_g, kv_norm_g, w_kv, k_norm_g, cmp_pos_k, cmp_w1_k, cmp_w2_k,
               cmp_pos_v, cmp_w1_v, cmp_w2_v, w_in, w_out, q_norm_g, rel_bias):
    n, d = h.shape
    g_heads, dh = NSA_KV_HEADS, NSA_HEAD_DIM
    nc = seq // CMP_STRIDE
    ns = seq // SLC_BLOCK
    hq = NSA_HEADS * dh

    kvproj = _norm_matmul(h, kv_norm_g, w_kv.astype(BF16), name="nsa_kv_proj")

    kv6 = kvproj.reshape(batch, seq, N_BRANCH, 2, g_heads, dh)

    def chunks(t):
        return t.transpose(0, 2, 1, 3).reshape(batch, g_heads, nc, CMP_STRIDE * dh).astype(BF16)

    k_cmp = _compress(chunks(kv6[:, :, 0, 0]), cmp_w1_k, cmp_pos_k, cmp_w2_k, k_norm_g[0],
                      apply_norm=True, name="nsa_compress_k")
    v_cmp = _compress(chunks(kv6[:, :, 0, 1]), cmp_w1_v, cmp_pos_v, cmp_w2_v, k_norm_g[0],
                      apply_norm=False, name="nsa_compress_v")
    cpad = nc - CMP_LOOKBACK
    padc = lambda t: jnp.pad(t, ((0, 0), (0, 0), (cpad, 0), (0, 0)))
    kvc = padc(jnp.concatenate([k_cmp, k_cmp], axis=-1))
    vkc = padc(jnp.concatenate([v_cmp, jnp.ones_like(v_cmp)], axis=-1))

    kvs, vks = _pack_kv(kvproj, k_norm_g[1], 1, batch, seq, name="nsa_pack_selected")
    kvw, vkw = _pack_kv(kvproj, k_norm_g[2], 2, batch, seq, name="nsa_pack_window")
    spad = (ns - 2) * SLC_BLOCK
    pads = lambda t, r: jnp.pad(t, ((0, 0), (0, 0), (r, 0), (0, 0)))
    kvs, vks = pads(kvs, spad), pads(vks, spad)
    kvw, vkw = pads(kvw, WINDOW), pads(vkw, WINDOW)

    qgain = jnp.tile(q_norm_g.astype(F32), NSA_HEADS) * (dh ** -0.5)
    q = _norm_matmul(h, norm_g, w_in[:, :hq].astype(BF16), epilogue="headnorm", gain=qgain,
                     flag=jnp.ones((hq,), F32), out_dtype=BF16, name="nsa_q_proj")
    ngate = N_BRANCH * NSA_HEADS
    wg = jnp.zeros((d, LANES), BF16).at[:, :ngate].set(w_in[:, hq:].astype(BF16))
    gates = _norm_matmul(h, norm_g, wg, epilogue="sigmoid", name="nsa_gate_proj")

    bias_c, ov2, bias_s, bwin, eexp, egate = _nsa_tables(rel_bias, seq)
    q3 = q.reshape(batch, seq, hq)
    o_cmp, imp = _nsa_cmp(q3, kvc, vkc, bias_c, ov2, batch, seq)
    sel = _topk_select(imp, seq)
    o = _nsa_slc_win(q3, sel, kvs, vks, kvw, vkw, bias_s, bwin, eexp,
                     gates.reshape(batch, seq, LANES), egate, o_cmp, batch, seq)
    return _matmul_res(o.reshape(n, hq), w_out.astype(BF16), h, name="nsa_out_proj")


def kernel(x, norm_mix_g, norm_ffn_g, a_w_in, a_w_out, a_onorm_g, lb_param, kv_norm_g, w_kv, k_norm_g,
           cmp_pos_k, cmp_w1_k, cmp_w2_k, cmp_pos_v, cmp_w1_v, cmp_w2_v, b_w_in, b_w_out, b_qnorm_g,
           rel_bias, ffn_w_gu, ffn_w_down, moe_router, moe_w_gu, moe_w_down):
    batch, seq, d = x.shape
    n = batch * seq
    h = x.reshape(n, d).astype(F32)
    ones = jnp.ones((n, 1), F32)

    lower = jnp.cumsum(jax.nn.softmax(lb_param.astype(F32), axis=0), axis=0)[0]
    proj = _norm_matmul(h, norm_mix_g[0], a_w_in[0].astype(BF16), name="hgrn_in_proj")
    og = _hgrn_mixer(proj, lower, a_onorm_g[0], batch, seq)
    h = _matmul_res(og, a_w_out[0].astype(BF16), h, name="hgrn_out_proj")
    h = _ffn(h, norm_ffn_g[0], ffn_w_gu[0].astype(BF16), ffn_w_down[0].astype(BF16), ones, h, name="dense_ffn")

    h = _nsa_layer(h, batch, seq, norm_mix_g[1], kv_norm_g, w_kv, k_norm_g, cmp_pos_k, cmp_w1_k, cmp_w2_k,
                   cmp_pos_v, cmp_w1_v, cmp_w2_v, b_w_in[0], b_w_out[0], b_qnorm_g[0], rel_bias)
    gates = _router(h, norm_ffn_g[1], moe_router[0])
    out = h
    for e in range(N_EXPERTS):
        out = _ffn(h, norm_ffn_g[1], moe_w_gu[0, e].astype(BF16), moe_w_down[0, e].astype(BF16),
                   gates[:, e:e + 1], out, name=f"expert_ffn_{e}")
    return out.reshape(batch, seq, d).astype(x.dtype)
```

```python
import functools
import math

import jax
import jax.numpy as jnp
import numpy as np
from jax import lax
from jax.experimental import pallas as pl
from jax.experimental.pallas import tpu as pltpu

F32 = jnp.float32
BF16 = jnp.bfloat16

NORM_EPS = 1e-6
NEG_INF = -1e30
FORCE_BONUS = 1e4
HGRN_HEAD_DIM = 128
NSA_HEADS = 16
NSA_KV_HEADS = 4
NSA_HPG = NSA_HEADS // NSA_KV_HEADS
NSA_HEAD_DIM = 64
N_BRANCH = 3
CMP_BLOCK = 32
CMP_STRIDE = 16
SLC_BLOCK = 64
SLC_TOP = 16
WINDOW = 512
Q_BLOCK = 128
REL_BUCKETS = 32
REL_MAX_DIST = 2048
N_EXPERTS = 8

LANES = 128
SUBLANES = 8
VMEM_LIMIT_BYTES = 56 * 1024 * 1024

HGRN_CHUNK = 128
HGRN_ROWS_PER_STEP = 512
SLC_KEY_TILE = 512
SLC_BLOCKS_PER_TILE = SLC_KEY_TILE // SLC_BLOCK
CMP_LOOKBACK = Q_BLOCK // CMP_STRIDE
MOE_ROW_TILE = 512


def _cparams(sem):
    return pltpu.CompilerParams(dimension_semantics=sem, vmem_limit_bytes=VMEM_LIMIT_BYTES)


def _dot(a, b):
    return jnp.dot(a, b, preferred_element_type=F32)


def _dot_nt(a, b):
    return lax.dot_general(a, b, (((1,), (1,)), ((), ())), preferred_element_type=F32)


def _split_bf16(x):
    hi = x.astype(BF16)
    lo = (x - hi.astype(F32)).astype(BF16)
    return hi, lo


def _norm_matmul_kernel(x_ref, g_ref, w_ref, *rest, epilogue, tn):
    if epilogue == "headnorm":
        gain_ref, flag_ref, bd_ref, o_ref = rest
    else:
        (o_ref,) = rest
    x = x_ref[...]
    ms = jnp.mean(x * x, axis=-1, keepdims=True)
    xn = (x * lax.rsqrt(ms + NORM_EPS) * g_ref[...]).astype(BF16)
    for c in range(o_ref.shape[1] // tn):
        cols = slice(c * tn, (c + 1) * tn)
        acc = _dot(xn, w_ref[:, cols])
        if epilogue == "headnorm":
            ss = _dot((acc * acc).astype(BF16), bd_ref[...])
            normed = acc * lax.rsqrt(ss * (1.0 / NSA_HEAD_DIM) + NORM_EPS) * gain_ref[:, cols]
            acc = jnp.where(flag_ref[:, cols] > 0.0, normed, acc)
        elif epilogue == "sigmoid":
            acc = jax.nn.sigmoid(acc)
        o_ref[:, cols] = acc.astype(o_ref.dtype)


def _norm_matmul(x, g, w, *, epilogue="none", gain=None, flag=None, out_dtype=F32, tm=512, tn=512,
                 name="norm_matmul"):
    n, d = x.shape
    m = w.shape[1]
    tm = min(tm, n)
    tn = min(tn, m)
    assert n % tm == 0 and m % tn == 0, (n, tm, m, tn)
    const = lambda i: (0, 0)
    in_specs = [pl.BlockSpec((tm, d), lambda i: (i, 0)), pl.BlockSpec((1, d), const), pl.BlockSpec((d, m), const)]
    args = [x, g.reshape(1, d).astype(F32), w]
    if epilogue == "headnorm":
        assert tn % NSA_HEAD_DIM == 0
        grp = np.arange(tn) // NSA_HEAD_DIM
        bd = jnp.asarray(grp[:, None] == grp[None, :], BF16)
        in_specs += [pl.BlockSpec((1, m), const), pl.BlockSpec((1, m), const), pl.BlockSpec((tn, tn), const)]
        args += [gain.reshape(1, m).astype(F32), flag.reshape(1, m).astype(F32), bd]
    return pl.pallas_call(
        functools.partial(_norm_matmul_kernel, epilogue=epilogue, tn=tn),
        out_shape=jax.ShapeDtypeStruct((n, m), out_dtype),
        grid=(n // tm,),
        in_specs=in_specs,
        out_specs=pl.BlockSpec((tm, m), lambda i: (i, 0)),
        compiler_params=_cparams(("parallel",)),
        name=name,
    )(*args)


def _matmul_res_kernel(a_ref, w_ref, r_ref, o_ref, *, tn):
    a = a_ref[...]
    for c in range(o_ref.shape[1] // tn):
        cols = slice(c * tn, (c + 1) * tn)
        o_ref[:, cols] = r_ref[:, cols] + _dot(a, w_ref[:, cols])


def _matmul_res(a, w, res, *, tm=512, tn=512, name="matmul_res"):
    n, k = a.shape
    m = w.shape[1]
    tm = min(tm, n)
    tn = min(tn, m)
    assert n % tm == 0 and m % tn == 0
    return pl.pallas_call(
        functools.partial(_matmul_res_kernel, tn=tn),
        out_shape=jax.ShapeDtypeStruct((n, m), F32),
        grid=(n // tm,),
        in_specs=[
            pl.BlockSpec((tm, k), lambda i: (i, 0)),
            pl.BlockSpec((k, m), lambda i: (0, 0)),
            pl.BlockSpec((tm, m), lambda i: (i, 0)),
        ],
        out_specs=pl.BlockSpec((tm, m), lambda i: (i, 0)),
        compiler_params=_cparams(("parallel",)),
        name=name,
    )(a, w, res)


def _ffn_kernel(x_ref, g_ref, wg_ref, wu_ref, wd_ref, s_ref, r_ref, o_ref, xn_ref, acc_ref):
    f = pl.program_id(1)

    @pl.when(f == 0)
    def _():
        x = x_ref[...]
        ms = jnp.mean(x * x, axis=-1, keepdims=True)
        xn_ref[...] = (x * lax.rsqrt(ms + NORM_EPS) * g_ref[...]).astype(BF16)
        acc_ref[...] = jnp.zeros_like(acc_ref)

    xn = xn_ref[...]
    gate = _dot(xn, wg_ref[...])
    up = _dot(xn, wu_ref[...])
    act = (gate * jax.nn.sigmoid(gate) * up).astype(BF16)
    acc_ref[...] += _dot(act, wd_ref[...])

    @pl.when(f == pl.num_programs(1) - 1)
    def _():
        o_ref[...] = r_ref[...] + s_ref[...] * acc_ref[...]


def _ffn(x, g, w_gu, w_down, scale, res, *, tm=512, tf=None, name="ffn"):
    n, d = x.shape
    ff = w_down.shape[0]
    if tf is None:
        tf = ff // 2 if (ff // 2) % LANES == 0 else ff
    tm = min(tm, n)
    assert n % tm == 0 and ff % tf == 0 and tf % LANES == 0
    nf = ff // tf
    return pl.pallas_call(
        _ffn_kernel,
        out_shape=jax.ShapeDtypeStruct((n, d), F32),
        grid=(n // tm, nf),
        in_specs=[
            pl.BlockSpec((tm, d), lambda i, f: (i, 0)),
            pl.BlockSpec((1, d), lambda i, f: (0, 0)),
            pl.BlockSpec((d, tf), lambda i, f: (0, f)),
            pl.BlockSpec((d, tf), lambda i, f: (0, nf + f)),
            pl.BlockSpec((tf, d), lambda i, f: (f, 0)),
            pl.BlockSpec((tm, 1), lambda i, f: (i, 0)),
            pl.BlockSpec((tm, d), lambda i, f: (i, 0)),
        ],
        out_specs=pl.BlockSpec((tm, d), lambda i, f: (i, 0)),
        scratch_shapes=[pltpu.VMEM((tm, d), BF16), pltpu.VMEM((tm, d), F32)],
        compiler_params=_cparams(("parallel", "arbitrary")),
        name=name,
    )(x, g.reshape(1, d).astype(F32), w_gu, w_gu, w_down, scale, res)


def _hgrn_decay_matrix(c):
    levels = int(math.log2(c))
    out = np.zeros(((levels + 2) * c, c), np.float32)
    for l in range(levels):
        m = c >> (l + 1)
        for r in range(c):
            mid = (r // (2 * m)) * 2 * m + m - 1
            if r % (2 * m) >= m:
                out[l * c + r, mid + 1:r + 1] = 1.0
            else:
                out[l * c + r, r + 1:mid + 1] = 1.0
    for r in range(c):
        out[levels * c + r, :r + 1] = 1.0
        out[(levels + 1) * c + r, r + 1:] = 1.0
    return out


def _hgrn_kernel(q_ref, f_ref, v_ref, g_ref, lb_ref, gn_ref, m_ref, o_ref, st_ref, *, chunk, nchunk):
    c = chunk
    dh = HGRN_HEAD_DIM
    levels = int(math.log2(c))

    @pl.when(pl.program_id(2) == 0)
    def _():
        st_ref[...] = jnp.zeros_like(st_ref)

    lb = lb_ref[...]
    gn = gn_ref[...]
    row = lax.broadcasted_iota(jnp.int32, (c, c), 0)
    col = lax.broadcasted_iota(jnp.int32, (c, c), 1)
    rowv = lax.broadcasted_iota(jnp.int32, (c, dh), 0)

    def body(ci, carry):
        sl = pl.ds(pl.multiple_of(ci * c, c), c)
        q = q_ref[sl, :]
        v = v_ref[sl, :]
        fg = lb + (1.0 - lb) * jax.nn.sigmoid(f_ref[sl, :])
        k = 1.0 - fg
        lf = jnp.log(fg)
        hi, lo = _split_bf16(lf)
        d2 = _dot(m_ref[...], jnp.concatenate([hi, lo], axis=1))
        dall = d2[:, :dh] + d2[:, dh:]

        a = jnp.where(row == col, _dot_nt(q.astype(BF16), k.astype(BF16)), 0.0)
        for l in range(levels):
            sh = levels - 1 - l
            e = jnp.exp(dall[l * c:(l + 1) * c])
            upper = ((rowv >> sh) & 1) == 1
            qe = jnp.where(upper, q * e, 0.0).astype(BF16)
            ke = jnp.where(upper, 0.0, k * e).astype(BF16)
            same = (row >> (sh + 1)) == (col >> (sh + 1))
            a = a + jnp.where(same, _dot_nt(qe, ke), 0.0)

        b = dall[levels * c:(levels + 1) * c]
        r = dall[(levels + 1) * c:]
        st = st_ref[...]
        o = _dot(a.astype(BF16), v.astype(BF16)) + _dot_nt((q * jnp.exp(b)).astype(BF16), st.astype(BF16))
        kr = (k * jnp.exp(r)).astype(BF16)
        st_ref[...] = st * jnp.exp(b[c - 1:c, :]) + _dot(v.T.astype(BF16), kr)

        ms = jnp.mean(o * o, axis=-1, keepdims=True)
        on = o * lax.rsqrt(ms + NORM_EPS) * gn
        gg = g_ref[sl, :]
        o_ref[sl, :] = (on * (gg * jax.nn.sigmoid(gg))).astype(o_ref.dtype)
        return carry

    lax.fori_loop(0, nchunk, body, 0)


def _hgrn_mixer(proj, lb, gn, batch, seq):
    n, d4 = proj.shape
    d = d4 // 4
    heads = d // HGRN_HEAD_DIM
    rows = min(HGRN_ROWS_PER_STEP, seq)
    chunk = min(HGRN_CHUNK, rows)
    assert seq % rows == 0 and rows % chunk == 0
    nt = seq // rows
    m = jnp.asarray(_hgrn_decay_matrix(chunk), BF16)

    def col_spec(s):
        return pl.BlockSpec((rows, HGRN_HEAD_DIM), lambda b, h, t, s=s: (b * nt + t, s * heads + h))

    return pl.pallas_call(
        functools.partial(_hgrn_kernel, chunk=chunk, nchunk=rows // chunk),
        out_shape=jax.ShapeDtypeStruct((n, d), BF16),
        grid=(batch, heads, nt),
        in_specs=[
            col_spec(0), col_spec(1), col_spec(2), col_spec(3),
            pl.BlockSpec((1, HGRN_HEAD_DIM), lambda b, h, t: (0, h)),
            pl.BlockSpec((1, HGRN_HEAD_DIM), lambda b, h, t: (0, 0)),
            pl.BlockSpec(m.shape, lambda b, h, t: (0, 0)),
        ],
        out_specs=pl.BlockSpec((rows, HGRN_HEAD_DIM), lambda b, h, t: (b * nt + t, h)),
        scratch_shapes=[pltpu.VMEM((HGRN_HEAD_DIM, HGRN_HEAD_DIM), F32)],
        compiler_params=_cparams(("parallel", "parallel", "arbitrary")),
        name="hgrn2_recurrence",
    )(proj, proj, proj, proj, lb.reshape(1, d).astype(F32), gn.reshape(1, HGRN_HEAD_DIM).astype(F32), m)


def _compress_kernel(x_ref, w1c_ref, pos_ref, w1_ref, w2_ref, gain_ref, o_ref, *, apply_norm):
    hid = w2_ref.shape[0]
    nc = x_ref.shape[0]
    uv = _dot(x_ref[...], w1c_ref[...])
    posb = _dot(pos_ref[...], w1_ref[...])[0:1, :]
    pre = uv[:, :hid] + pltpu.roll(uv[:, hid:], nc - 1, 0) + posb
    out = _dot(jax.nn.gelu(pre).astype(BF16), w2_ref[...])
    if apply_norm:
        ms = jnp.mean(out * out, axis=-1, keepdims=True)
        out = out * lax.rsqrt(ms + NORM_EPS) * gain_ref[...]
    o_ref[...] = out


def _compress(x, w1, pos, w2, gain, *, apply_norm, name):
    b, g, nc, half = x.shape
    hid = w1.shape[1]
    dh = w2.shape[1]
    w1c = jnp.concatenate([w1[:half], w1[half:]], axis=1).astype(BF16)
    posr = jnp.broadcast_to(pos.reshape(1, -1), (SUBLANES, pos.size)).astype(BF16)
    return pl.pallas_call(
        functools.partial(_compress_kernel, apply_norm=apply_norm),
        out_shape=jax.ShapeDtypeStruct((b, g, nc, dh), F32),
        grid=(b, g),
        in_specs=[
            pl.BlockSpec((None, None, nc, half), lambda i, j: (i, j, 0, 0)),
            pl.BlockSpec((half, 2 * hid), lambda i, j: (0, 0)),
            pl.BlockSpec((SUBLANES, 2 * half), lambda i, j: (0, 0)),
            pl.BlockSpec((2 * half, hid), lambda i, j: (0, 0)),
            pl.BlockSpec((hid, dh), lambda i, j: (0, 0)),
            pl.BlockSpec((1, dh), lambda i, j: (0, 0)),
        ],
        out_specs=pl.BlockSpec((None, None, nc, dh), lambda i, j: (i, j, 0, 0)),
        compiler_params=_cparams(("parallel", "parallel")),
        name=name,
    )(x, w1c, posr, w1.astype(BF16), w2.astype(BF16), gain.reshape(1, dh).astype(F32))


def _pack_kv_kernel(k_ref, v_ref, gain_ref, kk_ref, v1_ref):
    kp = k_ref[...]
    vp = v_ref[...]
    lane = lax.broadcasted_iota(jnp.int32, kp.shape, 1)
    low = lane < NSA_HEAD_DIM
    sq = kp * kp
    ss_lo = jnp.sum(jnp.where(low, sq, 0.0), axis=-1, keepdims=True)
    ss_hi = jnp.sum(jnp.where(low, 0.0, sq), axis=-1, keepdims=True)
    ms = jnp.where(low, ss_lo, ss_hi) * (1.0 / NSA_HEAD_DIM)
    kn = kp * lax.rsqrt(ms + NORM_EPS) * gain_ref[...]
    kr = pltpu.roll(kn, NSA_HEAD_DIM, 1)
    vr = pltpu.roll(vp, NSA_HEAD_DIM, 1)
    kk_ref[0] = jnp.where(low, kn, kr).astype(kk_ref.dtype)
    kk_ref[1] = jnp.where(low, kr, kn).astype(kk_ref.dtype)
    v1_ref[0] = jnp.where(low, vp, 1.0).astype(v1_ref.dtype)
    v1_ref[1] = jnp.where(low, vr, 1.0).astype(v1_ref.dtype)


def _pack_kv(kvproj, gain, branch, batch, seq, *, tm=512, name="pack_kv"):
    tm = min(tm, seq)
    nt = seq // tm
    pairs = NSA_KV_HEADS // 2
    kcol = branch * 2 * pairs
    vcol = kcol + pairs
    gain2 = jnp.tile(gain.reshape(1, NSA_HEAD_DIM), (1, 2)).astype(F32)
    out = jax.ShapeDtypeStruct((batch, NSA_KV_HEADS, seq, LANES), BF16)
    ospec = pl.BlockSpec((None, 2, tm, LANES), lambda b, t, p: (b, p, t, 0))
    return pl.pallas_call(
        _pack_kv_kernel,
        out_shape=(out, out),
        grid=(batch, nt, pairs),
        in_specs=[
            pl.BlockSpec((tm, LANES), lambda b, t, p: (b * nt + t, kcol + p)),
            pl.BlockSpec((tm, LANES), lambda b, t, p: (b * nt + t, vcol + p)),
            pl.BlockSpec((1, LANES), lambda b, t, p: (0, 0)),
        ],
        out_specs=(ospec, ospec),
        compiler_params=_cparams(("parallel", "parallel", "parallel")),
        name=name,
    )(kvproj, kvproj, gain2)


def _rel_bucket_np(dist):
    max_exact = REL_BUCKETS // 2
    d = np.maximum(dist, 0)
    large = max_exact + (np.log(np.maximum(d, 1).astype(np.float32) / max_exact)
                         / math.log(REL_MAX_DIST / max_exact) * (REL_BUCKETS - max_exact)).astype(np.int32)
    large = np.minimum(large, REL_BUCKETS - 1)
    return np.where(d < max_exact, d, large).astype(np.int32)


def _bias_by_distance(rel_bias, dist, valid):
    tb = rel_bias.astype(F32).T
    vals = tb[:, _rel_bucket_np(dist)]
    return jnp.where(jnp.asarray(valid)[None], vals, NEG_INF)


def _hankel(vec, nrow, ncol):
    lg = nrow + ncol - 1
    g = vec[..., :lg]
    flat = jnp.tile(g, (1,) * (g.ndim - 1) + (nrow + 1,))[..., :nrow * (lg + 1)]
    return flat.reshape(g.shape[:-1] + (nrow, lg + 1))[..., :ncol]


def _toeplitz_bias(rel_bias, ncol, max_valid):
    d = np.arange(Q_BLOCK + ncol - 1) - (Q_BLOCK - 1)
    vec = _bias_by_distance(rel_bias, d, (d >= 0) & (d < max_valid))
    tab = _hankel(vec, Q_BLOCK, ncol)[..., ::-1]
    return tab.reshape(NSA_KV_HEADS, NSA_HPG * Q_BLOCK, ncol)


def _head_masks():
    lane = lax.broadcasted_iota(jnp.int32, (Q_BLOCK, LANES), 1)
    return lane < NSA_HEAD_DIM


def _stack_heads(q_ref, low):
    parts = []
    for pair in range(NSA_HPG // 2):
        qp = q_ref[:, pair * LANES:(pair + 1) * LANES].astype(F32)
        parts.append(jnp.where(low, qp, 0.0))
        parts.append(jnp.where(low, 0.0, qp))
    return jnp.concatenate(parts, axis=0).astype(BF16)


def _unstack_heads(x, low):
    parts = []
    for pair in range(NSA_HPG // 2):
        a = x[(2 * pair) * Q_BLOCK:(2 * pair + 1) * Q_BLOCK]
        b = x[(2 * pair + 1) * Q_BLOCK:(2 * pair + 2) * Q_BLOCK]
        parts.append(jnp.where(low, a, pltpu.roll(b, NSA_HEAD_DIM, 1)))
    return jnp.concatenate(parts, axis=1)


def _normalize_pv(pv):
    den = pltpu.roll(pv, NSA_HEAD_DIM, 1)
    return pv * jnp.where(den > 0.0, 1.0 / den, 0.0)


def _nsa_cmp_kernel(q_ref, kk_ref, v1_ref, bias_ref, ov_ref, o_ref, imp_ref, *, nc):
    qb = pl.program_id(2)
    start = pl.multiple_of(qb * CMP_LOOKBACK, SUBLANES)
    kk = kk_ref[pl.ds(start, nc), :].astype(BF16)
    v1 = v1_ref[pl.ds(start, nc), :].astype(BF16)
    low = _head_masks()
    jcol = lax.broadcasted_iota(jnp.int32, (Q_BLOCK, nc), 1)
    exists = jcol >= (nc - CMP_LOOKBACK - CMP_LOOKBACK * qb)
    s4 = _dot_nt(_stack_heads(q_ref, low), kk)
    psum = jnp.zeros((Q_BLOCK, nc), F32)
    ps = []
    for hp in range(NSA_HPG):
        rows = slice(hp * Q_BLOCK, (hp + 1) * Q_BLOCK)
        s = jnp.where(exists, s4[rows] + bias_ref[rows, :], NEG_INF)
        keep = s > 0.5 * NEG_INF
        mx = jnp.max(s, axis=-1, keepdims=True)
        p = jnp.where(keep, jnp.exp(s - mx), 0.0)
        den = jnp.sum(p, axis=-1, keepdims=True)
        p = p * jnp.where(den > 0.0, 1.0 / den, 0.0)
        psum = psum + p
        ps.append(p.astype(BF16))
    pv = _dot(jnp.concatenate(ps, axis=0), v1)
    o_ref[...] = _unstack_heads(pv, low).astype(o_ref.dtype)
    hi, lo = _split_bf16(psum)
    imp_ref[...] = _dot(jnp.concatenate([hi, lo], axis=1), ov_ref[...])


def _nsa_cmp(q, kvc, vkc, bias_c, ov2, batch, seq):
    nc = seq // CMP_STRIDE
    ns = seq // SLC_BLOCK
    nqb = seq // Q_BLOCK
    rows = kvc.shape[2]
    gw = NSA_HPG * NSA_HEAD_DIM
    return pl.pallas_call(
        functools.partial(_nsa_cmp_kernel, nc=nc),
        out_shape=(jax.ShapeDtypeStruct((batch, seq, NSA_HEADS * NSA_HEAD_DIM), BF16),
                   jax.ShapeDtypeStruct((batch, NSA_KV_HEADS, seq, ns), F32)),
        grid=(batch, NSA_KV_HEADS, nqb),
        in_specs=[
            pl.BlockSpec((None, Q_BLOCK, gw), lambda b, g, i: (b, i, g)),
            pl.BlockSpec((None, None, rows, LANES), lambda b, g, i: (b, g, 0, 0)),
            pl.BlockSpec((None, None, rows, LANES), lambda b, g, i: (b, g, 0, 0)),
            pl.BlockSpec((None, NSA_HPG * Q_BLOCK, nc), lambda b, g, i: (g, 0, 0)),
            pl.BlockSpec((2 * nc, ns), lambda b, g, i: (0, 0)),
        ],
        out_specs=(pl.BlockSpec((None, Q_BLOCK, gw), lambda b, g, i: (b, i, g)),
                   pl.BlockSpec((None, None, Q_BLOCK, ns), lambda b, g, i: (b, g, i, 0))),
        compiler_params=_cparams(("parallel", "parallel", "parallel")),
        name="nsa_compressed",
    )(q, kvc, vkc, bias_c, ov2)


def _topk_kernel(imp_ref, sel_ref, *, seq, ns, ntop):
    rows = imp_ref.shape[0]
    r0 = pl.program_id(0) * rows
    rid = r0 + lax.broadcasted_iota(jnp.int32, (rows, ns), 0)
    t = rid & (seq - 1)
    qb = t >> int(math.log2(Q_BLOCK))
    js = lax.broadcasted_iota(jnp.int32, (rows, ns), 1)
    js_first = ns - 2 - 2 * qb
    js_cur = ns - 2 + ((t & (Q_BLOCK - 1)) >> int(math.log2(SLC_BLOCK)))
    causal = (js >= js_first) & (js <= js_cur)
    forced = (js == js_first) | (js == js_cur) | (js == js_cur - 1)
    n_forced = 1 + (js_cur - 1 >= js_first).astype(jnp.int32) + (js_first < js_cur - 1).astype(jnp.int32)
    n_pick = jnp.minimum(ntop, js_cur - js_first + 1) - n_forced
    ninf = -jnp.inf
    score = jnp.where(causal & jnp.logical_not(forced), imp_ref[...], ninf)
    sel = jnp.where(causal & forced, 1.0, 0.0)
    jsf = js.astype(F32)
    for it in range(ntop - 1):
        mx = jnp.max(score, axis=-1, keepdims=True)
        first = jnp.min(jnp.where(score == mx, jsf, float(ns)), axis=-1, keepdims=True)
        hit = (jsf == first) & (it < n_pick)
        sel = jnp.where(hit, 1.0, sel)
        score = jnp.where(hit, ninf, score)
    sel_ref[...] = sel.astype(sel_ref.dtype)


def _topk_select(imp, seq, *, rows=1024):
    b, g, t, ns = imp.shape
    n = b * g * t
    rows = min(rows, n)
    assert n % rows == 0
    sel = pl.pallas_call(
        functools.partial(_topk_kernel, seq=seq, ns=ns, ntop=min(SLC_TOP, ns)),
        out_shape=jax.ShapeDtypeStruct((n, ns), BF16),
        grid=(n // rows,),
        in_specs=[pl.BlockSpec((rows, ns), lambda i: (i, 0))],
        out_specs=pl.BlockSpec((rows, ns), lambda i: (i, 0)),
        compiler_params=_cparams(("parallel",)),
        name="nsa_topk",
    )(imp.reshape(n, ns))
    return sel.reshape(b, g, t, ns)


def _nsa_slc_win_kernel(q_ref, sel_ref, kks_ref, v1s_ref, kkw_ref, v1w_ref, bias_ref, bwin_ref,
                        eexp_ref, gates_ref, egate_ref, ocmp_ref, o_ref, m_ref, acc_ref, p_ref, alpha_ref,
                        *, ns, kt_near, nkt):
    qb = pl.program_id(2)
    low = _head_masks()
    kw = SLC_KEY_TILE
    kt0 = (ns - 2 - 2 * qb) // SLC_BLOCKS_PER_TILE

    m_ref[...] = jnp.full_like(m_ref, NEG_INF)
    acc_ref[...] = jnp.zeros_like(acc_ref)
    p_ref[...] = jnp.zeros_like(p_ref)
    alpha_ref[...] = jnp.ones_like(alpha_ref)
    q4 = _stack_heads(q_ref, low)
    selb = sel_ref[...]

    def row_start(kt):
        return pl.multiple_of(jnp.maximum(Q_BLOCK * qb + kw * kt, 0), Q_BLOCK)

    def accumulate(kt):
        pv = _dot(p_ref[...], v1s_ref[pl.ds(row_start(kt), kw), :])
        acc_ref[...] = alpha_ref[...] * acc_ref[...] + pv

    def body(kt):
        s4 = _dot_nt(q4, kks_ref[pl.ds(row_start(kt), kw), :])
        mb = (_dot(selb, eexp_ref[kt]) - 1.0) * (-NEG_INF)
        accumulate(kt - 1)
        bt = jnp.maximum(kt - kt_near + 1, 0)
        for hp in range(NSA_HPG):
            rows = slice(hp * Q_BLOCK, (hp + 1) * Q_BLOCK)
            s = s4[rows] + bias_ref[bt, rows, :] + mb
            m_old = m_ref[rows, :]
            m_new = jnp.maximum(m_old, jnp.max(s, axis=-1, keepdims=True))
            alpha_ref[rows, :] = jnp.exp(m_old - m_new)
            p_ref[rows, :] = jnp.exp(s - m_new[:, 0:1]).astype(BF16)
            m_ref[rows, :] = m_new

    kt_even = (kt0 // 2) * 2

    def body2(j, carry):
        body(kt_even + 2 * j)
        body(kt_even + 2 * j + 1)
        return carry

    lax.fori_loop(0, (nkt - kt_even) // 2, body2, 0)
    accumulate(nkt - 1)
    o_slc = _unstack_heads(_normalize_pv(acc_ref[...]), low)

    wk = WINDOW + Q_BLOCK
    ws = pl.multiple_of(Q_BLOCK * qb, Q_BLOCK)
    sw = _dot_nt(q4, kkw_ref[pl.ds(ws, wk), :])
    wcol = lax.broadcasted_iota(jnp.int32, (Q_BLOCK, wk), 1)
    in_seq = wcol >= (WINDOW - Q_BLOCK * qb)
    ps = []
    for hp in range(NSA_HPG):
        rows = slice(hp * Q_BLOCK, (hp + 1) * Q_BLOCK)
        s = jnp.where(in_seq, sw[rows] + bwin_ref[rows, :], NEG_INF)
        ps.append(jnp.exp(s - jnp.max(s, axis=-1, keepdims=True)).astype(BF16))
    pvw = _dot(jnp.concatenate(ps, axis=0), v1w_ref[pl.ds(ws, wk), :])
    o_win = _unstack_heads(_normalize_pv(pvw), low)

    gh, gl = _split_bf16(gates_ref[...])
    g2 = jnp.concatenate([gh, gl], axis=1)
    o = (_dot(g2, egate_ref[0]) * ocmp_ref[...].astype(F32)
         + _dot(g2, egate_ref[1]) * o_slc
         + _dot(g2, egate_ref[2]) * o_win)
    o_ref[...] = o.astype(o_ref.dtype)


def _nsa_slc_win(q, sel, kvs, vks, kvw, vkw, bias_s, bwin, eexp, gates, egate, ocmp, batch, seq):
    ns = seq // SLC_BLOCK
    nqb = seq // Q_BLOCK
    nkt = (ns * SLC_BLOCK) // SLC_KEY_TILE
    kt_near = nkt - (bias_s.shape[1] - 1)
    gw = NSA_HPG * NSA_HEAD_DIM
    srows = kvs.shape[2]
    wrows = kvw.shape[2]
    slab = lambda r: pl.BlockSpec((None, None, r, LANES), lambda b, g, i: (b, g, 0, 0))
    return pl.pallas_call(
        functools.partial(_nsa_slc_win_kernel, ns=ns, kt_near=kt_near, nkt=nkt),
        out_shape=jax.ShapeDtypeStruct((batch, seq, NSA_HEADS * NSA_HEAD_DIM), BF16),
        grid=(batch, NSA_KV_HEADS, nqb),
        in_specs=[
            pl.BlockSpec((None, Q_BLOCK, gw), lambda b, g, i: (b, i, g)),
            pl.BlockSpec((None, None, Q_BLOCK, ns), lambda b, g, i: (b, g, i, 0)),
            slab(srows), slab(srows), slab(wrows), slab(wrows),
            pl.BlockSpec((None,) + bias_s.shape[1:], lambda b, g, i: (g, 0, 0, 0)),
            pl.BlockSpec((None,) + bwin.shape[1:], lambda b, g, i: (g, 0, 0)),
            pl.BlockSpec(eexp.shape, lambda b, g, i: (0, 0, 0)),
            pl.BlockSpec((None, Q_BLOCK, LANES), lambda b, g, i: (b, i, 0)),
            pl.BlockSpec((N_BRANCH, 2 * LANES, gw), lambda b, g, i: (0, 0, g)),
            pl.BlockSpec((None, Q_BLOCK, gw), lambda b, g, i: (b, i, g)),
        ],
        out_specs=pl.BlockSpec((None, Q_BLOCK, gw), lambda b, g, i: (b, i, g)),
        scratch_shapes=[pltpu.VMEM((NSA_HPG * Q_BLOCK, LANES), F32),
                        pltpu.VMEM((NSA_HPG * Q_BLOCK, LANES), F32),
                        pltpu.VMEM((NSA_HPG * Q_BLOCK, SLC_KEY_TILE), BF16),
                        pltpu.VMEM((NSA_HPG * Q_BLOCK, LANES), F32)],
        compiler_params=_cparams(("parallel", "parallel", "arbitrary")),
        name="nsa_selected_window",
    )(q, sel, kvs, vks, kvw, vkw, bias_s, bwin, eexp, gates, egate, ocmp)


def _nsa_tables(rel_bias, seq):
    nc = seq // CMP_STRIDE
    ns = seq // SLC_BLOCK
    off = CMP_STRIDE * (CMP_LOOKBACK - 1) + CMP_BLOCK - 1
    x = np.arange(CMP_LOOKBACK + nc - 1)[None, :]
    d = CMP_STRIDE * x + np.arange(CMP_STRIDE)[:, None] - off
    vec = _bias_by_distance(rel_bias, d, d >= 0)
    bias_c = _hankel(vec, CMP_LOOKBACK, nc)[..., ::-1]
    bias_c = bias_c.transpose(0, 2, 1, 3).reshape(NSA_KV_HEADS, NSA_HPG * Q_BLOCK, nc)
    cs = np.arange(nc)[:, None] * CMP_STRIDE
    ss = np.arange(ns)[None, :] * SLC_BLOCK
    ov = np.clip(np.minimum(cs + CMP_BLOCK, ss + SLC_BLOCK) - np.maximum(cs, ss), 0, None) / CMP_BLOCK
    ov2 = jnp.asarray(np.concatenate([ov, ov], axis=0), BF16)
    nkeys = ns * SLC_BLOCK
    nkt = nkeys // SLC_KEY_TILE
    pad_rows = (ns - 2) * SLC_BLOCK
    kt_near = max(0, pad_rows - REL_MAX_DIST) // SLC_KEY_TILE
    near = _toeplitz_bias(rel_bias, nkeys - kt_near * SLC_KEY_TILE, 2 * seq)
    near = near.reshape(NSA_KV_HEADS, NSA_HPG * Q_BLOCK, nkt - kt_near, SLC_KEY_TILE).transpose(0, 2, 1, 3)
    far = jnp.broadcast_to(rel_bias.astype(F32)[REL_BUCKETS - 1].reshape(NSA_KV_HEADS, 1, NSA_HPG, 1, 1),
                           (NSA_KV_HEADS, 1, NSA_HPG, Q_BLOCK, SLC_KEY_TILE))
    bias_s = jnp.concatenate([far.reshape(NSA_KV_HEADS, 1, NSA_HPG * Q_BLOCK, SLC_KEY_TILE), near], axis=1)
    bwin = _toeplitz_bias(rel_bias, WINDOW + Q_BLOCK, WINDOW)
    blk = np.arange(ns)[None, :, None]
    key = np.arange(SLC_KEY_TILE)[None, None, :]
    ktile = np.arange(nkt)[:, None, None]
    eexp = jnp.asarray(blk == ktile * SLC_BLOCKS_PER_TILE + key // SLC_BLOCK, BF16)
    col = np.arange(LANES)[:, None]
    head = (np.arange(NSA_HEADS * NSA_HEAD_DIM) // NSA_HEAD_DIM)[None, :]
    eg = np.stack([col == head * N_BRANCH + br for br in range(N_BRANCH)])
    egate = jnp.asarray(np.concatenate([eg, eg], axis=1), BF16)
    return bias_c, ov2, bias_s, bwin, eexp, egate


def _router_kernel(x_ref, g_ref, w_ref, o_ref):
    x = x_ref[...]
    ms = jnp.mean(x * x, axis=-1, keepdims=True)
    xn = x * lax.rsqrt(ms + NORM_EPS) * g_ref[...]
    logits = jnp.dot(xn, w_ref[...], preferred_element_type=F32, precision=lax.Precision.HIGHEST)
    lane = lax.broadcasted_iota(jnp.int32, logits.shape, 1).astype(F32)
    ninf = -jnp.inf
    s = jnp.where(lane < N_EXPERTS, logits, ninf)
    m1 = jnp.max(s, axis=-1, keepdims=True)
    i1 = jnp.min(jnp.where(s == m1, lane, float(LANES)), axis=-1, keepdims=True)
    s2 = jnp.where(lane == i1, ninf, s)
    m2 = jnp.max(s2, axis=-1, keepdims=True)
    i2 = jnp.min(jnp.where(s2 == m2, lane, float(LANES)), axis=-1, keepdims=True)
    e2 = jnp.exp(m2 - m1)
    w1 = 1.0 / (1.0 + e2)
    w2 = e2 * w1
    o_ref[...] = (jnp.where(lane == 0.0, i1, 0.0) + jnp.where(lane == 1.0, i2, 0.0)
                  + jnp.where(lane == 2.0, w1, 0.0) + jnp.where(lane == 3.0, w2, 0.0))


def _router(x, g, w_router, *, tm=512):
    n, d = x.shape
    tm = min(tm, n)
    wpad = jnp.zeros((d, LANES), F32).at[:, :N_EXPERTS].set(w_router.astype(F32))
    return pl.pallas_call(
        _router_kernel,
        out_shape=jax.ShapeDtypeStruct((n, LANES), F32),
        grid=(n // tm,),
        in_specs=[
            pl.BlockSpec((tm, d), lambda i: (i, 0)),
            pl.BlockSpec((1, d), lambda i: (0, 0)),
            pl.BlockSpec((d, LANES), lambda i: (0, 0)),
        ],
        out_specs=pl.BlockSpec((tm, LANES), lambda i: (i, 0)),
        compiler_params=_cparams(("parallel",)),
        name="moe_router",
    )(x, g.reshape(1, d).astype(F32), wpad)


def _moe_dispatch(route, tm):
    n = route.shape[0]
    pairs = 2 * n
    e = route[:, 0:2].astype(jnp.int32).reshape(pairs)
    w = route[:, 2:4].reshape(pairs)
    onehot = (e[:, None] == jnp.arange(N_EXPERTS, dtype=jnp.int32)[None, :]).astype(jnp.int32)
    csum = jnp.cumsum(onehot, axis=0)
    rank = jnp.sum(csum * onehot, axis=1) - 1
    counts = csum[-1]
    cpad = ((counts + tm - 1) // tm) * tm
    gend = jnp.cumsum(cpad)
    dst = jnp.sum(onehot * (gend - cpad)[None, :], axis=1) + rank
    rows = pairs + N_EXPERTS * tm
    ntiles = rows // tm
    nvalid = gend[-1] // tm
    tile = jnp.arange(ntiles, dtype=jnp.int32)
    te = jnp.sum((tile[:, None] * tm >= gend[None, :]).astype(jnp.int32), axis=1)
    te = jnp.where(tile < nvalid, te, te[nvalid - 1])
    src = jnp.zeros((rows,), jnp.int32).at[dst].set(jnp.arange(pairs, dtype=jnp.int32) // 2)
    wrow = jnp.zeros((rows,), F32).at[dst].set(w)
    return src, wrow, dst, te.astype(jnp.int32), nvalid.reshape(1).astype(jnp.int32)


def _row_copy(src_hbm, row, dst_ref, j, sem):
    return pltpu.make_async_copy(src_hbm.at[pl.ds(row, 1), :], dst_ref.at[pl.ds(j, 1), :], sem)


def _gather_rows_kernel(idx_ref, src_hbm, o_ref, sem):
    rows = o_ref.shape[0]

    def issue(j, c):
        _row_copy(src_hbm, idx_ref[j], o_ref, j, sem).start()
        return c

    lax.fori_loop(0, rows, issue, 0, unroll=8)
    pltpu.make_async_copy(src_hbm.at[pl.ds(0, rows), :], o_ref, sem).wait()


def _gather_rows(x, idx, *, rows=256):
    n, d = x.shape
    total = idx.shape[0]
    assert total % rows == 0
    return pl.pallas_call(
        _gather_rows_kernel,
        out_shape=jax.ShapeDtypeStruct((total, d), x.dtype),
        grid=(total // rows,),
        in_specs=[
            pl.BlockSpec((rows,), lambda i: (i,), memory_space=pltpu.SMEM),
            pl.BlockSpec(memory_space=pl.ANY),
        ],
        out_specs=pl.BlockSpec((rows, d), lambda i: (i, 0)),
        scratch_shapes=[pltpu.SemaphoreType.DMA],
        compiler_params=_cparams(("arbitrary",)),
        name="moe_gather",
    )(idx, x)


def _combine_kernel(i0_ref, i1_ref, y_hbm, h_ref, o_ref, buf_ref, sem):
    rows = h_ref.shape[0]

    def issue(j, c):
        _row_copy(y_hbm, i0_ref[j], buf_ref.at[0], j, sem).start()
        _row_copy(y_hbm, i1_ref[j], buf_ref.at[1], j, sem).start()
        return c

    lax.fori_loop(0, rows, issue, 0, unroll=8)
    for k in range(2):
        pltpu.make_async_copy(y_hbm.at[pl.ds(0, rows), :], buf_ref.at[k], sem).wait()
    o_ref[...] = h_ref[...] + buf_ref[0] + buf_ref[1]


def _combine(h, y, dst, *, rows=256):
    n, d = h.shape
    assert n % rows == 0
    dst2 = dst.reshape(n, 2)
    return pl.pallas_call(
        _combine_kernel,
        out_shape=jax.ShapeDtypeStruct((n, d), F32),
        grid=(n // rows,),
        in_specs=[
            pl.BlockSpec((rows,), lambda i: (i,), memory_space=pltpu.SMEM),
            pl.BlockSpec((rows,), lambda i: (i,), memory_space=pltpu.SMEM),
            pl.BlockSpec(memory_space=pl.ANY),
            pl.BlockSpec((rows, d), lambda i: (i, 0)),
        ],
        out_specs=pl.BlockSpec((rows, d), lambda i: (i, 0)),
        scratch_shapes=[pltpu.VMEM((2, rows, d), F32), pltpu.SemaphoreType.DMA],
        compiler_params=_cparams(("arbitrary",)),
        name="moe_combine",
    )(dst2[:, 0], dst2[:, 1], y, h)


def _ffn_grouped_kernel(te_ref, nv_ref, x_ref, g_ref, wg_ref, wu_ref, wd_ref, s_ref, o_ref, xn_ref, acc_ref):
    i = pl.program_id(0)
    f = pl.program_id(1)
    used = i < nv_ref[0]

    @pl.when(jnp.logical_and(used, f == 0))
    def _():
        x = x_ref[...]
        ms = jnp.mean(x * x, axis=-1, keepdims=True)
        xn_ref[...] = (x * lax.rsqrt(ms + NORM_EPS) * g_ref[...]).astype(BF16)
        acc_ref[...] = jnp.zeros_like(acc_ref)

    @pl.when(used)
    def _():
        xn = xn_ref[...]
        gate = _dot(xn, wg_ref[...])
        up = _dot(xn, wu_ref[...])
        act = (gate * jax.nn.sigmoid(gate) * up).astype(BF16)
        acc_ref[...] += _dot(act, wd_ref[...])

    last = f == pl.num_programs(1) - 1

    @pl.when(jnp.logical_and(used, last))
    def _():
        o_ref[...] = s_ref[...] * acc_ref[...]

    @pl.when(jnp.logical_and(jnp.logical_not(used), last))
    def _():
        o_ref[...] = jnp.zeros_like(o_ref)


def _ffn_grouped(x, g, w_gu, w_down, wrow, te, nvalid, *, tm, tf=None):
    rows, d = x.shape
    ff = w_down.shape[1]
    if tf is None:
        tf = ff // 2 if (ff // 2) % LANES == 0 else ff
    assert rows % tm == 0 and ff % tf == 0 and tf % LANES == 0
    nf = ff // tf

    def fcol(i, f, te_ref, nv_ref):
        return jnp.where(i < nv_ref[0], f, nf - 1)

    grid_spec = pltpu.PrefetchScalarGridSpec(
        num_scalar_prefetch=2,
        grid=(rows // tm, nf),
        in_specs=[
            pl.BlockSpec((tm, d), lambda i, f, te_ref, nv_ref: (i, 0)),
            pl.BlockSpec((1, d), lambda i, f, te_ref, nv_ref: (0, 0)),
            pl.BlockSpec((None, d, tf), lambda i, f, te_ref, nv_ref: (te_ref[i], 0, fcol(i, f, te_ref, nv_ref))),
            pl.BlockSpec((None, d, tf),
                         lambda i, f, te_ref, nv_ref: (te_ref[i], 0, nf + fcol(i, f, te_ref, nv_ref))),
            pl.BlockSpec((None, tf, d), lambda i, f, te_ref, nv_ref: (te_ref[i], fcol(i, f, te_ref, nv_ref), 0)),
            pl.BlockSpec((tm, 1), lambda i, f, te_ref, nv_ref: (i, 0)),
        ],
        out_specs=pl.BlockSpec((tm, d), lambda i, f, te_ref, nv_ref: (i, 0)),
        scratch_shapes=[pltpu.VMEM((tm, d), BF16), pltpu.VMEM((tm, d), F32)],
    )
    return pl.pallas_call(
        _ffn_grouped_kernel,
        out_shape=jax.ShapeDtypeStruct((rows, d), F32),
        grid_spec=grid_spec,
        compiler_params=_cparams(("arbitrary", "arbitrary")),
        name="expert_ffn_grouped",
    )(te, nvalid, x, g.reshape(1, d).astype(F32), w_gu, w_gu, w_down, wrow.reshape(rows, 1))


def _nsa_layer(h, batch, seq, norm_g, kv_norm_g, w_kv, k_norm_g, cmp_pos_k, cmp_w1_k, cmp_w2_k,
               cmp_pos_v, cmp_w1_v, cmp_w2_v, w_in, w_out, q_norm_g, rel_bias):
    n, d = h.shape
    g_heads, dh = NSA_KV_HEADS, NSA_HEAD_DIM
    nc = seq // CMP_STRIDE
    ns = seq // SLC_BLOCK
    hq = NSA_HEADS * dh

    kvproj = _norm_matmul(h, kv_norm_g, w_kv.astype(BF16), name="nsa_kv_proj")

    kv6 = kvproj.reshape(batch, seq, N_BRANCH, 2, g_heads, dh)

    def chunks(t):
        return t.transpose(0, 2, 1, 3).reshape(batch, g_heads, nc, CMP_STRIDE * dh).astype(BF16)

    k_cmp = _compress(chunks(kv6[:, :, 0, 0]), cmp_w1_k, cmp_pos_k, cmp_w2_k, k_norm_g[0],
                      apply_norm=True, name="nsa_compress_k")
    v_cmp = _compress(chunks(kv6[:, :, 0, 1]), cmp_w1_v, cmp_pos_v, cmp_w2_v, k_norm_g[0],
                      apply_norm=False, name="nsa_compress_v")
    cpad = nc - CMP_LOOKBACK
    padc = lambda t: jnp.pad(t, ((0, 0), (0, 0), (cpad, 0), (0, 0)))
    kvc = padc(jnp.concatenate([k_cmp, k_cmp], axis=-1))
    vkc = padc(jnp.concatenate([v_cmp, jnp.ones_like(v_cmp)], axis=-1))

    kvs, vks = _pack_kv(kvproj, k_norm_g[1], 1, batch, seq, name="nsa_pack_selected")
    kvw, vkw = _pack_kv(kvproj, k_norm_g[2], 2, batch, seq, name="nsa_pack_window")
    spad = (ns - 2) * SLC_BLOCK
    pads = lambda t, r: jnp.pad(t, ((0, 0), (0, 0), (r, 0), (0, 0)))
    kvs, vks = pads(kvs, spad), pads(vks, spad)
    kvw, vkw = pads(kvw, WINDOW), pads(vkw, WINDOW)

    qgain = jnp.tile(q_norm_g.astype(F32), NSA_HEADS) * (dh ** -0.5)
    q = _norm_matmul(h, norm_g, w_in[:, :hq].astype(BF16), epilogue="headnorm", gain=qgain,
                     flag=jnp.ones((hq,), F32), out_dtype=BF16, name="nsa_q_proj")
    ngate = N_BRANCH * NSA_HEADS
    wg = jnp.zeros((d, LANES), BF16).at[:, :ngate].set(w_in[:, hq:].astype(BF16))
    gates = _norm_matmul(h, norm_g, wg, epilogue="sigmoid", name="nsa_gate_proj")

    bias_c, ov2, bias_s, bwin, eexp, egate = _nsa_tables(rel_bias, seq)
    q3 = q.reshape(batch, seq, hq)
    o_cmp, imp = _nsa_cmp(q3, kvc, vkc, bias_c, ov2, batch, seq)
    sel = _topk_select(imp, seq)
    o = _nsa_slc_win(q3, sel, kvs, vks, kvw, vkw, bias_s, bwin, eexp,
                     gates.reshape(batch, seq, LANES), egate, o_cmp, batch, seq)
    return _matmul_res(o.reshape(n, hq), w_out.astype(BF16), h, name="nsa_out_proj")


def kernel(x, norm_mix_g, norm_ffn_g, a_w_in, a_w_out, a_onorm_g, lb_param, kv_norm_g, w_kv, k_norm_g,
           cmp_pos_k, cmp_w1_k, cmp_w2_k, cmp_pos_v, cmp_w1_v, cmp_w2_v, b_w_in, b_w_out, b_qnorm_g,
           rel_bias, ffn_w_gu, ffn_w_down, moe_router, moe_w_gu, moe_w_down):
    batch, seq, d = x.shape
    n = batch * seq
    h = x.reshape(n, d).astype(F32)
    ones = jnp.ones((n, 1), F32)

    lower = jnp.cumsum(jax.nn.softmax(lb_param.astype(F32), axis=0), axis=0)[0]
    proj = _norm_matmul(h, norm_mix_g[0], a_w_in[0].astype(BF16), name="hgrn_in_proj")
    og = _hgrn_mixer(proj, lower, a_onorm_g[0], batch, seq)
    h = _matmul_res(og, a_w_out[0].astype(BF16), h, name="hgrn_out_proj")
    h = _ffn(h, norm_ffn_g[0], ffn_w_gu[0].astype(BF16), ffn_w_down[0].astype(BF16), ones, h, name="dense_ffn")

    h = _nsa_layer(h, batch, seq, norm_mix_g[1], kv_norm_g, w_kv, k_norm_g, cmp_pos_k, cmp_w1_k, cmp_w2_k,
                   cmp_pos_v, cmp_w1_v, cmp_w2_v, b_w_in[0], b_w_out[0], b_qnorm_g[0], rel_bias)
    route = _router(h, norm_ffn_g[1], moe_router[0])
    src, wrow, dst, tile_expert, tiles_used = _moe_dispatch(route, MOE_ROW_TILE)
    ys = _ffn_grouped(_gather_rows(h, src), norm_ffn_g[1], moe_w_gu[0].astype(BF16), moe_w_down[0].astype(BF16),
                      wrow, tile_expert, tiles_used, tm=MOE_ROW_TILE)
    out = _combine(h, ys, dst)
    return out.reshape(batch, seq, d).astype(x.dtype)
```

```python
import functools
import math

import jax
import jax.numpy as jnp
import numpy as np
from jax import lax
from jax.experimental import pallas as pl
from jax.experimental.pallas import tpu as pltpu

F32 = jnp.float32
BF16 = jnp.bfloat16

NORM_EPS = 1e-6
NEG_INF = -1e30
FORCE_BONUS = 1e4
HGRN_HEAD_DIM = 128
NSA_HEADS = 16
NSA_KV_HEADS = 4
NSA_HPG = NSA_HEADS // NSA_KV_HEADS
NSA_HEAD_DIM = 64
N_BRANCH = 3
CMP_BLOCK = 32
CMP_STRIDE = 16
SLC_BLOCK = 64
SLC_TOP = 16
WINDOW = 512
Q_BLOCK = 128
REL_BUCKETS = 32
REL_MAX_DIST = 2048
N_EXPERTS = 8

LANES = 128
SUBLANES = 8
VMEM_LIMIT_BYTES = 56 * 1024 * 1024

HGRN_CHUNK = 128
HGRN_ROWS_PER_STEP = 512
HGRN_HEADS_PER_STEP = 4
SLC_KEY_TILE = 512
SLC_BLOCKS_PER_TILE = SLC_KEY_TILE // SLC_BLOCK
CMP_LOOKBACK = Q_BLOCK // CMP_STRIDE
MOE_ROW_TILE = 512


def _cparams(sem):
    return pltpu.CompilerParams(dimension_semantics=sem, vmem_limit_bytes=VMEM_LIMIT_BYTES)


def _dot(a, b):
    return jnp.dot(a, b, preferred_element_type=F32)


def _dot_nt(a, b):
    return lax.dot_general(a, b, (((1,), (1,)), ((), ())), preferred_element_type=F32)


def _split_bf16(x):
    hi = x.astype(BF16)
    lo = (x - hi.astype(F32)).astype(BF16)
    return hi, lo


def _norm_matmul_kernel(x_ref, g_ref, w_ref, *rest, epilogue, tn):
    if epilogue == "headnorm":
        gain_ref, flag_ref, bd_ref, o_ref = rest
    else:
        (o_ref,) = rest
    x = x_ref[...]
    ms = jnp.mean(x * x, axis=-1, keepdims=True)
    xn = (x * lax.rsqrt(ms + NORM_EPS) * g_ref[...]).astype(BF16)
    for c in range(o_ref.shape[1] // tn):
        cols = slice(c * tn, (c + 1) * tn)
        acc = _dot(xn, w_ref[:, cols])
        if epilogue == "headnorm":
            ss = _dot((acc * acc).astype(BF16), bd_ref[...])
            normed = acc * lax.rsqrt(ss * (1.0 / NSA_HEAD_DIM) + NORM_EPS) * gain_ref[:, cols]
            acc = jnp.where(flag_ref[:, cols] > 0.0, normed, acc)
        elif epilogue == "sigmoid":
            acc = jax.nn.sigmoid(acc)
        o_ref[:, cols] = acc.astype(o_ref.dtype)


def _norm_matmul(x, g, w, *, epilogue="none", gain=None, flag=None, out_dtype=F32, tm=512, tn=512,
                 name="norm_matmul"):
    n, d = x.shape
    m = w.shape[1]
    tm = min(tm, n)
    tn = min(tn, m)
    assert n % tm == 0 and m % tn == 0, (n, tm, m, tn)
    const = lambda i: (0, 0)
    in_specs = [pl.BlockSpec((tm, d), lambda i: (i, 0)), pl.BlockSpec((1, d), const), pl.BlockSpec((d, m), const)]
    args = [x, g.reshape(1, d).astype(F32), w]
    if epilogue == "headnorm":
        assert tn % NSA_HEAD_DIM == 0
        grp = np.arange(tn) // NSA_HEAD_DIM
        bd = jnp.asarray(grp[:, None] == grp[None, :], BF16)
        in_specs += [pl.BlockSpec((1, m), const), pl.BlockSpec((1, m), const), pl.BlockSpec((tn, tn), const)]
        args += [gain.reshape(1, m).astype(F32), flag.reshape(1, m).astype(F32), bd]
    return pl.pallas_call(
        functools.partial(_norm_matmul_kernel, epilogue=epilogue, tn=tn),
        out_shape=jax.ShapeDtypeStruct((n, m), out_dtype),
        grid=(n // tm,),
        in_specs=in_specs,
        out_specs=pl.BlockSpec((tm, m), lambda i: (i, 0)),
        compiler_params=_cparams(("parallel",)),
        name=name,
    )(*args)


def _matmul_res_kernel(a_ref, w_ref, r_ref, o_ref, *, tn):
    a = a_ref[...]
    for c in range(o_ref.shape[1] // tn):
        cols = slice(c * tn, (c + 1) * tn)
        o_ref[:, cols] = r_ref[:, cols] + _dot(a, w_ref[:, cols])


def _matmul_res(a, w, res, *, tm=512, tn=512, name="matmul_res"):
    n, k = a.shape
    m = w.shape[1]
    tm = min(tm, n)
    tn = min(tn, m)
    assert n % tm == 0 and m % tn == 0
    return pl.pallas_call(
        functools.partial(_matmul_res_kernel, tn=tn),
        out_shape=jax.ShapeDtypeStruct((n, m), F32),
        grid=(n // tm,),
        in_specs=[
            pl.BlockSpec((tm, k), lambda i: (i, 0)),
            pl.BlockSpec((k, m), lambda i: (0, 0)),
            pl.BlockSpec((tm, m), lambda i: (i, 0)),
        ],
        out_specs=pl.BlockSpec((tm, m), lambda i: (i, 0)),
        compiler_params=_cparams(("parallel",)),
        name=name,
    )(a, w, res)


def _ffn_kernel(x_ref, g_ref, wg_ref, wu_ref, wd_ref, o_ref, xn_ref, acc_ref):
    f = pl.program_id(1)

    @pl.when(f == 0)
    def _():
        x = x_ref[...]
        ms = jnp.mean(x * x, axis=-1, keepdims=True)
        xn_ref[...] = (x * lax.rsqrt(ms + NORM_EPS) * g_ref[...]).astype(BF16)
        acc_ref[...] = jnp.zeros_like(acc_ref)

    xn = xn_ref[...]
    gate = _dot(xn, wg_ref[...])
    up = _dot(xn, wu_ref[...])
    act = (gate * jax.nn.sigmoid(gate) * up).astype(BF16)
    acc_ref[...] += _dot(act, wd_ref[...])

    @pl.when(f == pl.num_programs(1) - 1)
    def _():
        o_ref[...] = x_ref[...] + acc_ref[...]


def _ffn(x, g, w_gu, w_down, *, tm=512, tf=None, name="ffn"):
    n, d = x.shape
    ff = w_down.shape[0]
    if tf is None:
        tf = ff // 2 if (ff // 2) % LANES == 0 else ff
    tm = min(tm, n)
    assert n % tm == 0 and ff % tf == 0 and tf % LANES == 0
    nf = ff // tf
    return pl.pallas_call(
        _ffn_kernel,
        out_shape=jax.ShapeDtypeStruct((n, d), F32),
        grid=(n // tm, nf),
        in_specs=[
            pl.BlockSpec((tm, d), lambda i, f: (i, 0)),
            pl.BlockSpec((1, d), lambda i, f: (0, 0)),
            pl.BlockSpec((d, tf), lambda i, f: (0, f)),
            pl.BlockSpec((d, tf), lambda i, f: (0, nf + f)),
            pl.BlockSpec((tf, d), lambda i, f: (f, 0)),
        ],
        out_specs=pl.BlockSpec((tm, d), lambda i, f: (i, 0)),
        scratch_shapes=[pltpu.VMEM((tm, d), BF16), pltpu.VMEM((tm, d), F32)],
        compiler_params=_cparams(("parallel", "arbitrary")),
        name=name,
    )(x, g.reshape(1, d).astype(F32), w_gu, w_gu, w_down)


def _hgrn_decay_matrix(c):
    levels = int(math.log2(c))
    out = np.zeros(((levels + 2) * c, c), np.float32)
    for l in range(levels):
        m = c >> (l + 1)
        for r in range(c):
            mid = (r // (2 * m)) * 2 * m + m - 1
            if r % (2 * m) >= m:
                out[l * c + r, mid + 1:r + 1] = 1.0
            else:
                out[l * c + r, r + 1:mid + 1] = 1.0
    for r in range(c):
        out[levels * c + r, :r + 1] = 1.0
        out[(levels + 1) * c + r, r + 1:] = 1.0
    return out


def _hgrn_kernel(q_ref, f_ref, v_ref, g_ref, lb_ref, gn_ref, m_ref, o_ref, st_ref, *, chunk, nchunk):
    c = chunk
    dh = HGRN_HEAD_DIM
    levels = int(math.log2(c))

    @pl.when(pl.program_id(2) == 0)
    def _():
        st_ref[...] = jnp.zeros_like(st_ref)

    gn = gn_ref[...]
    row = lax.broadcasted_iota(jnp.int32, (c, c), 0)
    col = lax.broadcasted_iota(jnp.int32, (c, c), 1)
    rowv = lax.broadcasted_iota(jnp.int32, (c, dh), 0)
    nheads = st_ref.shape[0]

    def body(ci, carry):
        sl = pl.ds(pl.multiple_of(ci * c, c), c)
        lb = lb_ref[...]
        fg = lb + (1.0 - lb) * jax.nn.sigmoid(f_ref[sl, :])
        kall = 1.0 - fg
        hi, lo = _split_bf16(jnp.log(fg))
        dall = _dot(m_ref[...], jnp.concatenate([hi, lo], axis=0))
        hs = range(nheads)
        cols = [slice(hh * dh, (hh + 1) * dh) for hh in hs]
        q = [q_ref[sl, cols[hh]] for hh in hs]
        k = [kall[:, cols[hh]] for hh in hs]
        v = [v_ref[sl, cols[hh]] for hh in hs]

        a = [jnp.where(row == col, _dot_nt(q[hh].astype(BF16), k[hh].astype(BF16)), 0.0) for hh in hs]
        for l in range(levels):
            sh = levels - 1 - l
            upper = ((rowv >> sh) & 1) == 1
            same = (row >> (sh + 1)) == (col >> (sh + 1))
            for hh in hs:
                e = jnp.exp(dall[l * c:(l + 1) * c, cols[hh]])
                qe = jnp.where(upper, q[hh] * e, 0.0).astype(BF16)
                ke = jnp.where(upper, 0.0, k[hh] * e).astype(BF16)
                a[hh] = a[hh] + jnp.where(same, _dot_nt(qe, ke), 0.0)
        b = [dall[levels * c:(levels + 1) * c, cols[hh]] for hh in hs]
        st = [st_ref[hh] for hh in hs]
        o = [_dot(a[hh].astype(BF16), v[hh].astype(BF16))
             + _dot_nt((q[hh] * jnp.exp(b[hh])).astype(BF16), st[hh].astype(BF16)) for hh in hs]
        for hh in hs:
            kr = (k[hh] * jnp.exp(dall[(levels + 1) * c:, cols[hh]])).astype(BF16)
            st_ref[hh] = st[hh] * jnp.exp(b[hh][c - 1:c, :]) + _dot(v[hh].T.astype(BF16), kr)
        for hh in hs:
            ms = jnp.mean(o[hh] * o[hh], axis=-1, keepdims=True)
            on = o[hh] * lax.rsqrt(ms + NORM_EPS) * gn
            gg = g_ref[sl, cols[hh]]
            o_ref[sl, cols[hh]] = (on * (gg * jax.nn.sigmoid(gg))).astype(o_ref.dtype)
        return carry

    lax.fori_loop(0, nchunk, body, 0)


def _hgrn_mixer(proj, lb, gn, batch, seq):
    n, d4 = proj.shape
    d = d4 // 4
    heads = d // HGRN_HEAD_DIM
    rows = min(HGRN_ROWS_PER_STEP, seq)
    chunk = min(HGRN_CHUNK, rows)
    assert seq % rows == 0 and rows % chunk == 0
    nt = seq // rows
    m1 = _hgrn_decay_matrix(chunk)
    m = jnp.asarray(np.concatenate([m1, m1], axis=1), BF16)

    hps = HGRN_HEADS_PER_STEP
    assert heads % hps == 0
    groups = heads // hps
    width = hps * HGRN_HEAD_DIM

    def col_spec(s):
        return pl.BlockSpec((rows, width), lambda b, h, t, s=s: (b * nt + t, s * groups + h))

    return pl.pallas_call(
        functools.partial(_hgrn_kernel, chunk=chunk, nchunk=rows // chunk),
        out_shape=jax.ShapeDtypeStruct((n, d), BF16),
        grid=(batch, groups, nt),
        in_specs=[
            col_spec(0), col_spec(1), col_spec(2), col_spec(3),
            pl.BlockSpec((1, width), lambda b, h, t: (0, h)),
            pl.BlockSpec((1, HGRN_HEAD_DIM), lambda b, h, t: (0, 0)),
            pl.BlockSpec(m.shape, lambda b, h, t: (0, 0)),
        ],
        out_specs=pl.BlockSpec((rows, width), lambda b, h, t: (b * nt + t, h)),
        scratch_shapes=[pltpu.VMEM((hps, HGRN_HEAD_DIM, HGRN_HEAD_DIM), F32)],
        compiler_params=_cparams(("parallel", "parallel", "arbitrary")),
        name="hgrn2_recurrence",
    )(proj, proj, proj, proj, lb.reshape(1, d).astype(F32), gn.reshape(1, HGRN_HEAD_DIM).astype(F32), m)


def _compress_kernel(x_ref, w1c_ref, pos_ref, w1_ref, w2_ref, gain_ref, o_ref, *, apply_norm):
    hid = w2_ref.shape[0]
    nc = x_ref.shape[0]
    uv = _dot(x_ref[...], w1c_ref[...])
    posb = _dot(pos_ref[...], w1_ref[...])[0:1, :]
    pre = uv[:, :hid] + pltpu.roll(uv[:, hid:], nc - 1, 0) + posb
    out = _dot(jax.nn.gelu(pre).astype(BF16), w2_ref[...])
    if apply_norm:
        ms = jnp.mean(out * out, axis=-1, keepdims=True)
        out = out * lax.rsqrt(ms + NORM_EPS) * gain_ref[...]
    o_ref[...] = out


def _compress(x, w1, pos, w2, gain, *, apply_norm, name):
    b, g, nc, half = x.shape
    hid = w1.shape[1]
    dh = w2.shape[1]
    w1c = jnp.concatenate([w1[:half], w1[half:]], axis=1).astype(BF16)
    posr = jnp.broadcast_to(pos.reshape(1, -1), (SUBLANES, pos.size)).astype(BF16)
    return pl.pallas_call(
        functools.partial(_compress_kernel, apply_norm=apply_norm),
        out_shape=jax.ShapeDtypeStruct((b, g, nc, dh), F32),
        grid=(b, g),
        in_specs=[
            pl.BlockSpec((None, None, nc, half), lambda i, j: (i, j, 0, 0)),
            pl.BlockSpec((half, 2 * hid), lambda i, j: (0, 0)),
            pl.BlockSpec((SUBLANES, 2 * half), lambda i, j: (0, 0)),
            pl.BlockSpec((2 * half, hid), lambda i, j: (0, 0)),
            pl.BlockSpec((hid, dh), lambda i, j: (0, 0)),
            pl.BlockSpec((1, dh), lambda i, j: (0, 0)),
        ],
        out_specs=pl.BlockSpec((None, None, nc, dh), lambda i, j: (i, j, 0, 0)),
        compiler_params=_cparams(("parallel", "parallel")),
        name=name,
    )(x, w1c, posr, w1.astype(BF16), w2.astype(BF16), gain.reshape(1, dh).astype(F32))


def _pack_kv_kernel(k_ref, v_ref, gain_ref, kk_ref, v1_ref):
    kp = k_ref[...]
    vp = v_ref[...]
    lane = lax.broadcasted_iota(jnp.int32, kp.shape, 1)
    low = lane < NSA_HEAD_DIM
    sq = kp * kp
    ss_lo = jnp.sum(jnp.where(low, sq, 0.0), axis=-1, keepdims=True)
    ss_hi = jnp.sum(jnp.where(low, 0.0, sq), axis=-1, keepdims=True)
    ms = jnp.where(low, ss_lo, ss_hi) * (1.0 / NSA_HEAD_DIM)
    kn = kp * lax.rsqrt(ms + NORM_EPS) * gain_ref[...]
    kr = pltpu.roll(kn, NSA_HEAD_DIM, 1)
    vr = pltpu.roll(vp, NSA_HEAD_DIM, 1)
    kk_ref[0] = jnp.where(low, kn, kr).astype(kk_ref.dtype)
    kk_ref[1] = jnp.where(low, kr, kn).astype(kk_ref.dtype)
    v1_ref[0] = jnp.where(low, vp, 1.0).astype(v1_ref.dtype)
    v1_ref[1] = jnp.where(low, vr, 1.0).astype(v1_ref.dtype)


def _pack_kv(kvproj, gain, branch, batch, seq, *, tm=512, name="pack_kv"):
    tm = min(tm, seq)
    nt = seq // tm
    pairs = NSA_KV_HEADS // 2
    kcol = branch * 2 * pairs
    vcol = kcol + pairs
    gain2 = jnp.tile(gain.reshape(1, NSA_HEAD_DIM), (1, 2)).astype(F32)
    out = jax.ShapeDtypeStruct((batch, NSA_KV_HEADS, seq, LANES), BF16)
    ospec = pl.BlockSpec((None, 2, tm, LANES), lambda b, t, p: (b, p, t, 0))
    return pl.pallas_call(
        _pack_kv_kernel,
        out_shape=(out, out),
        grid=(batch, nt, pairs),
        in_specs=[
            pl.BlockSpec((tm, LANES), lambda b, t, p: (b * nt + t, kcol + p)),
            pl.BlockSpec((tm, LANES), lambda b, t, p: (b * nt + t, vcol + p)),
            pl.BlockSpec((1, LANES), lambda b, t, p: (0, 0)),
        ],
        out_specs=(ospec, ospec),
        compiler_params=_cparams(("parallel", "parallel", "parallel")),
        name=name,
    )(kvproj, kvproj, gain2)


def _rel_bucket_np(dist):
    max_exact = REL_BUCKETS // 2
    d = np.maximum(dist, 0)
    large = max_exact + (np.log(np.maximum(d, 1).astype(np.float32) / max_exact)
                         / math.log(REL_MAX_DIST / max_exact) * (REL_BUCKETS - max_exact)).astype(np.int32)
    large = np.minimum(large, REL_BUCKETS - 1)
    return np.where(d < max_exact, d, large).astype(np.int32)


def _bias_table(rel_bias, dist, valid):
    onehot = jax.nn.one_hot(_rel_bucket_np(dist), REL_BUCKETS, dtype=F32)
    vals = jnp.einsum("qkb,bh->hqk", onehot, rel_bias.astype(F32), precision=lax.Precision.HIGHEST)
    vals = jnp.where(jnp.asarray(valid)[None], vals, NEG_INF)
    return vals.reshape(NSA_KV_HEADS, NSA_HPG * dist.shape[0], dist.shape[1])


def _head_masks():
    lane = lax.broadcasted_iota(jnp.int32, (Q_BLOCK, LANES), 1)
    return lane < NSA_HEAD_DIM


def _stack_heads(q_ref, low):
    parts = []
    for pair in range(NSA_HPG // 2):
        qp = q_ref[:, pair * LANES:(pair + 1) * LANES].astype(F32)
        parts.append(jnp.where(low, qp, 0.0))
        parts.append(jnp.where(low, 0.0, qp))
    return jnp.concatenate(parts, axis=0).astype(BF16)


def _unstack_heads(x, low):
    parts = []
    for pair in range(NSA_HPG // 2):
        a = x[(2 * pair) * Q_BLOCK:(2 * pair + 1) * Q_BLOCK]
        b = x[(2 * pair + 1) * Q_BLOCK:(2 * pair + 2) * Q_BLOCK]
        parts.append(jnp.where(low, a, pltpu.roll(b, NSA_HEAD_DIM, 1)))
    return jnp.concatenate(parts, axis=1)


def _normalize_pv(pv):
    den = pltpu.roll(pv, NSA_HEAD_DIM, 1)
    return pv * jnp.where(den > 0.0, 1.0 / den, 0.0)


def _nsa_cmp_kernel(q_ref, kk_ref, v1_ref, bias_ref, ov_ref, o_ref, imp_ref, *, nc):
    qb = pl.program_id(2)
    start = pl.multiple_of(qb * CMP_LOOKBACK, SUBLANES)
    kk = kk_ref[pl.ds(start, nc), :].astype(BF16)
    v1 = v1_ref[pl.ds(start, nc), :].astype(BF16)
    low = _head_masks()
    jcol = lax.broadcasted_iota(jnp.int32, (Q_BLOCK, nc), 1)
    exists = jcol >= (nc - CMP_LOOKBACK - CMP_LOOKBACK * qb)
    s4 = _dot_nt(_stack_heads(q_ref, low), kk)
    psum = jnp.zeros((Q_BLOCK, nc), F32)
    ps = []
    for hp in range(NSA_HPG):
        rows = slice(hp * Q_BLOCK, (hp + 1) * Q_BLOCK)
        s = jnp.where(exists, s4[rows] + bias_ref[rows, :], NEG_INF)
        mx = jnp.max(s, axis=-1, keepdims=True)
        p = jnp.exp(s - mx)
        den = jnp.sum(p, axis=-1, keepdims=True)
        p = p * jnp.where(mx > 0.5 * NEG_INF, 1.0 / den, 0.0)
        psum = psum + p
        ps.append(p.astype(BF16))
    pv = _dot(jnp.concatenate(ps, axis=0), v1)
    o_ref[...] = _unstack_heads(pv, low).astype(o_ref.dtype)
    hi, lo = _split_bf16(psum)
    imp_ref[...] = _dot(jnp.concatenate([hi, lo], axis=1), ov_ref[...])


def _nsa_cmp(q, kvc, vkc, bias_c, ov2, batch, seq):
    nc = seq // CMP_STRIDE
    ns = seq // SLC_BLOCK
    nqb = seq // Q_BLOCK
    rows = kvc.shape[2]
    gw = NSA_HPG * NSA_HEAD_DIM
    return pl.pallas_call(
        functools.partial(_nsa_cmp_kernel, nc=nc),
        out_shape=(jax.ShapeDtypeStruct((batch, seq, NSA_HEADS * NSA_HEAD_DIM), BF16),
                   jax.ShapeDtypeStruct((batch, NSA_KV_HEADS, seq, ns), F32)),
        grid=(batch, NSA_KV_HEADS, nqb),
        in_specs=[
            pl.BlockSpec((None, Q_BLOCK, gw), lambda b, g, i: (b, i, g)),
            pl.BlockSpec((None, None, rows, LANES), lambda b, g, i: (b, g, 0, 0)),
            pl.BlockSpec((None, None, rows, LANES), lambda b, g, i: (b, g, 0, 0)),
            pl.BlockSpec((None, NSA_HPG * Q_BLOCK, nc), lambda b, g, i: (g, 0, 0)),
            pl.BlockSpec((2 * nc, ns), lambda b, g, i: (0, 0)),
        ],
        out_specs=(pl.BlockSpec((None, Q_BLOCK, gw), lambda b, g, i: (b, i, g)),
                   pl.BlockSpec((None, None, Q_BLOCK, ns), lambda b, g, i: (b, g, i, 0))),
        compiler_params=_cparams(("parallel", "parallel", "parallel")),
        name="nsa_compressed",
    )(q, kvc, vkc, bias_c, ov2)


def _topk_kernel(imp_ref, sel_ref, *, seq, ns, ntop):
    rows = imp_ref.shape[0]
    r0 = pl.program_id(0) * rows
    rid = r0 + lax.broadcasted_iota(jnp.int32, (rows, ns), 0)
    t = rid & (seq - 1)
    qb = t >> int(math.log2(Q_BLOCK))
    js = lax.broadcasted_iota(jnp.int32, (rows, ns), 1)
    js_first = ns - 2 - 2 * qb
    js_cur = ns - 2 + ((t & (Q_BLOCK - 1)) >> int(math.log2(SLC_BLOCK)))
    causal = (js >= js_first) & (js <= js_cur)
    forced = (js == js_first) | (js == js_cur) | (js == js_cur - 1)
    n_forced = 1 + (js_cur - 1 >= js_first).astype(jnp.int32) + (js_first < js_cur - 1).astype(jnp.int32)
    n_pick = jnp.minimum(ntop, js_cur - js_first + 1) - n_forced
    ninf = -jnp.inf
    score = jnp.where(causal & jnp.logical_not(forced), imp_ref[...], ninf)
    sel = jnp.where(causal & forced, 1.0, 0.0)
    jsf = js.astype(F32)
    for it in range(ntop - 1):
        mx = jnp.max(score, axis=-1, keepdims=True)
        first = jnp.min(jnp.where(score == mx, jsf, float(ns)), axis=-1, keepdims=True)
        hit = (jsf == first) & (it < n_pick)
        sel = jnp.where(hit, 1.0, sel)
        score = jnp.where(hit, ninf, score)
    sel_ref[...] = sel.astype(sel_ref.dtype)


def _topk_select(imp, seq, *, rows=1024):
    b, g, t, ns = imp.shape
    n = b * g * t
    rows = min(rows, n)
    assert n % rows == 0
    sel = pl.pallas_call(
        functools.partial(_topk_kernel, seq=seq, ns=ns, ntop=min(SLC_TOP, ns)),
        out_shape=jax.ShapeDtypeStruct((n, ns), BF16),
        grid=(n // rows,),
        in_specs=[pl.BlockSpec((rows, ns), lambda i: (i, 0))],
        out_specs=pl.BlockSpec((rows, ns), lambda i: (i, 0)),
        compiler_params=_cparams(("parallel",)),
        name="nsa_topk",
    )(imp.reshape(n, ns))
    return sel.reshape(b, g, t, ns)


def _nsa_slc_win_kernel(q_ref, sel_ref, kks_ref, v1s_ref, kkw_ref, v1w_ref, bias_ref, bwin_ref,
                        eexp_ref, gates_ref, egate_ref, ocmp_ref, o_ref, m_ref, acc_ref, p_ref, alpha_ref,
                        *, ns, kt_near, nkt):
    qb = pl.program_id(2)
    low = _head_masks()
    kw = SLC_KEY_TILE
    kt0 = (ns - 2 - 2 * qb) // SLC_BLOCKS_PER_TILE

    m_ref[...] = jnp.full_like(m_ref, NEG_INF)
    acc_ref[...] = jnp.zeros_like(acc_ref)
    p_ref[...] = jnp.zeros_like(p_ref)
    alpha_ref[...] = jnp.ones_like(alpha_ref)
    q4 = _stack_heads(q_ref, low)
    selb = sel_ref[...]

    def row_start(kt):
        return pl.multiple_of(jnp.maximum(Q_BLOCK * qb + kw * kt, 0), Q_BLOCK)

    def accumulate(kt):
        pv = _dot(p_ref[...], v1s_ref[pl.ds(row_start(kt), kw), :])
        acc_ref[...] = alpha_ref[...] * acc_ref[...] + pv

    def body(kt):
        s4 = _dot_nt(q4, kks_ref[pl.ds(row_start(kt), kw), :])
        mb = (_dot(selb, eexp_ref[kt]) - 1.0) * (-NEG_INF)
        accumulate(kt - 1)
        bt = jnp.maximum(kt - kt_near + 1, 0)
        for hp in range(NSA_HPG):
            rows = slice(hp * Q_BLOCK, (hp + 1) * Q_BLOCK)
            s = s4[rows] + bias_ref[bt, rows, :] + mb
            m_old = m_ref[rows, :]
            m_new = jnp.maximum(m_old, jnp.max(s, axis=-1, keepdims=True))
            alpha_ref[rows, :] = jnp.exp(m_old - m_new)
            p_ref[rows, :] = jnp.exp(s - m_new[:, 0:1]).astype(BF16)
            m_ref[rows, :] = m_new

    kt_even = (kt0 // 2) * 2

    def body2(j, carry):
        body(kt_even + 2 * j)
        body(kt_even + 2 * j + 1)
        return carry

    lax.fori_loop(0, (nkt - kt_even) // 2, body2, 0)
    accumulate(nkt - 1)
    o_slc = _unstack_heads(_normalize_pv(acc_ref[...]), low)

    wk = WINDOW + Q_BLOCK
    ws = pl.multiple_of(Q_BLOCK * qb, Q_BLOCK)
    sw = _dot_nt(q4, kkw_ref[pl.ds(ws, wk), :])
    wcol = lax.broadcasted_iota(jnp.int32, (Q_BLOCK, wk), 1)
    in_seq = wcol >= (WINDOW - Q_BLOCK * qb)
    ps = []
    for hp in range(NSA_HPG):
        rows = slice(hp * Q_BLOCK, (hp + 1) * Q_BLOCK)
        s = jnp.where(in_seq, sw[rows] + bwin_ref[rows, :], NEG_INF)
        ps.append(jnp.exp(s - jnp.max(s, axis=-1, keepdims=True)).astype(BF16))
    pvw = _dot(jnp.concatenate(ps, axis=0), v1w_ref[pl.ds(ws, wk), :])
    o_win = _unstack_heads(_normalize_pv(pvw), low)

    gh, gl = _split_bf16(gates_ref[...])
    g2 = jnp.concatenate([gh, gl], axis=1)
    o = (_dot(g2, egate_ref[0]) * ocmp_ref[...].astype(F32)
         + _dot(g2, egate_ref[1]) * o_slc
         + _dot(g2, egate_ref[2]) * o_win)
    o_ref[...] = o.astype(o_ref.dtype)


def _nsa_slc_win(q, sel, kvs, vks, kvw, vkw, bias_s, bwin, eexp, gates, egate, ocmp, batch, seq):
    ns = seq // SLC_BLOCK
    nqb = seq // Q_BLOCK
    nkt = (ns * SLC_BLOCK) // SLC_KEY_TILE
    kt_near = nkt - (bias_s.shape[1] - 1)
    gw = NSA_HPG * NSA_HEAD_DIM
    srows = kvs.shape[2]
    wrows = kvw.shape[2]
    slab = lambda r: pl.BlockSpec((None, None, r, LANES), lambda b, g, i: (b, g, 0, 0))
    return pl.pallas_call(
        functools.partial(_nsa_slc_win_kernel, ns=ns, kt_near=kt_near, nkt=nkt),
        out_shape=jax.ShapeDtypeStruct((batch, seq, NSA_HEADS * NSA_HEAD_DIM), BF16),
        grid=(batch, NSA_KV_HEADS, nqb),
        in_specs=[
            pl.BlockSpec((None, Q_BLOCK, gw), lambda b, g, i: (b, i, g)),
            pl.BlockSpec((None, None, Q_BLOCK, ns), lambda b, g, i: (b, g, i, 0)),
            slab(srows), slab(srows), slab(wrows), slab(wrows),
            pl.BlockSpec((None,) + bias_s.shape[1:], lambda b, g, i: (g, 0, 0, 0)),
            pl.BlockSpec((None,) + bwin.shape[1:], lambda b, g, i: (g, 0, 0)),
            pl.BlockSpec(eexp.shape, lambda b, g, i: (0, 0, 0)),
            pl.BlockSpec((None, Q_BLOCK, LANES), lambda b, g, i: (b, i, 0)),
            pl.BlockSpec((N_BRANCH, 2 * LANES, gw), lambda b, g, i: (0, 0, g)),
            pl.BlockSpec((None, Q_BLOCK, gw), lambda b, g, i: (b, i, g)),
        ],
        out_specs=pl.BlockSpec((None, Q_BLOCK, gw), lambda b, g, i: (b, i, g)),
        scratch_shapes=[pltpu.VMEM((NSA_HPG * Q_BLOCK, LANES), F32),
                        pltpu.VMEM((NSA_HPG * Q_BLOCK, LANES), F32),
                        pltpu.VMEM((NSA_HPG * Q_BLOCK, SLC_KEY_TILE), BF16),
                        pltpu.VMEM((NSA_HPG * Q_BLOCK, LANES), F32)],
        compiler_params=_cparams(("parallel", "parallel", "arbitrary")),
        name="nsa_selected_window",
    )(q, sel, kvs, vks, kvw, vkw, bias_s, bwin, eexp, gates, egate, ocmp)


def _nsa_tables(rel_bias, seq):
    nc = seq // CMP_STRIDE
    ns = seq // SLC_BLOCK
    i = np.arange(Q_BLOCK)[:, None]
    j = np.arange(nc)[None, :]
    dist_c = i - (CMP_BLOCK - 1) - CMP_STRIDE * (j - (nc - CMP_LOOKBACK))
    bias_c = _bias_table(rel_bias, dist_c, dist_c >= 0)
    cs = np.arange(nc)[:, None] * CMP_STRIDE
    ss = np.arange(ns)[None, :] * SLC_BLOCK
    ov = np.clip(np.minimum(cs + CMP_BLOCK, ss + SLC_BLOCK) - np.maximum(cs, ss), 0, None) / CMP_BLOCK
    ov2 = jnp.asarray(np.concatenate([ov, ov], axis=0), BF16)
    nkeys = ns * SLC_BLOCK
    nkt = nkeys // SLC_KEY_TILE
    pad_rows = (ns - 2) * SLC_BLOCK
    kt_near = max(0, pad_rows - REL_MAX_DIST) // SLC_KEY_TILE
    kr = np.arange(kt_near * SLC_KEY_TILE, nkeys)[None, :]
    dist_s = i + pad_rows - kr
    near = _bias_table(rel_bias, dist_s, dist_s >= 0)
    near = near.reshape(NSA_KV_HEADS, NSA_HPG * Q_BLOCK, nkt - kt_near, SLC_KEY_TILE).transpose(0, 2, 1, 3)
    far = jnp.broadcast_to(rel_bias.astype(F32)[REL_BUCKETS - 1].reshape(NSA_KV_HEADS, 1, NSA_HPG, 1, 1),
                           (NSA_KV_HEADS, 1, NSA_HPG, Q_BLOCK, SLC_KEY_TILE))
    bias_s = jnp.concatenate([far.reshape(NSA_KV_HEADS, 1, NSA_HPG * Q_BLOCK, SLC_KEY_TILE), near], axis=1)
    dist_w = i + WINDOW - np.arange(WINDOW + Q_BLOCK)[None, :]
    bwin = _bias_table(rel_bias, dist_w, (dist_w >= 0) & (dist_w < WINDOW))
    blk = np.arange(ns)[None, :, None]
    key = np.arange(SLC_KEY_TILE)[None, None, :]
    ktile = np.arange(nkt)[:, None, None]
    eexp = jnp.asarray(blk == ktile * SLC_BLOCKS_PER_TILE + key // SLC_BLOCK, BF16)
    col = np.arange(LANES)[:, None]
    head = (np.arange(NSA_HEADS * NSA_HEAD_DIM) // NSA_HEAD_DIM)[None, :]
    eg = np.stack([col == head * N_BRANCH + br for br in range(N_BRANCH)])
    egate = jnp.asarray(np.concatenate([eg, eg], axis=1), BF16)
    return bias_c, ov2, bias_s, bwin, eexp, egate


def _router_kernel(x_ref, g_ref, w_ref, o_ref):
    x = x_ref[...]
    ms = jnp.mean(x * x, axis=-1, keepdims=True)
    xn = x * lax.rsqrt(ms + NORM_EPS) * g_ref[...]
    logits = jnp.dot(xn, w_ref[...], preferred_element_type=F32, precision=lax.Precision.HIGHEST)
    lane = lax.broadcasted_iota(jnp.int32, logits.shape, 1).astype(F32)
    ninf = -jnp.inf
    s = jnp.where(lane < N_EXPERTS, logits, ninf)
    m1 = jnp.max(s, axis=-1, keepdims=True)
    i1 = jnp.min(jnp.where(s == m1, lane, float(LANES)), axis=-1, keepdims=True)
    s2 = jnp.where(lane == i1, ninf, s)
    m2 = jnp.max(s2, axis=-1, keepdims=True)
    i2 = jnp.min(jnp.where(s2 == m2, lane, float(LANES)), axis=-1, keepdims=True)
    e2 = jnp.exp(m2 - m1)
    w1 = 1.0 / (1.0 + e2)
    w2 = e2 * w1
    o_ref[...] = (jnp.where(lane == 0.0, i1, 0.0) + jnp.where(lane == 1.0, i2, 0.0)
                  + jnp.where(lane == 2.0, w1, 0.0) + jnp.where(lane == 3.0, w2, 0.0))


def _router(x, g, w_router, *, tm=512):
    n, d = x.shape
    tm = min(tm, n)
    wpad = jnp.zeros((d, LANES), F32).at[:, :N_EXPERTS].set(w_router.astype(F32))
    return pl.pallas_call(
        _router_kernel,
        out_shape=jax.ShapeDtypeStruct((n, LANES), F32),
        grid=(n // tm,),
        in_specs=[
            pl.BlockSpec((tm, d), lambda i: (i, 0)),
            pl.BlockSpec((1, d), lambda i: (0, 0)),
            pl.BlockSpec((d, LANES), lambda i: (0, 0)),
        ],
        out_specs=pl.BlockSpec((tm, LANES), lambda i: (i, 0)),
        compiler_params=_cparams(("parallel",)),
        name="moe_router",
    )(x, g.reshape(1, d).astype(F32), wpad)


def _moe_dispatch(route, tm):
    n = route.shape[0]
    pairs = 2 * n
    e = route[:, 0:2].astype(jnp.int32).reshape(pairs)
    onehot = (e[:, None] == jnp.arange(N_EXPERTS, dtype=jnp.int32)[None, :]).astype(jnp.int32)
    csum = jnp.cumsum(onehot, axis=0)
    rank = jnp.sum(csum * onehot, axis=1) - 1
    counts = csum[-1]
    cpad = ((counts + tm - 1) // tm) * tm
    gend = jnp.cumsum(cpad)
    dst = jnp.sum(onehot * (gend - cpad)[None, :], axis=1) + rank
    rows = pairs + N_EXPERTS * tm
    ntiles = rows // tm
    nvalid = gend[-1] // tm
    tile = jnp.arange(ntiles, dtype=jnp.int32)
    te = jnp.sum((tile[:, None] * tm >= gend[None, :]).astype(jnp.int32), axis=1)
    te = jnp.where(tile < nvalid, te, te[nvalid - 1])
    src = jnp.zeros((rows,), jnp.int32).at[dst].set(jnp.arange(pairs, dtype=jnp.int32) // 2)
    return src, dst, te.astype(jnp.int32), nvalid.reshape(1).astype(jnp.int32)


def _row_copy(src_hbm, row, dst_ref, j, sem):
    return pltpu.make_async_copy(src_hbm.at[pl.ds(row, 1), :], dst_ref.at[pl.ds(j, 1), :], sem)


def _gather_rows_kernel(idx_ref, src_hbm, o_ref, sem):
    rows = o_ref.shape[0]

    def issue(j, c):
        _row_copy(src_hbm, idx_ref[j], o_ref, j, sem).start()
        return c

    lax.fori_loop(0, rows, issue, 0, unroll=8)
    pltpu.make_async_copy(src_hbm.at[pl.ds(0, rows), :], o_ref, sem).wait()


def _gather_rows(x, idx, *, rows=256):
    n, d = x.shape
    total = idx.shape[0]
    assert total % rows == 0
    return pl.pallas_call(
        _gather_rows_kernel,
        out_shape=jax.ShapeDtypeStruct((total, d), x.dtype),
        grid=(total // rows,),
        in_specs=[
            pl.BlockSpec((rows,), lambda i: (i,), memory_space=pltpu.SMEM),
            pl.BlockSpec(memory_space=pl.ANY),
        ],
        out_specs=pl.BlockSpec((rows, d), lambda i: (i, 0)),
        scratch_shapes=[pltpu.SemaphoreType.DMA],
        compiler_params=_cparams(("arbitrary",)),
        name="moe_gather",
    )(idx, x)


def _combine_kernel(i0_ref, i1_ref, y_hbm, h_ref, route_ref, o_ref, buf_ref, sem):
    rows = h_ref.shape[0]

    def issue(j, c):
        _row_copy(y_hbm, i0_ref[j], buf_ref.at[0], j, sem).start()
        _row_copy(y_hbm, i1_ref[j], buf_ref.at[1], j, sem).start()
        return c

    lax.fori_loop(0, rows, issue, 0, unroll=8)
    for k in range(2):
        pltpu.make_async_copy(y_hbm.at[pl.ds(0, rows), :], buf_ref.at[k], sem).wait()
    route = route_ref[...]
    o_ref[...] = h_ref[...] + route[:, 2:3] * buf_ref[0] + route[:, 3:4] * buf_ref[1]


def _combine(h, y, dst, route, *, rows=256):
    n, d = h.shape
    assert n % rows == 0
    dst2 = dst.reshape(n, 2)
    return pl.pallas_call(
        _combine_kernel,
        out_shape=jax.ShapeDtypeStruct((n, d), F32),
        grid=(n // rows,),
        in_specs=[
            pl.BlockSpec((rows,), lambda i: (i,), memory_space=pltpu.SMEM),
            pl.BlockSpec((rows,), lambda i: (i,), memory_space=pltpu.SMEM),
            pl.BlockSpec(memory_space=pl.ANY),
            pl.BlockSpec((rows, d), lambda i: (i, 0)),
            pl.BlockSpec((rows, LANES), lambda i: (i, 0)),
        ],
        out_specs=pl.BlockSpec((rows, d), lambda i: (i, 0)),
        scratch_shapes=[pltpu.VMEM((2, rows, d), F32), pltpu.SemaphoreType.DMA],
        compiler_params=_cparams(("arbitrary",)),
        name="moe_combine",
    )(dst2[:, 0], dst2[:, 1], y, h, route)


def _ffn_grouped_kernel(te_ref, nv_ref, x_ref, g_ref, wg_ref, wu_ref, wd_ref, o_ref, xn_ref, acc_ref):
    i = pl.program_id(0)
    f = pl.program_id(1)
    used = i < nv_ref[0]

    @pl.when(jnp.logical_and(used, f == 0))
    def _():
        x = x_ref[...]
        ms = jnp.mean(x * x, axis=-1, keepdims=True)
        xn_ref[...] = (x * lax.rsqrt(ms + NORM_EPS) * g_ref[...]).astype(BF16)
        acc_ref[...] = jnp.zeros_like(acc_ref)

    @pl.when(used)
    def _():
        xn = xn_ref[...]
        gate = _dot(xn, wg_ref[...])
        up = _dot(xn, wu_ref[...])
        act = (gate * jax.nn.sigmoid(gate) * up).astype(BF16)
        acc_ref[...] += _dot(act, wd_ref[...])

    last = f == pl.num_programs(1) - 1

    @pl.when(jnp.logical_and(used, last))
    def _():
        o_ref[...] = acc_ref[...]

    @pl.when(jnp.logical_and(jnp.logical_not(used), last))
    def _():
        o_ref[...] = jnp.zeros_like(o_ref)


def _ffn_grouped(x, g, w_gu, w_down, te, nvalid, *, tm, tf=None):
    rows, d = x.shape
    ff = w_down.shape[1]
    if tf is None:
        tf = ff // 2 if (ff // 2) % LANES == 0 else ff
    assert rows % tm == 0 and ff % tf == 0 and tf % LANES == 0
    nf = ff // tf

    def fcol(i, f, te_ref, nv_ref):
        return jnp.where(i < nv_ref[0], f, nf - 1)

    grid_spec = pltpu.PrefetchScalarGridSpec(
        num_scalar_prefetch=2,
        grid=(rows // tm, nf),
        in_specs=[
            pl.BlockSpec((tm, d), lambda i, f, te_ref, nv_ref: (i, 0)),
            pl.BlockSpec((1, d), lambda i, f, te_ref, nv_ref: (0, 0)),
            pl.BlockSpec((None, d, tf), lambda i, f, te_ref, nv_ref: (te_ref[i], 0, fcol(i, f, te_ref, nv_ref))),
            pl.BlockSpec((None, d, tf),
                         lambda i, f, te_ref, nv_ref: (te_ref[i], 0, nf + fcol(i, f, te_ref, nv_ref))),
            pl.BlockSpec((None, tf, d), lambda i, f, te_ref, nv_ref: (te_ref[i], fcol(i, f, te_ref, nv_ref), 0)),
        ],
        out_specs=pl.BlockSpec((tm, d), lambda i, f, te_ref, nv_ref: (i, 0)),
        scratch_shapes=[pltpu.VMEM((tm, d), BF16), pltpu.VMEM((tm, d), F32)],
    )
    return pl.pallas_call(
        _ffn_grouped_kernel,
        out_shape=jax.ShapeDtypeStruct((rows, d), F32),
        grid_spec=grid_spec,
        compiler_params=_cparams(("arbitrary", "arbitrary")),
        name="expert_ffn_grouped",
    )(te, nvalid, x, g.reshape(1, d).astype(F32), w_gu, w_gu, w_down)


def _nsa_layer(h, batch, seq, norm_g, kv_norm_g, w_kv, k_norm_g, cmp_pos_k, cmp_w1_k, cmp_w2_k,
               cmp_pos_v, cmp_w1_v, cmp_w2_v, w_in, w_out, q_norm_g, rel_bias):
    n, d = h.shape
    g_heads, dh = NSA_KV_HEADS, NSA_HEAD_DIM
    nc = seq // CMP_STRIDE
    ns = seq // SLC_BLOCK
    hq = NSA_HEADS * dh

    kvproj = _norm_matmul(h, kv_norm_g, w_kv.astype(BF16), name="nsa_kv_proj")

    kv6 = kvproj.reshape(batch, seq, N_BRANCH, 2, g_heads, dh)

    def chunks(t):
        return t.transpose(0, 2, 1, 3).reshape(batch, g_heads, nc, CMP_STRIDE * dh).astype(BF16)

    k_cmp = _compress(chunks(kv6[:, :, 0, 0]), cmp_w1_k, cmp_pos_k, cmp_w2_k, k_norm_g[0],
                      apply_norm=True, name="nsa_compress_k")
    v_cmp = _compress(chunks(kv6[:, :, 0, 1]), cmp_w1_v, cmp_pos_v, cmp_w2_v, k_norm_g[0],
                      apply_norm=False, name="nsa_compress_v")
    cpad = nc - CMP_LOOKBACK
    padc = lambda t: jnp.pad(t, ((0, 0), (0, 0), (cpad, 0), (0, 0)))
    kvc = padc(jnp.concatenate([k_cmp, k_cmp], axis=-1))
    vkc = padc(jnp.concatenate([v_cmp, jnp.ones_like(v_cmp)], axis=-1))

    kvs, vks = _pack_kv(kvproj, k_norm_g[1], 1, batch, seq, name="nsa_pack_selected")
    kvw, vkw = _pack_kv(kvproj, k_norm_g[2], 2, batch, seq, name="nsa_pack_window")
    spad = (ns - 2) * SLC_BLOCK
    pads = lambda t, r: jnp.pad(t, ((0, 0), (0, 0), (r, 0), (0, 0)))
    kvs, vks = pads(kvs, spad), pads(vks, spad)
    kvw, vkw = pads(kvw, WINDOW), pads(vkw, WINDOW)

    qgain = jnp.tile(q_norm_g.astype(F32), NSA_HEADS) * (dh ** -0.5)
    q = _norm_matmul(h, norm_g, w_in[:, :hq].astype(BF16), epilogue="headnorm", gain=qgain,
                     flag=jnp.ones((hq,), F32), out_dtype=BF16, name="nsa_q_proj")
    ngate = N_BRANCH * NSA_HEADS
    wg = jnp.zeros((d, LANES), BF16).at[:, :ngate].set(w_in[:, hq:].astype(BF16))
    gates = _norm_matmul(h, norm_g, wg, epilogue="sigmoid", name="nsa_gate_proj")

    bias_c, ov2, bias_s, bwin, eexp, egate = _nsa_tables(rel_bias, seq)
    q3 = q.reshape(batch, seq, hq)
    o_cmp, imp = _nsa_cmp(q3, kvc, vkc, bias_c, ov2, batch, seq)
    sel = _topk_select(imp, seq)
    o = _nsa_slc_win(q3, sel, kvs, vks, kvw, vkw, bias_s, bwin, eexp,
                     gates.reshape(batch, seq, LANES), egate, o_cmp, batch, seq)
    return _matmul_res(o.reshape(n, hq), w_out.astype(BF16), h, name="nsa_out_proj")


def kernel(x, norm_mix_g, norm_ffn_g, a_w_in, a_w_out, a_onorm_g, lb_param, kv_norm_g, w_kv, k_norm_g,
           cmp_pos_k, cmp_w1_k, cmp_w2_k, cmp_pos_v, cmp_w1_v, cmp_w2_v, b_w_in, b_w_out, b_qnorm_g,
           rel_bias, ffn_w_gu, ffn_w_down, moe_router, moe_w_gu, moe_w_down):
    batch, seq, d = x.shape
    n = batch * seq
    h = x.reshape(n, d).astype(F32)

    lower = jnp.cumsum(jax.nn.softmax(lb_param.astype(F32), axis=0), axis=0)[0]
    proj = _norm_matmul(h, norm_mix_g[0], a_w_in[0].astype(BF16), name="hgrn_in_proj")
    og = _hgrn_mixer(proj, lower, a_onorm_g[0], batch, seq)
    h = _matmul_res(og, a_w_out[0].astype(BF16), h, name="hgrn_out_proj")
    h = _ffn(h, norm_ffn_g[0], ffn_w_gu[0].astype(BF16), ffn_w_down[0].astype(BF16), name="dense_ffn")

    h = _nsa_layer(h, batch, seq, norm_mix_g[1], kv_norm_g, w_kv, k_norm_g, cmp_pos_k, cmp_w1_k, cmp_w2_k,
                   cmp_pos_v, cmp_w1_v, cmp_w2_v, b_w_in[0], b_w_out[0], b_qnorm_g[0], rel_bias)
    route = _router(h, norm_ffn_g[1], moe_router[0])
    src, dst, tile_expert, tiles_used = _moe_dispatch(route, MOE_ROW_TILE)
    ys = _ffn_grouped(_gather_rows(h, src), norm_ffn_g[1], moe_w_gu[0].astype(BF16), moe_w_down[0].astype(BF16),
                      tile_expert, tiles_used, tm=MOE_ROW_TILE)
    out = _combine(h, ys, dst, route)
    return out.reshape(batch, seq, d).astype(x.dtype)
```

```python
import functools
import math

import jax
import jax.numpy as jnp
import numpy as np
from jax import lax
from jax.experimental import pallas as pl
from jax.experimental.pallas import tpu as pltpu

F32 = jnp.float32
BF16 = jnp.bfloat16

NORM_EPS = 1e-6
NEG_INF = -1e30
FORCE_BONUS = 1e4
HGRN_HEAD_DIM = 128
NSA_HEADS = 16
NSA_KV_HEADS = 4
NSA_HPG = NSA_HEADS // NSA_KV_HEADS
NSA_HEAD_DIM = 64
N_BRANCH = 3
CMP_BLOCK = 32
CMP_STRIDE = 16
SLC_BLOCK = 64
SLC_TOP = 16
WINDOW = 512
Q_BLOCK = 128
REL_BUCKETS = 32
REL_MAX_DIST = 2048
N_EXPERTS = 8

LANES = 128
SUBLANES = 8
VMEM_LIMIT_BYTES = 56 * 1024 * 1024

HGRN_CHUNK = 128
HGRN_ROWS_PER_STEP = 512
HGRN_HEADS_PER_STEP = 4
SLC_KEY_TILE = 512
SLC_BLOCKS_PER_TILE = SLC_KEY_TILE // SLC_BLOCK
CMP_LOOKBACK = Q_BLOCK // CMP_STRIDE
CMP_QBLOCKS_PER_STEP = 2
MOE_ROW_TILE = 512


def _cparams(sem):
    return pltpu.CompilerParams(dimension_semantics=sem, vmem_limit_bytes=VMEM_LIMIT_BYTES)


def _dot(a, b):
    return jnp.dot(a, b, preferred_element_type=F32)


def _dot_nt(a, b):
    return lax.dot_general(a, b, (((1,), (1,)), ((), ())), preferred_element_type=F32)


def _split_bf16(x):
    hi = x.astype(BF16)
    lo = (x - hi.astype(F32)).astype(BF16)
    return hi, lo


def _norm_matmul_kernel(x_ref, g_ref, w_ref, *rest, epilogue, tn):
    if epilogue == "headnorm":
        gain_ref, flag_ref, bd_ref, o_ref = rest
    else:
        (o_ref,) = rest
    x = x_ref[...]
    ms = jnp.mean(x * x, axis=-1, keepdims=True)
    xn = (x * lax.rsqrt(ms + NORM_EPS) * g_ref[...]).astype(BF16)
    for c in range(o_ref.shape[1] // tn):
        cols = slice(c * tn, (c + 1) * tn)
        acc = _dot(xn, w_ref[:, cols])
        if epilogue == "headnorm":
            ss = _dot((acc * acc).astype(BF16), bd_ref[...])
            normed = acc * lax.rsqrt(ss * (1.0 / NSA_HEAD_DIM) + NORM_EPS) * gain_ref[:, cols]
            acc = jnp.where(flag_ref[:, cols] > 0.0, normed, acc)
        elif epilogue == "sigmoid":
            acc = jax.nn.sigmoid(acc)
        o_ref[:, cols] = acc.astype(o_ref.dtype)


def _norm_matmul(x, g, w, *, epilogue="none", gain=None, flag=None, out_dtype=F32, tm=512, tn=512,
                 name="norm_matmul"):
    n, d = x.shape
    m = w.shape[1]
    tm = min(tm, n)
    tn = min(tn, m)
    assert n % tm == 0 and m % tn == 0, (n, tm, m, tn)
    const = lambda i: (0, 0)
    in_specs = [pl.BlockSpec((tm, d), lambda i: (i, 0)), pl.BlockSpec((1, d), const), pl.BlockSpec((d, m), const)]
    args = [x, g.reshape(1, d).astype(F32), w]
    if epilogue == "headnorm":
        assert tn % NSA_HEAD_DIM == 0
        grp = np.arange(tn) // NSA_HEAD_DIM
        bd = jnp.asarray(grp[:, None] == grp[None, :], BF16)
        in_specs += [pl.BlockSpec((1, m), const), pl.BlockSpec((1, m), const), pl.BlockSpec((tn, tn), const)]
        args += [gain.reshape(1, m).astype(F32), flag.reshape(1, m).astype(F32), bd]
    return pl.pallas_call(
        functools.partial(_norm_matmul_kernel, epilogue=epilogue, tn=tn),
        out_shape=jax.ShapeDtypeStruct((n, m), out_dtype),
        grid=(n // tm,),
        in_specs=in_specs,
        out_specs=pl.BlockSpec((tm, m), lambda i: (i, 0)),
        compiler_params=_cparams(("parallel",)),
        name=name,
    )(*args)


def _matmul_res_kernel(a_ref, w_ref, r_ref, o_ref, *, tn):
    a = a_ref[...]
    for c in range(o_ref.shape[1] // tn):
        cols = slice(c * tn, (c + 1) * tn)
        o_ref[:, cols] = r_ref[:, cols] + _dot(a, w_ref[:, cols])


def _matmul_res(a, w, res, *, tm=512, tn=512, name="matmul_res"):
    n, k = a.shape
    m = w.shape[1]
    tm = min(tm, n)
    tn = min(tn, m)
    assert n % tm == 0 and m % tn == 0
    return pl.pallas_call(
        functools.partial(_matmul_res_kernel, tn=tn),
        out_shape=jax.ShapeDtypeStruct((n, m), F32),
        grid=(n // tm,),
        in_specs=[
            pl.BlockSpec((tm, k), lambda i: (i, 0)),
            pl.BlockSpec((k, m), lambda i: (0, 0)),
            pl.BlockSpec((tm, m), lambda i: (i, 0)),
        ],
        out_specs=pl.BlockSpec((tm, m), lambda i: (i, 0)),
        compiler_params=_cparams(("parallel",)),
        name=name,
    )(a, w, res)


def _ffn_kernel(x_ref, g_ref, wg_ref, wu_ref, wd_ref, o_ref, xn_ref, acc_ref):
    f = pl.program_id(1)

    @pl.when(f == 0)
    def _():
        x = x_ref[...]
        ms = jnp.mean(x * x, axis=-1, keepdims=True)
        xn_ref[...] = (x * lax.rsqrt(ms + NORM_EPS) * g_ref[...]).astype(BF16)
        acc_ref[...] = jnp.zeros_like(acc_ref)

    xn = xn_ref[...]
    gate = _dot(xn, wg_ref[...])
    up = _dot(xn, wu_ref[...])
    act = (gate * jax.nn.sigmoid(gate) * up).astype(BF16)
    acc_ref[...] += _dot(act, wd_ref[...])

    @pl.when(f == pl.num_programs(1) - 1)
    def _():
        o_ref[...] = x_ref[...] + acc_ref[...]


def _ffn(x, g, w_gu, w_down, *, tm=512, tf=None, name="ffn"):
    n, d = x.shape
    ff = w_down.shape[0]
    if tf is None:
        tf = ff // 2 if (ff // 2) % LANES == 0 else ff
    tm = min(tm, n)
    assert n % tm == 0 and ff % tf == 0 and tf % LANES == 0
    nf = ff // tf
    return pl.pallas_call(
        _ffn_kernel,
        out_shape=jax.ShapeDtypeStruct((n, d), F32),
        grid=(n // tm, nf),
        in_specs=[
            pl.BlockSpec((tm, d), lambda i, f: (i, 0)),
            pl.BlockSpec((1, d), lambda i, f: (0, 0)),
            pl.BlockSpec((d, tf), lambda i, f: (0, f)),
            pl.BlockSpec((d, tf), lambda i, f: (0, nf + f)),
            pl.BlockSpec((tf, d), lambda i, f: (f, 0)),
        ],
        out_specs=pl.BlockSpec((tm, d), lambda i, f: (i, 0)),
        scratch_shapes=[pltpu.VMEM((tm, d), BF16), pltpu.VMEM((tm, d), F32)],
        compiler_params=_cparams(("parallel", "arbitrary")),
        name=name,
    )(x, g.reshape(1, d).astype(F32), w_gu, w_gu, w_down)


def _hgrn_decay_matrix(c):
    levels = int(math.log2(c))
    out = np.zeros(((levels + 2) * c, c), np.float32)
    for l in range(levels):
        m = c >> (l + 1)
        for r in range(c):
            mid = (r // (2 * m)) * 2 * m + m - 1
            if r % (2 * m) >= m:
                out[l * c + r, mid + 1:r + 1] = 1.0
            else:
                out[l * c + r, r + 1:mid + 1] = 1.0
    for r in range(c):
        out[levels * c + r, :r + 1] = 1.0
        out[(levels + 1) * c + r, r + 1:] = 1.0
    return out


def _hgrn_kernel(q_ref, f_ref, v_ref, g_ref, lb_ref, gn_ref, m_ref, o_ref, st_ref, *, chunk, nchunk):
    c = chunk
    dh = HGRN_HEAD_DIM
    levels = int(math.log2(c))

    @pl.when(pl.program_id(2) == 0)
    def _():
        st_ref[...] = jnp.zeros_like(st_ref)

    gn = gn_ref[...]
    row = lax.broadcasted_iota(jnp.int32, (c, c), 0)
    col = lax.broadcasted_iota(jnp.int32, (c, c), 1)
    rowv = lax.broadcasted_iota(jnp.int32, (c, dh), 0)
    nheads = st_ref.shape[0]

    def body(ci, carry):
        sl = pl.ds(pl.multiple_of(ci * c, c), c)
        lb = lb_ref[...]
        fg = lb + (1.0 - lb) * jax.nn.sigmoid(f_ref[sl, :])
        kall = 1.0 - fg
        hi, lo = _split_bf16(jnp.log(fg))
        dall = _dot(m_ref[...], jnp.concatenate([hi, lo], axis=0))
        hs = range(nheads)
        cols = [slice(hh * dh, (hh + 1) * dh) for hh in hs]
        q = [q_ref[sl, cols[hh]] for hh in hs]
        k = [kall[:, cols[hh]] for hh in hs]
        v = [v_ref[sl, cols[hh]] for hh in hs]

        a = [jnp.where(row == col, _dot_nt(q[hh].astype(BF16), k[hh].astype(BF16)), 0.0) for hh in hs]
        for l in range(levels):
            sh = levels - 1 - l
            upper = ((rowv >> sh) & 1) == 1
            same = (row >> (sh + 1)) == (col >> (sh + 1))
            for hh in hs:
                e = jnp.exp(dall[l * c:(l + 1) * c, cols[hh]])
                qe = jnp.where(upper, q[hh] * e, 0.0).astype(BF16)
                ke = jnp.where(upper, 0.0, k[hh] * e).astype(BF16)
                a[hh] = a[hh] + jnp.where(same, _dot_nt(qe, ke), 0.0)
        b = [dall[levels * c:(levels + 1) * c, cols[hh]] for hh in hs]
        st = [st_ref[hh] for hh in hs]
        o = [_dot(a[hh].astype(BF16), v[hh].astype(BF16))
             + _dot_nt((q[hh] * jnp.exp(b[hh])).astype(BF16), st[hh].astype(BF16)) for hh in hs]
        for hh in hs:
            kr = (k[hh] * jnp.exp(dall[(levels + 1) * c:, cols[hh]])).astype(BF16)
            st_ref[hh] = st[hh] * jnp.exp(b[hh][c - 1:c, :]) + _dot(v[hh].T.astype(BF16), kr)
        for hh in hs:
            ms = jnp.mean(o[hh] * o[hh], axis=-1, keepdims=True)
            on = o[hh] * lax.rsqrt(ms + NORM_EPS) * gn
            gg = g_ref[sl, cols[hh]]
            o_ref[sl, cols[hh]] = (on * (gg * jax.nn.sigmoid(gg))).astype(o_ref.dtype)
        return carry

    lax.fori_loop(0, nchunk, body, 0)


def _hgrn_mixer(proj, lb, gn, batch, seq):
    n, d4 = proj.shape
    d = d4 // 4
    heads = d // HGRN_HEAD_DIM
    rows = min(HGRN_ROWS_PER_STEP, seq)
    chunk = min(HGRN_CHUNK, rows)
    assert seq % rows == 0 and rows % chunk == 0
    nt = seq // rows
    m1 = _hgrn_decay_matrix(chunk)
    m = jnp.asarray(np.concatenate([m1, m1], axis=1), BF16)

    hps = HGRN_HEADS_PER_STEP
    assert heads % hps == 0
    groups = heads // hps
    width = hps * HGRN_HEAD_DIM

    def col_spec(s):
        return pl.BlockSpec((rows, width), lambda b, h, t, s=s: (b * nt + t, s * groups + h))

    return pl.pallas_call(
        functools.partial(_hgrn_kernel, chunk=chunk, nchunk=rows // chunk),
        out_shape=jax.ShapeDtypeStruct((n, d), BF16),
        grid=(batch, groups, nt),
        in_specs=[
            col_spec(0), col_spec(1), col_spec(2), col_spec(3),
            pl.BlockSpec((1, width), lambda b, h, t: (0, h)),
            pl.BlockSpec((1, HGRN_HEAD_DIM), lambda b, h, t: (0, 0)),
            pl.BlockSpec(m.shape, lambda b, h, t: (0, 0)),
        ],
        out_specs=pl.BlockSpec((rows, width), lambda b, h, t: (b * nt + t, h)),
        scratch_shapes=[pltpu.VMEM((hps, HGRN_HEAD_DIM, HGRN_HEAD_DIM), F32)],
        compiler_params=_cparams(("parallel", "parallel", "arbitrary")),
        name="hgrn2_recurrence",
    )(proj, proj, proj, proj, lb.reshape(1, d).astype(F32), gn.reshape(1, HGRN_HEAD_DIM).astype(F32), m)


def _compress_kernel(x_ref, w1c_ref, pos_ref, w1_ref, w2_ref, gain_ref, o_ref, *, apply_norm):
    hid = w2_ref.shape[0]
    nc = x_ref.shape[0]
    uv = _dot(x_ref[...], w1c_ref[...])
    posb = _dot(pos_ref[...], w1_ref[...])[0:1, :]
    pre = uv[:, :hid] + pltpu.roll(uv[:, hid:], nc - 1, 0) + posb
    out = _dot(jax.nn.gelu(pre).astype(BF16), w2_ref[...])
    if apply_norm:
        ms = jnp.mean(out * out, axis=-1, keepdims=True)
        out = out * lax.rsqrt(ms + NORM_EPS) * gain_ref[...]
    o_ref[...] = out


def _compress(x, w1, pos, w2, gain, *, apply_norm, name):
    b, g, nc, half = x.shape
    hid = w1.shape[1]
    dh = w2.shape[1]
    w1c = jnp.concatenate([w1[:half], w1[half:]], axis=1).astype(BF16)
    posr = jnp.broadcast_to(pos.reshape(1, -1), (SUBLANES, pos.size)).astype(BF16)
    return pl.pallas_call(
        functools.partial(_compress_kernel, apply_norm=apply_norm),
        out_shape=jax.ShapeDtypeStruct((b, g, nc, dh), F32),
        grid=(b, g),
        in_specs=[
            pl.BlockSpec((None, None, nc, half), lambda i, j: (i, j, 0, 0)),
            pl.BlockSpec((half, 2 * hid), lambda i, j: (0, 0)),
            pl.BlockSpec((SUBLANES, 2 * half), lambda i, j: (0, 0)),
            pl.BlockSpec((2 * half, hid), lambda i, j: (0, 0)),
            pl.BlockSpec((hid, dh), lambda i, j: (0, 0)),
            pl.BlockSpec((1, dh), lambda i, j: (0, 0)),
        ],
        out_specs=pl.BlockSpec((None, None, nc, dh), lambda i, j: (i, j, 0, 0)),
        compiler_params=_cparams(("parallel", "parallel")),
        name=name,
    )(x, w1c, posr, w1.astype(BF16), w2.astype(BF16), gain.reshape(1, dh).astype(F32))


def _pack_kv_kernel(k_ref, v_ref, gain_ref, kk_ref, v1_ref):
    kp = k_ref[...]
    vp = v_ref[...]
    lane = lax.broadcasted_iota(jnp.int32, kp.shape, 1)
    low = lane < NSA_HEAD_DIM
    sq = kp * kp
    ss_lo = jnp.sum(jnp.where(low, sq, 0.0), axis=-1, keepdims=True)
    ss_hi = jnp.sum(jnp.where(low, 0.0, sq), axis=-1, keepdims=True)
    ms = jnp.where(low, ss_lo, ss_hi) * (1.0 / NSA_HEAD_DIM)
    kn = kp * lax.rsqrt(ms + NORM_EPS) * gain_ref[...]
    kr = pltpu.roll(kn, NSA_HEAD_DIM, 1)
    vr = pltpu.roll(vp, NSA_HEAD_DIM, 1)
    kk_ref[0] = jnp.where(low, kn, kr).astype(kk_ref.dtype)
    kk_ref[1] = jnp.where(low, kr, kn).astype(kk_ref.dtype)
    v1_ref[0] = jnp.where(low, vp, 1.0).astype(v1_ref.dtype)
    v1_ref[1] = jnp.where(low, vr, 1.0).astype(v1_ref.dtype)


def _pack_kv(kvproj, gain, branch, batch, seq, *, tm=512, name="pack_kv"):
    tm = min(tm, seq)
    nt = seq // tm
    pairs = NSA_KV_HEADS // 2
    kcol = branch * 2 * pairs
    vcol = kcol + pairs
    gain2 = jnp.tile(gain.reshape(1, NSA_HEAD_DIM), (1, 2)).astype(F32)
    out = jax.ShapeDtypeStruct((batch, NSA_KV_HEADS, seq, LANES), BF16)
    ospec = pl.BlockSpec((None, 2, tm, LANES), lambda b, t, p: (b, p, t, 0))
    return pl.pallas_call(
        _pack_kv_kernel,
        out_shape=(out, out),
        grid=(batch, nt, pairs),
        in_specs=[
            pl.BlockSpec((tm, LANES), lambda b, t, p: (b * nt + t, kcol + p)),
            pl.BlockSpec((tm, LANES), lambda b, t, p: (b * nt + t, vcol + p)),
            pl.BlockSpec((1, LANES), lambda b, t, p: (0, 0)),
        ],
        out_specs=(ospec, ospec),
        compiler_params=_cparams(("parallel", "parallel", "parallel")),
        name=name,
    )(kvproj, kvproj, gain2)


def _rel_bucket_np(dist):
    max_exact = REL_BUCKETS // 2
    d = np.maximum(dist, 0)
    large = max_exact + (np.log(np.maximum(d, 1).astype(np.float32) / max_exact)
                         / math.log(REL_MAX_DIST / max_exact) * (REL_BUCKETS - max_exact)).astype(np.int32)
    large = np.minimum(large, REL_BUCKETS - 1)
    return np.where(d < max_exact, d, large).astype(np.int32)


def _bias_table(rel_bias, dist, valid):
    onehot = jax.nn.one_hot(_rel_bucket_np(dist), REL_BUCKETS, dtype=F32)
    vals = jnp.einsum("qkb,bh->hqk", onehot, rel_bias.astype(F32), precision=lax.Precision.HIGHEST)
    vals = jnp.where(jnp.asarray(valid)[None], vals, NEG_INF)
    return vals.reshape(NSA_KV_HEADS, NSA_HPG * dist.shape[0], dist.shape[1])


def _head_masks():
    lane = lax.broadcasted_iota(jnp.int32, (Q_BLOCK, LANES), 1)
    return lane < NSA_HEAD_DIM


def _stack_heads(q_ref, low):
    parts = []
    for pair in range(NSA_HPG // 2):
        qp = q_ref[:, pair * LANES:(pair + 1) * LANES].astype(F32)
        parts.append(jnp.where(low, qp, 0.0))
        parts.append(jnp.where(low, 0.0, qp))
    return jnp.concatenate(parts, axis=0).astype(BF16)


def _unstack_heads(x, low):
    parts = []
    for pair in range(NSA_HPG // 2):
        a = x[(2 * pair) * Q_BLOCK:(2 * pair + 1) * Q_BLOCK]
        b = x[(2 * pair + 1) * Q_BLOCK:(2 * pair + 2) * Q_BLOCK]
        parts.append(jnp.where(low, a, pltpu.roll(b, NSA_HEAD_DIM, 1)))
    return jnp.concatenate(parts, axis=1)


def _normalize_pv(pv):
    den = pltpu.roll(pv, NSA_HEAD_DIM, 1)
    return pv * jnp.where(den > 0.0, 1.0 / den, 0.0)


def _nsa_cmp_kernel(q_ref, kk_ref, v1_ref, bias_ref, ov_ref, o_ref, imp_ref, *, nc):
    nsub = q_ref.shape[0] // Q_BLOCK
    low = _head_masks()
    jcol = lax.broadcasted_iota(jnp.int32, (Q_BLOCK, nc), 1)
    subs = range(nsub)
    qbs = [pl.program_id(2) * nsub + u for u in subs]
    starts = [pl.multiple_of(qb * CMP_LOOKBACK, SUBLANES) for qb in qbs]
    s4 = [_dot_nt(_stack_heads(q_ref.at[u * Q_BLOCK:(u + 1) * Q_BLOCK], low),
                  kk_ref[pl.ds(starts[u], nc), :].astype(BF16)) for u in subs]
    probs, psums = [], []
    for u in subs:
        exists = jcol >= (nc - CMP_LOOKBACK - CMP_LOOKBACK * qbs[u])
        psum = jnp.zeros((Q_BLOCK, nc), F32)
        ps = []
        for hp in range(NSA_HPG):
            rows = slice(hp * Q_BLOCK, (hp + 1) * Q_BLOCK)
            s = jnp.where(exists, s4[u][rows] + bias_ref[rows, :], NEG_INF)
            mx = jnp.max(s, axis=-1, keepdims=True)
            p = jnp.exp(s - mx)
            den = jnp.sum(p, axis=-1, keepdims=True)
            p = p * jnp.where(mx > 0.5 * NEG_INF, 1.0 / den, 0.0)
            psum = psum + p
            ps.append(p.astype(BF16))
        probs.append(jnp.concatenate(ps, axis=0))
        psums.append(psum)
    for u in subs:
        pv = _dot(probs[u], v1_ref[pl.ds(starts[u], nc), :].astype(BF16))
        o_ref[u * Q_BLOCK:(u + 1) * Q_BLOCK, :] = _unstack_heads(pv, low).astype(o_ref.dtype)
    for u in subs:
        hi, lo = _split_bf16(psums[u])
        imp_ref[:, u * Q_BLOCK:(u + 1) * Q_BLOCK] = _dot_nt(ov_ref[...], jnp.concatenate([hi, lo], axis=1))


def _nsa_cmp(q, kvc, vkc, bias_c, ov2, batch, seq):
    nc = seq // CMP_STRIDE
    ns = seq // SLC_BLOCK
    nqb = seq // Q_BLOCK
    rows = kvc.shape[2]
    gw = NSA_HPG * NSA_HEAD_DIM
    qrows = CMP_QBLOCKS_PER_STEP * Q_BLOCK
    assert nqb % CMP_QBLOCKS_PER_STEP == 0
    return pl.pallas_call(
        functools.partial(_nsa_cmp_kernel, nc=nc),
        out_shape=(jax.ShapeDtypeStruct((batch, seq, NSA_HEADS * NSA_HEAD_DIM), BF16),
                   jax.ShapeDtypeStruct((batch, NSA_KV_HEADS, ns, seq), F32)),
        grid=(batch, NSA_KV_HEADS, nqb // CMP_QBLOCKS_PER_STEP),
        in_specs=[
            pl.BlockSpec((None, qrows, gw), lambda b, g, i: (b, i, g)),
            pl.BlockSpec((None, None, rows, LANES), lambda b, g, i: (b, g, 0, 0)),
            pl.BlockSpec((None, None, rows, LANES), lambda b, g, i: (b, g, 0, 0)),
            pl.BlockSpec((None, NSA_HPG * Q_BLOCK, nc), lambda b, g, i: (g, 0, 0)),
            pl.BlockSpec((ns, 2 * nc), lambda b, g, i: (0, 0)),
        ],
        out_specs=(pl.BlockSpec((None, qrows, gw), lambda b, g, i: (b, i, g)),
                   pl.BlockSpec((None, None, ns, qrows), lambda b, g, i: (b, g, 0, i))),
        compiler_params=_cparams(("parallel", "parallel", "parallel")),
        name="nsa_compressed",
    )(q, kvc, vkc, bias_c, ov2)


def _topk_kernel(imp_ref, sel_ref, *, ns, ntop):
    toks = imp_ref.shape[1]
    t = pl.program_id(1) * toks + lax.broadcasted_iota(jnp.int32, (ns, toks), 1)
    qb = t >> int(math.log2(Q_BLOCK))
    js = lax.broadcasted_iota(jnp.int32, (ns, toks), 0)
    js_first = ns - 2 - 2 * qb
    js_cur = ns - 2 + ((t & (Q_BLOCK - 1)) >> int(math.log2(SLC_BLOCK)))
    causal = (js >= js_first) & (js <= js_cur)
    forced = (js == js_first) | (js == js_cur) | (js == js_cur - 1)
    n_forced = 1 + (js_cur - 1 >= js_first).astype(jnp.int32) + (js_first < js_cur - 1).astype(jnp.int32)
    n_pick = jnp.minimum(ntop, js_cur - js_first + 1) - n_forced
    ninf = -jnp.inf
    score = jnp.where(causal & jnp.logical_not(forced), imp_ref[...], ninf)
    sel = jnp.where(causal & forced, 1.0, 0.0)
    jsf = js.astype(F32)
    for it in range(ntop - 1):
        mx = jnp.max(score, axis=0, keepdims=True)
        first = jnp.min(jnp.where(score == mx, jsf, float(ns)), axis=0, keepdims=True)
        hit = (jsf == first) & (it < n_pick)
        sel = jnp.where(hit, 1.0, sel)
        score = jnp.where(hit, ninf, score)
    sel_ref[...] = sel.T.astype(sel_ref.dtype)


def _topk_select(imp_t, *, toks=512):
    b, g, ns, t = imp_t.shape
    toks = min(toks, t)
    assert t % toks == 0
    nt = t // toks
    sel = pl.pallas_call(
        functools.partial(_topk_kernel, ns=ns, ntop=min(SLC_TOP, ns)),
        out_shape=jax.ShapeDtypeStruct((b * g * t, ns), BF16),
        grid=(b * g, nt),
        in_specs=[pl.BlockSpec((None, ns, toks), lambda i, j: (i, 0, j))],
        out_specs=pl.BlockSpec((toks, ns), lambda i, j: (i * nt + j, 0)),
        compiler_params=_cparams(("parallel", "parallel")),
        name="nsa_topk",
    )(imp_t.reshape(b * g, ns, t))
    return sel.reshape(b, g, t, ns)


def _nsa_slc_win_kernel(q_ref, sel_ref, kks_ref, v1s_ref, kkw_ref, v1w_ref, bias_ref, bwin_ref,
                        eexp_ref, gates_ref, egate_ref, ocmp_ref, o_ref, m_ref, acc_ref, p_ref, alpha_ref,
                        *, ns, kt_near, nkt):
    qb = pl.program_id(2)
    low = _head_masks()
    kw = SLC_KEY_TILE
    kt0 = (ns - 2 - 2 * qb) // SLC_BLOCKS_PER_TILE

    m_ref[...] = jnp.full_like(m_ref, NEG_INF)
    acc_ref[...] = jnp.zeros_like(acc_ref)
    p_ref[...] = jnp.zeros_like(p_ref)
    alpha_ref[...] = jnp.ones_like(alpha_ref)
    q4 = _stack_heads(q_ref, low)
    selb = sel_ref[...]

    def row_start(kt):
        return pl.multiple_of(jnp.maximum(Q_BLOCK * qb + kw * kt, 0), Q_BLOCK)

    def accumulate(kt):
        pv = _dot(p_ref[...], v1s_ref[pl.ds(row_start(kt), kw), :])
        acc_ref[...] = alpha_ref[...] * acc_ref[...] + pv

    def body(kt):
        s4 = _dot_nt(q4, kks_ref[pl.ds(row_start(kt), kw), :])
        mb = (_dot(selb, eexp_ref[kt]) - 1.0) * (-NEG_INF)
        accumulate(kt - 1)
        bt = jnp.maximum(kt - kt_near + 1, 0)
        for hp in range(NSA_HPG):
            rows = slice(hp * Q_BLOCK, (hp + 1) * Q_BLOCK)
            s = s4[rows] + bias_ref[bt, rows, :] + mb
            m_old = m_ref[rows, :]
            m_new = jnp.maximum(m_old, jnp.max(s, axis=-1, keepdims=True))
            alpha_ref[rows, :] = jnp.exp(m_old - m_new)
            p_ref[rows, :] = jnp.exp(s - m_new[:, 0:1]).astype(BF16)
            m_ref[rows, :] = m_new

    kt_even = (kt0 // 2) * 2

    def body2(j, carry):
        body(kt_even + 2 * j)
        body(kt_even + 2 * j + 1)
        return carry

    lax.fori_loop(0, (nkt - kt_even) // 2, body2, 0)
    accumulate(nkt - 1)
    o_slc = _unstack_heads(_normalize_pv(acc_ref[...]), low)

    wk = WINDOW + Q_BLOCK
    ws = pl.multiple_of(Q_BLOCK * qb, Q_BLOCK)
    sw = _dot_nt(q4, kkw_ref[pl.ds(ws, wk), :])
    wcol = lax.broadcasted_iota(jnp.int32, (Q_BLOCK, wk), 1)
    in_seq = wcol >= (WINDOW - Q_BLOCK * qb)
    ps = []
    for hp in range(NSA_HPG):
        rows = slice(hp * Q_BLOCK, (hp + 1) * Q_BLOCK)
        s = jnp.where(in_seq, sw[rows] + bwin_ref[rows, :], NEG_INF)
        ps.append(jnp.exp(s - jnp.max(s, axis=-1, keepdims=True)).astype(BF16))
    pvw = _dot(jnp.concatenate(ps, axis=0), v1w_ref[pl.ds(ws, wk), :])
    o_win = _unstack_heads(_normalize_pv(pvw), low)

    gh, gl = _split_bf16(gates_ref[...])
    g2 = jnp.concatenate([gh, gl], axis=1)
    o = (_dot(g2, egate_ref[0]) * ocmp_ref[...].astype(F32)
         + _dot(g2, egate_ref[1]) * o_slc
         + _dot(g2, egate_ref[2]) * o_win)
    o_ref[...] = o.astype(o_ref.dtype)


def _nsa_slc_win(q, sel, kvs, vks, kvw, vkw, bias_s, bwin, eexp, gates, egate, ocmp, batch, seq):
    ns = seq // SLC_BLOCK
    nqb = seq // Q_BLOCK
    nkt = (ns * SLC_BLOCK) // SLC_KEY_TILE
    kt_near = nkt - (bias_s.shape[1] - 1)
    gw = NSA_HPG * NSA_HEAD_DIM
    srows = kvs.shape[2]
    wrows = kvw.shape[2]
    slab = lambda r: pl.BlockSpec((None, None, r, LANES), lambda b, g, i: (b, g, 0, 0))
    return pl.pallas_call(
        functools.partial(_nsa_slc_win_kernel, ns=ns, kt_near=kt_near, nkt=nkt),
        out_shape=jax.ShapeDtypeStruct((batch, seq, NSA_HEADS * NSA_HEAD_DIM), BF16),
        grid=(batch, NSA_KV_HEADS, nqb),
        in_specs=[
            pl.BlockSpec((None, Q_BLOCK, gw), lambda b, g, i: (b, i, g)),
            pl.BlockSpec((None, None, Q_BLOCK, ns), lambda b, g, i: (b, g, i, 0)),
            slab(srows), slab(srows), slab(wrows), slab(wrows),
            pl.BlockSpec((None,) + bias_s.shape[1:], lambda b, g, i: (g, 0, 0, 0)),
            pl.BlockSpec((None,) + bwin.shape[1:], lambda b, g, i: (g, 0, 0)),
            pl.BlockSpec(eexp.shape, lambda b, g, i: (0, 0, 0)),
            pl.BlockSpec((None, Q_BLOCK, LANES), lambda b, g, i: (b, i, 0)),
            pl.BlockSpec((N_BRANCH, 2 * LANES, gw), lambda b, g, i: (0, 0, g)),
            pl.BlockSpec((None, Q_BLOCK, gw), lambda b, g, i: (b, i, g)),
        ],
        out_specs=pl.BlockSpec((None, Q_BLOCK, gw), lambda b, g, i: (b, i, g)),
        scratch_shapes=[pltpu.VMEM((NSA_HPG * Q_BLOCK, LANES), F32),
                        pltpu.VMEM((NSA_HPG * Q_BLOCK, LANES), F32),
                        pltpu.VMEM((NSA_HPG * Q_BLOCK, SLC_KEY_TILE), BF16),
                        pltpu.VMEM((NSA_HPG * Q_BLOCK, LANES), F32)],
        compiler_params=_cparams(("parallel", "parallel", "arbitrary")),
        name="nsa_selected_window",
    )(q, sel, kvs, vks, kvw, vkw, bias_s, bwin, eexp, gates, egate, ocmp)


def _nsa_tables(rel_bias, seq):
    nc = seq // CMP_STRIDE
    ns = seq // SLC_BLOCK
    i = np.arange(Q_BLOCK)[:, None]
    j = np.arange(nc)[None, :]
    dist_c = i - (CMP_BLOCK - 1) - CMP_STRIDE * (j - (nc - CMP_LOOKBACK))
    bias_c = _bias_table(rel_bias, dist_c, dist_c >= 0)
    cs = np.arange(nc)[:, None] * CMP_STRIDE
    ss = np.arange(ns)[None, :] * SLC_BLOCK
    ov = np.clip(np.minimum(cs + CMP_BLOCK, ss + SLC_BLOCK) - np.maximum(cs, ss), 0, None) / CMP_BLOCK
    ov2 = jnp.asarray(np.concatenate([ov, ov], axis=0).T, BF16)
    nkeys = ns * SLC_BLOCK
    nkt = nkeys // SLC_KEY_TILE
    pad_rows = (ns - 2) * SLC_BLOCK
    kt_near = max(0, pad_rows - REL_MAX_DIST) // SLC_KEY_TILE
    kr = np.arange(kt_near * SLC_KEY_TILE, nkeys)[None, :]
    dist_s = i + pad_rows - kr
    near = _bias_table(rel_bias, dist_s, dist_s >= 0)
    near = near.reshape(NSA_KV_HEADS, NSA_HPG * Q_BLOCK, nkt - kt_near, SLC_KEY_TILE).transpose(0, 2, 1, 3)
    far = jnp.broadcast_to(rel_bias.astype(F32)[REL_BUCKETS - 1].reshape(NSA_KV_HEADS, 1, NSA_HPG, 1, 1),
                           (NSA_KV_HEADS, 1, NSA_HPG, Q_BLOCK, SLC_KEY_TILE))
    bias_s = jnp.concatenate([far.reshape(NSA_KV_HEADS, 1, NSA_HPG * Q_BLOCK, SLC_KEY_TILE), near], axis=1)
    dist_w = i + WINDOW - np.arange(WINDOW + Q_BLOCK)[None, :]
    bwin = _bias_table(rel_bias, dist_w, (dist_w >= 0) & (dist_w < WINDOW))
    blk = np.arange(ns)[None, :, None]
    key = np.arange(SLC_KEY_TILE)[None, None, :]
    ktile = np.arange(nkt)[:, None, None]
    eexp = jnp.asarray(blk == ktile * SLC_BLOCKS_PER_TILE + key // SLC_BLOCK, BF16)
    col = np.arange(LANES)[:, None]
    head = (np.arange(NSA_HEADS * NSA_HEAD_DIM) // NSA_HEAD_DIM)[None, :]
    eg = np.stack([col == head * N_BRANCH + br for br in range(N_BRANCH)])
    egate = jnp.asarray(np.concatenate([eg, eg], axis=1), BF16)
    return bias_c, ov2, bias_s, bwin, eexp, egate


def _router_kernel(x_ref, g_ref, w_ref, o_ref):
    x = x_ref[...]
    ms = jnp.mean(x * x, axis=-1, keepdims=True)
    xn = x * lax.rsqrt(ms + NORM_EPS) * g_ref[...]
    logits = jnp.dot(xn, w_ref[...], preferred_element_type=F32, precision=lax.Precision.HIGHEST)
    lane = lax.broadcasted_iota(jnp.int32, logits.shape, 1).astype(F32)
    ninf = -jnp.inf
    s = jnp.where(lane < N_EXPERTS, logits, ninf)
    m1 = jnp.max(s, axis=-1, keepdims=True)
    i1 = jnp.min(jnp.where(s == m1, lane, float(LANES)), axis=-1, keepdims=True)
    s2 = jnp.where(lane == i1, ninf, s)
    m2 = jnp.max(s2, axis=-1, keepdims=True)
    i2 = jnp.min(jnp.where(s2 == m2, lane, float(LANES)), axis=-1, keepdims=True)
    e2 = jnp.exp(m2 - m1)
    w1 = 1.0 / (1.0 + e2)
    w2 = e2 * w1
    o_ref[...] = (jnp.where(lane == 0.0, i1, 0.0) + jnp.where(lane == 1.0, i2, 0.0)
                  + jnp.where(lane == 2.0, w1, 0.0) + jnp.where(lane == 3.0, w2, 0.0))


def _router(x, g, w_router, *, tm=512):
    n, d = x.shape
    tm = min(tm, n)
    wpad = jnp.zeros((d, LANES), F32).at[:, :N_EXPERTS].set(w_router.astype(F32))
    return pl.pallas_call(
        _router_kernel,
        out_shape=jax.ShapeDtypeStruct((n, LANES), F32),
        grid=(n // tm,),
        in_specs=[
            pl.BlockSpec((tm, d), lambda i: (i, 0)),
            pl.BlockSpec((1, d), lambda i: (0, 0)),
            pl.BlockSpec((d, LANES), lambda i: (0, 0)),
        ],
        out_specs=pl.BlockSpec((tm, LANES), lambda i: (i, 0)),
        compiler_params=_cparams(("parallel",)),
        name="moe_router",
    )(x, g.reshape(1, d).astype(F32), wpad)


def _moe_dispatch(route, tm):
    n = route.shape[0]
    pairs = 2 * n
    e = route[:, 0:2].astype(jnp.int32).reshape(pairs)
    onehot = (e[:, None] == jnp.arange(N_EXPERTS, dtype=jnp.int32)[None, :]).astype(jnp.int32)
    csum = jnp.cumsum(onehot, axis=0)
    rank = jnp.sum(csum * onehot, axis=1) - 1
    counts = csum[-1]
    cpad = ((counts + tm - 1) // tm) * tm
    gend = jnp.cumsum(cpad)
    dst = jnp.sum(onehot * (gend - cpad)[None, :], axis=1) + rank
    rows = pairs + N_EXPERTS * tm
    ntiles = rows // tm
    nvalid = gend[-1] // tm
    tile = jnp.arange(ntiles, dtype=jnp.int32)
    te = jnp.sum((tile[:, None] * tm >= gend[None, :]).astype(jnp.int32), axis=1)
    te = jnp.where(tile < nvalid, te, te[nvalid - 1])
    src = jnp.zeros((rows,), jnp.int32).at[dst].set(jnp.arange(pairs, dtype=jnp.int32) // 2)
    return src, dst, te.astype(jnp.int32), nvalid.reshape(1).astype(jnp.int32)


def _row_copy(src_hbm, row, dst_ref, j, sem):
    return pltpu.make_async_copy(src_hbm.at[pl.ds(row, 1), :], dst_ref.at[pl.ds(j, 1), :], sem)


def _gather_rows_kernel(idx_ref, src_hbm, o_ref, sem):
    rows = o_ref.shape[0]

    def issue(j, c):
        _row_copy(src_hbm, idx_ref[j], o_ref, j, sem).start()
        return c

    lax.fori_loop(0, rows, issue, 0, unroll=8)
    pltpu.make_async_copy(src_hbm.at[pl.ds(0, rows), :], o_ref, sem).wait()


def _gather_rows(x, idx, *, rows=1024):
    n, d = x.shape
    total = idx.shape[0]
    assert total % rows == 0
    return pl.pallas_call(
        _gather_rows_kernel,
        out_shape=jax.ShapeDtypeStruct((total, d), x.dtype),
        grid=(total // rows,),
        in_specs=[
            pl.BlockSpec((rows,), lambda i: (i,), memory_space=pltpu.SMEM),
            pl.BlockSpec(memory_space=pl.ANY),
        ],
        out_specs=pl.BlockSpec((rows, d), lambda i: (i, 0)),
        scratch_shapes=[pltpu.SemaphoreType.DMA],
        compiler_params=_cparams(("arbitrary",)),
        name="moe_gather",
    )(idx, x)


def _combine_kernel(i0_ref, i1_ref, y_hbm, h_ref, route_ref, o_ref, buf_ref, sem):
    rows = h_ref.shape[0]

    def issue(j, c):
        _row_copy(y_hbm, i0_ref[j], buf_ref.at[0], j, sem).start()
        _row_copy(y_hbm, i1_ref[j], buf_ref.at[1], j, sem).start()
        return c

    lax.fori_loop(0, rows, issue, 0, unroll=8)
    for k in range(2):
        pltpu.make_async_copy(y_hbm.at[pl.ds(0, rows), :], buf_ref.at[k], sem).wait()
    route = route_ref[...]
    o_ref[...] = h_ref[...] + route[:, 2:3] * buf_ref[0] + route[:, 3:4] * buf_ref[1]


def _combine(h, y, dst, route, *, rows=512):
    n, d = h.shape
    assert n % rows == 0
    dst2 = dst.reshape(n, 2)
    return pl.pallas_call(
        _combine_kernel,
        out_shape=jax.ShapeDtypeStruct((n, d), F32),
        grid=(n // rows,),
        in_specs=[
            pl.BlockSpec((rows,), lambda i: (i,), memory_space=pltpu.SMEM),
            pl.BlockSpec((rows,), lambda i: (i,), memory_space=pltpu.SMEM),
            pl.BlockSpec(memory_space=pl.ANY),
            pl.BlockSpec((rows, d), lambda i: (i, 0)),
            pl.BlockSpec((rows, LANES), lambda i: (i, 0)),
        ],
        out_specs=pl.BlockSpec((rows, d), lambda i: (i, 0)),
        scratch_shapes=[pltpu.VMEM((2, rows, d), F32), pltpu.SemaphoreType.DMA],
        compiler_params=_cparams(("arbitrary",)),
        name="moe_combine",
    )(dst2[:, 0], dst2[:, 1], y, h, route)


def _ffn_grouped_kernel(te_ref, nv_ref, x_ref, g_ref, wg_ref, wu_ref, wd_ref, o_ref, xn_ref, acc_ref):
    i = pl.program_id(0)
    f = pl.program_id(1)
    used = i < nv_ref[0]

    @pl.when(jnp.logical_and(used, f == 0))
    def _():
        x = x_ref[...]
        ms = jnp.mean(x * x, axis=-1, keepdims=True)
        xn_ref[...] = (x * lax.rsqrt(ms + NORM_EPS) * g_ref[...]).astype(BF16)
        acc_ref[...] = jnp.zeros_like(acc_ref)

    @pl.when(used)
    def _():
        xn = xn_ref[...]
        gate = _dot(xn, wg_ref[...])
        up = _dot(xn, wu_ref[...])
        act = (gate * jax.nn.sigmoid(gate) * up).astype(BF16)
        acc_ref[...] += _dot(act, wd_ref[...])

    last = f == pl.num_programs(1) - 1

    @pl.when(jnp.logical_and(used, last))
    def _():
        o_ref[...] = acc_ref[...]

    @pl.when(jnp.logical_and(jnp.logical_not(used), last))
    def _():
        o_ref[...] = jnp.zeros_like(o_ref)


def _ffn_grouped(x, g, w_gu, w_down, te, nvalid, *, tm, tf=None):
    rows, d = x.shape
    ff = w_down.shape[1]
    if tf is None:
        tf = ff // 2 if (ff // 2) % LANES == 0 else ff
    assert rows % tm == 0 and ff % tf == 0 and tf % LANES == 0
    nf = ff // tf

    def fcol(i, f, te_ref, nv_ref):
        return jnp.where(i < nv_ref[0], f, nf - 1)

    grid_spec = pltpu.PrefetchScalarGridSpec(
        num_scalar_prefetch=2,
        grid=(rows // tm, nf),
        in_specs=[
            pl.BlockSpec((tm, d), lambda i, f, te_ref, nv_ref: (i, 0)),
            pl.BlockSpec((1, d), lambda i, f, te_ref, nv_ref: (0, 0)),
            pl.BlockSpec((None, d, tf), lambda i, f, te_ref, nv_ref: (te_ref[i], 0, fcol(i, f, te_ref, nv_ref))),
            pl.BlockSpec((None, d, tf),
                         lambda i, f, te_ref, nv_ref: (te_ref[i], 0, nf + fcol(i, f, te_ref, nv_ref))),
            pl.BlockSpec((None, tf, d), lambda i, f, te_ref, nv_ref: (te_ref[i], fcol(i, f, te_ref, nv_ref), 0)),
        ],
        out_specs=pl.BlockSpec((tm, d), lambda i, f, te_ref, nv_ref: (i, 0)),
        scratch_shapes=[pltpu.VMEM((tm, d), BF16), pltpu.VMEM((tm, d), F32)],
    )
    return pl.pallas_call(
        _ffn_grouped_kernel,
        out_shape=jax.ShapeDtypeStruct((rows, d), F32),
        grid_spec=grid_spec,
        compiler_params=_cparams(("arbitrary", "arbitrary")),
        name="expert_ffn_grouped",
    )(te, nvalid, x, g.reshape(1, d).astype(F32), w_gu, w_gu, w_down)


def _nsa_layer(h, batch, seq, norm_g, kv_norm_g, w_kv, k_norm_g, cmp_pos_k, cmp_w1_k, cmp_w2_k,
               cmp_pos_v, cmp_w1_v, cmp_w2_v, w_in, w_out, q_norm_g, rel_bias):
    n, d = h.shape
    g_heads, dh = NSA_KV_HEADS, NSA_HEAD_DIM
    nc = seq // CMP_STRIDE
    ns = seq // SLC_BLOCK
    hq = NSA_HEADS * dh

    kvproj = _norm_matmul(h, kv_norm_g, w_kv.astype(BF16), name="nsa_kv_proj")

    gw_kv = g_heads * dh

    def chunks(col0):
        t = kvproj[:, col0:col0 + gw_kv].astype(BF16).reshape(batch, seq, g_heads, dh)
        return t.transpose(0, 2, 1, 3).reshape(batch, g_heads, nc, CMP_STRIDE * dh)

    k_cmp = _compress(chunks(0), cmp_w1_k, cmp_pos_k, cmp_w2_k, k_norm_g[0],
                      apply_norm=True, name="nsa_compress_k")
    v_cmp = _compress(chunks(gw_kv), cmp_w1_v, cmp_pos_v, cmp_w2_v, k_norm_g[0],
                      apply_norm=False, name="nsa_compress_v")
    cpad = nc - CMP_LOOKBACK
    padc = lambda t: jnp.pad(t, ((0, 0), (0, 0), (cpad, 0), (0, 0)))
    kvc = padc(jnp.concatenate([k_cmp, k_cmp], axis=-1))
    vkc = padc(jnp.concatenate([v_cmp, jnp.ones_like(v_cmp)], axis=-1))

    kvs, vks = _pack_kv(kvproj, k_norm_g[1], 1, batch, seq, name="nsa_pack_selected")
    kvw, vkw = _pack_kv(kvproj, k_norm_g[2], 2, batch, seq, name="nsa_pack_window")
    spad = (ns - 2) * SLC_BLOCK
    pads = lambda t, r: jnp.pad(t, ((0, 0), (0, 0), (r, 0), (0, 0)))
    kvs, vks = pads(kvs, spad), pads(vks, spad)
    kvw, vkw = pads(kvw, WINDOW), pads(vkw, WINDOW)

    qgain = jnp.tile(q_norm_g.astype(F32), NSA_HEADS) * (dh ** -0.5)
    q = _norm_matmul(h, norm_g, w_in[:, :hq].astype(BF16), epilogue="headnorm", gain=qgain,
                     flag=jnp.ones((hq,), F32), out_dtype=BF16, name="nsa_q_proj")
    ngate = N_BRANCH * NSA_HEADS
    wg = jnp.zeros((d, LANES), BF16).at[:, :ngate].set(w_in[:, hq:].astype(BF16))
    gates = _norm_matmul(h, norm_g, wg, epilogue="sigmoid", name="nsa_gate_proj")

    bias_c, ov2, bias_s, bwin, eexp, egate = _nsa_tables(rel_bias, seq)
    q3 = q.reshape(batch, seq, hq)
    o_cmp, imp_t = _nsa_cmp(q3, kvc, vkc, bias_c, ov2, batch, seq)
    sel = _topk_select(imp_t)
    o = _nsa_slc_win(q3, sel, kvs, vks, kvw, vkw, bias_s, bwin, eexp,
                     gates.reshape(batch, seq, LANES), egate, o_cmp, batch, seq)
    return _matmul_res(o.reshape(n, hq), w_out.astype(BF16), h, name="nsa_out_proj")


def kernel(x, norm_mix_g, norm_ffn_g, a_w_in, a_w_out, a_onorm_g, lb_param, kv_norm_g, w_kv, k_norm_g,
           cmp_pos_k, cmp_w1_k, cmp_w2_k, cmp_pos_v, cmp_w1_v, cmp_w2_v, b_w_in, b_w_out, b_qnorm_g,
           rel_bias, ffn_w_gu, ffn_w_down, moe_router, moe_w_gu, moe_w_down):
    batch, seq, d = x.shape
    n = batch * seq
    h = x.reshape(n, d).astype(F32)

    lower = jnp.cumsum(jax.nn.softmax(lb_param.astype(F32), axis=0), axis=0)[0]
    proj = _norm_matmul(h, norm_mix_g[0], a_w_in[0].astype(BF16), name="hgrn_in_proj")
    og = _hgrn_mixer(proj, lower, a_onorm_g[0], batch, seq)
    h = _matmul_res(og, a_w_out[0].astype(BF16), h, name="hgrn_out_proj")
    h = _ffn(h, norm_ffn_g[0], ffn_w_gu[0].astype(BF16), ffn_w_down[0].astype(BF16), name="dense_ffn")

    h = _nsa_layer(h, batch, seq, norm_mix_g[1], kv_norm_g, w_kv, k_norm_g, cmp_pos_k, cmp_w1_k, cmp_w2_k,
                   cmp_pos_v, cmp_w1_v, cmp_w2_v, b_w_in[0], b_w_out[0], b_qnorm_g[0], rel_bias)
    route = _router(h, norm_ffn_g[1], moe_router[0])
    src, dst, tile_expert, tiles_used = _moe_dispatch(route, MOE_ROW_TILE)
    ys = _ffn_grouped(_gather_rows(h, src), norm_ffn_g[1], moe_w_gu[0].astype(BF16), moe_w_down[0].astype(BF16),
                      tile_expert, tiles_used, tm=MOE_ROW_TILE)
    out = _combine(h, ys, dst, route)
    return out.reshape(batch, seq, d).astype(x.dtype)
```

```python
import functools
import math

import jax
import jax.numpy as jnp
import numpy as np
from jax import lax
from jax.experimental import pallas as pl
from jax.experimental.pallas import tpu as pltpu

F32 = jnp.float32
BF16 = jnp.bfloat16

NORM_EPS = 1e-6
NEG_INF = -1e30
FORCE_BONUS = 1e4
HGRN_HEAD_DIM = 128
NSA_HEADS = 16
NSA_KV_HEADS = 4
NSA_HPG = NSA_HEADS // NSA_KV_HEADS
NSA_HEAD_DIM = 64
N_BRANCH = 3
CMP_BLOCK = 32
CMP_STRIDE = 16
SLC_BLOCK = 64
SLC_TOP = 16
WINDOW = 512
Q_BLOCK = 128
REL_BUCKETS = 32
REL_MAX_DIST = 2048
N_EXPERTS = 8

LANES = 128
SUBLANES = 8
VMEM_LIMIT_BYTES = 56 * 1024 * 1024

HGRN_CHUNK = 128
HGRN_ROWS_PER_STEP = 512
HGRN_HEADS_PER_STEP = 4
SLC_KEY_TILE = 512
SLC_BLOCKS_PER_TILE = SLC_KEY_TILE // SLC_BLOCK
CMP_LOOKBACK = Q_BLOCK // CMP_STRIDE
CMP_QBLOCKS_PER_STEP = 2
STABLE_LOGIT_SPREAD = 60.0
MOE_ROW_TILE = 512


def _cparams(sem):
    return pltpu.CompilerParams(dimension_semantics=sem, vmem_limit_bytes=VMEM_LIMIT_BYTES)


def _dot(a, b):
    return jnp.dot(a, b, preferred_element_type=F32)


def _dot_nt(a, b):
    return lax.dot_general(a, b, (((1,), (1,)), ((), ())), preferred_element_type=F32)


def _split_bf16(x):
    hi = x.astype(BF16)
    lo = (x - hi.astype(F32)).astype(BF16)
    return hi, lo


def _norm_matmul_kernel(x_ref, g_ref, w_ref, *rest, epilogue, tn):
    if epilogue == "headnorm":
        gain_ref, flag_ref, bd_ref, o_ref = rest
    else:
        (o_ref,) = rest
    x = x_ref[...]
    ms = jnp.mean(x * x, axis=-1, keepdims=True)
    xn = (x * lax.rsqrt(ms + NORM_EPS) * g_ref[...]).astype(BF16)
    for c in range(o_ref.shape[1] // tn):
        cols = slice(c * tn, (c + 1) * tn)
        acc = _dot(xn, w_ref[:, cols])
        if epilogue == "headnorm":
            ss = _dot((acc * acc).astype(BF16), bd_ref[...])
            normed = acc * lax.rsqrt(ss * (1.0 / NSA_HEAD_DIM) + NORM_EPS) * gain_ref[:, cols]
            acc = jnp.where(flag_ref[:, cols] > 0.0, normed, acc)
        elif epilogue == "sigmoid":
            acc = jax.nn.sigmoid(acc)
        o_ref[:, cols] = acc.astype(o_ref.dtype)


def _norm_matmul(x, g, w, *, epilogue="none", gain=None, flag=None, out_dtype=F32, tm=512, tn=512,
                 name="norm_matmul"):
    n, d = x.shape
    m = w.shape[1]
    tm = min(tm, n)
    tn = min(tn, m)
    assert n % tm == 0 and m % tn == 0, (n, tm, m, tn)
    const = lambda i: (0, 0)
    in_specs = [pl.BlockSpec((tm, d), lambda i: (i, 0)), pl.BlockSpec((1, d), const), pl.BlockSpec((d, m), const)]
    args = [x, g.reshape(1, d).astype(F32), w]
    if epilogue == "headnorm":
        assert tn % NSA_HEAD_DIM == 0
        grp = np.arange(tn) // NSA_HEAD_DIM
        bd = jnp.asarray(grp[:, None] == grp[None, :], BF16)
        in_specs += [pl.BlockSpec((1, m), const), pl.BlockSpec((1, m), const), pl.BlockSpec((tn, tn), const)]
        args += [gain.reshape(1, m).astype(F32), flag.reshape(1, m).astype(F32), bd]
    return pl.pallas_call(
        functools.partial(_norm_matmul_kernel, epilogue=epilogue, tn=tn),
        out_shape=jax.ShapeDtypeStruct((n, m), out_dtype),
        grid=(n // tm,),
        in_specs=in_specs,
        out_specs=pl.BlockSpec((tm, m), lambda i: (i, 0)),
        compiler_params=_cparams(("parallel",)),
        name=name,
    )(*args)


def _matmul_res_kernel(a_ref, w_ref, r_ref, o_ref, *, tn):
    a = a_ref[...]
    for c in range(o_ref.shape[1] // tn):
        cols = slice(c * tn, (c + 1) * tn)
        o_ref[:, cols] = r_ref[:, cols] + _dot(a, w_ref[:, cols])


def _matmul_res(a, w, res, *, tm=512, tn=512, name="matmul_res"):
    n, k = a.shape
    m = w.shape[1]
    tm = min(tm, n)
    tn = min(tn, m)
    assert n % tm == 0 and m % tn == 0
    return pl.pallas_call(
        functools.partial(_matmul_res_kernel, tn=tn),
        out_shape=jax.ShapeDtypeStruct((n, m), F32),
        grid=(n // tm,),
        in_specs=[
            pl.BlockSpec((tm, k), lambda i: (i, 0)),
            pl.BlockSpec((k, m), lambda i: (0, 0)),
            pl.BlockSpec((tm, m), lambda i: (i, 0)),
        ],
        out_specs=pl.BlockSpec((tm, m), lambda i: (i, 0)),
        compiler_params=_cparams(("parallel",)),
        name=name,
    )(a, w, res)


def _ffn_kernel(x_ref, g_ref, wg_ref, wu_ref, wd_ref, o_ref, xn_ref, acc_ref):
    f = pl.program_id(1)

    @pl.when(f == 0)
    def _():
        x = x_ref[...]
        ms = jnp.mean(x * x, axis=-1, keepdims=True)
        xn_ref[...] = (x * lax.rsqrt(ms + NORM_EPS) * g_ref[...]).astype(BF16)
        acc_ref[...] = jnp.zeros_like(acc_ref)

    xn = xn_ref[...]
    gate = _dot(xn, wg_ref[...])
    up = _dot(xn, wu_ref[...])
    act = (gate * jax.nn.sigmoid(gate) * up).astype(BF16)
    acc_ref[...] += _dot(act, wd_ref[...])

    @pl.when(f == pl.num_programs(1) - 1)
    def _():
        o_ref[...] = x_ref[...] + acc_ref[...]


def _ffn(x, g, w_gu, w_down, *, tm=512, tf=None, name="ffn"):
    n, d = x.shape
    ff = w_down.shape[0]
    if tf is None:
        tf = ff // 2 if (ff // 2) % LANES == 0 else ff
    tm = min(tm, n)
    assert n % tm == 0 and ff % tf == 0 and tf % LANES == 0
    nf = ff // tf
    return pl.pallas_call(
        _ffn_kernel,
        out_shape=jax.ShapeDtypeStruct((n, d), F32),
        grid=(n // tm, nf),
        in_specs=[
            pl.BlockSpec((tm, d), lambda i, f: (i, 0)),
            pl.BlockSpec((1, d), lambda i, f: (0, 0)),
            pl.BlockSpec((d, tf), lambda i, f: (0, f)),
            pl.BlockSpec((d, tf), lambda i, f: (0, nf + f)),
            pl.BlockSpec((tf, d), lambda i, f: (f, 0)),
        ],
        out_specs=pl.BlockSpec((tm, d), lambda i, f: (i, 0)),
        scratch_shapes=[pltpu.VMEM((tm, d), BF16), pltpu.VMEM((tm, d), F32)],
        compiler_params=_cparams(("parallel", "arbitrary")),
        name=name,
    )(x, g.reshape(1, d).astype(F32), w_gu, w_gu, w_down)


def _hgrn_decay_matrix(c):
    levels = int(math.log2(c))
    out = np.zeros(((levels + 2) * c, c), np.float32)
    for l in range(levels):
        m = c >> (l + 1)
        for r in range(c):
            mid = (r // (2 * m)) * 2 * m + m - 1
            if r % (2 * m) >= m:
                out[l * c + r, mid + 1:r + 1] = 1.0
            else:
                out[l * c + r, r + 1:mid + 1] = 1.0
    for r in range(c):
        out[levels * c + r, :r + 1] = 1.0
        out[(levels + 1) * c + r, r + 1:] = 1.0
    return out


def _hgrn_kernel(q_ref, f_ref, v_ref, g_ref, lb_ref, gn_ref, m_ref, o_ref, st_ref, *, chunk, nchunk):
    c = chunk
    dh = HGRN_HEAD_DIM
    levels = int(math.log2(c))

    @pl.when(pl.program_id(2) == 0)
    def _():
        st_ref[...] = jnp.zeros_like(st_ref)

    gn = gn_ref[...]
    row = lax.broadcasted_iota(jnp.int32, (c, c), 0)
    col = lax.broadcasted_iota(jnp.int32, (c, c), 1)
    rowv = lax.broadcasted_iota(jnp.int32, (c, dh), 0)
    nheads = st_ref.shape[0]

    def body(ci, carry):
        sl = pl.ds(pl.multiple_of(ci * c, c), c)
        lb = lb_ref[...]
        fg = lb + (1.0 - lb) * jax.nn.sigmoid(f_ref[sl, :])
        kall = 1.0 - fg
        hi, lo = _split_bf16(jnp.log(fg))
        dall = _dot(m_ref[...], jnp.concatenate([hi, lo], axis=0))
        hs = range(nheads)
        cols = [slice(hh * dh, (hh + 1) * dh) for hh in hs]
        q = [q_ref[sl, cols[hh]] for hh in hs]
        k = [kall[:, cols[hh]] for hh in hs]
        v = [v_ref[sl, cols[hh]] for hh in hs]

        a = [jnp.where(row == col, _dot_nt(q[hh].astype(BF16), k[hh].astype(BF16)), 0.0) for hh in hs]
        for l in range(levels):
            sh = levels - 1 - l
            upper = ((rowv >> sh) & 1) == 1
            same = (row >> (sh + 1)) == (col >> (sh + 1))
            for hh in hs:
                e = jnp.exp(dall[l * c:(l + 1) * c, cols[hh]])
                qe = jnp.where(upper, q[hh] * e, 0.0).astype(BF16)
                ke = jnp.where(upper, 0.0, k[hh] * e).astype(BF16)
                a[hh] = a[hh] + jnp.where(same, _dot_nt(qe, ke), 0.0)
        b = [dall[levels * c:(levels + 1) * c, cols[hh]] for hh in hs]
        st = [st_ref[hh] for hh in hs]
        o = [_dot(a[hh].astype(BF16), v[hh].astype(BF16))
             + _dot_nt((q[hh] * jnp.exp(b[hh])).astype(BF16), st[hh].astype(BF16)) for hh in hs]
        for hh in hs:
            kr = (k[hh] * jnp.exp(dall[(levels + 1) * c:, cols[hh]])).astype(BF16)
            st_ref[hh] = st[hh] * jnp.exp(b[hh][c - 1:c, :]) + _dot(v[hh].T.astype(BF16), kr)
        for hh in hs:
            ms = jnp.mean(o[hh] * o[hh], axis=-1, keepdims=True)
            on = o[hh] * lax.rsqrt(ms + NORM_EPS) * gn
            gg = g_ref[sl, cols[hh]]
            o_ref[sl, cols[hh]] = (on * (gg * jax.nn.sigmoid(gg))).astype(o_ref.dtype)
        return carry

    lax.fori_loop(0, nchunk, body, 0)


def _hgrn_mixer(proj, lb, gn, batch, seq):
    n, d4 = proj.shape
    d = d4 // 4
    heads = d // HGRN_HEAD_DIM
    rows = min(HGRN_ROWS_PER_STEP, seq)
    chunk = min(HGRN_CHUNK, rows)
    assert seq % rows == 0 and rows % chunk == 0
    nt = seq // rows
    m1 = _hgrn_decay_matrix(chunk)
    m = jnp.asarray(np.concatenate([m1, m1], axis=1), BF16)

    hps = HGRN_HEADS_PER_STEP
    assert heads % hps == 0
    groups = heads // hps
    width = hps * HGRN_HEAD_DIM

    def col_spec(s):
        return pl.BlockSpec((rows, width), lambda b, h, t, s=s: (b * nt + t, s * groups + h))

    return pl.pallas_call(
        functools.partial(_hgrn_kernel, chunk=chunk, nchunk=rows // chunk),
        out_shape=jax.ShapeDtypeStruct((n, d), BF16),
        grid=(batch, groups, nt),
        in_specs=[
            col_spec(0), col_spec(1), col_spec(2), col_spec(3),
            pl.BlockSpec((1, width), lambda b, h, t: (0, h)),
            pl.BlockSpec((1, HGRN_HEAD_DIM), lambda b, h, t: (0, 0)),
            pl.BlockSpec(m.shape, lambda b, h, t: (0, 0)),
        ],
        out_specs=pl.BlockSpec((rows, width), lambda b, h, t: (b * nt + t, h)),
        scratch_shapes=[pltpu.VMEM((hps, HGRN_HEAD_DIM, HGRN_HEAD_DIM), F32)],
        compiler_params=_cparams(("parallel", "parallel", "arbitrary")),
        name="hgrn2_recurrence",
    )(proj, proj, proj, proj, lb.reshape(1, d).astype(F32), gn.reshape(1, HGRN_HEAD_DIM).astype(F32), m)


def _compress_kernel(x_ref, w1c_ref, pos_ref, w1_ref, w2_ref, gain_ref, o_ref, *, apply_norm):
    hid = w2_ref.shape[0]
    nc = x_ref.shape[0]
    uv = _dot(x_ref[...], w1c_ref[...])
    posb = _dot(pos_ref[...], w1_ref[...])[0:1, :]
    pre = uv[:, :hid] + pltpu.roll(uv[:, hid:], nc - 1, 0) + posb
    out = _dot(jax.nn.gelu(pre).astype(BF16), w2_ref[...])
    if apply_norm:
        ms = jnp.mean(out * out, axis=-1, keepdims=True)
        out = out * lax.rsqrt(ms + NORM_EPS) * gain_ref[...]
    o_ref[...] = out


def _compress(x, w1, pos, w2, gain, *, apply_norm, name):
    b, g, nc, half = x.shape
    hid = w1.shape[1]
    dh = w2.shape[1]
    w1c = jnp.concatenate([w1[:half], w1[half:]], axis=1).astype(BF16)
    posr = jnp.broadcast_to(pos.reshape(1, -1), (SUBLANES, pos.size)).astype(BF16)
    return pl.pallas_call(
        functools.partial(_compress_kernel, apply_norm=apply_norm),
        out_shape=jax.ShapeDtypeStruct((b, g, nc, dh), F32),
        grid=(b, g),
        in_specs=[
            pl.BlockSpec((None, None, nc, half), lambda i, j: (i, j, 0, 0)),
            pl.BlockSpec((half, 2 * hid), lambda i, j: (0, 0)),
            pl.BlockSpec((SUBLANES, 2 * half), lambda i, j: (0, 0)),
            pl.BlockSpec((2 * half, hid), lambda i, j: (0, 0)),
            pl.BlockSpec((hid, dh), lambda i, j: (0, 0)),
            pl.BlockSpec((1, dh), lambda i, j: (0, 0)),
        ],
        out_specs=pl.BlockSpec((None, None, nc, dh), lambda i, j: (i, j, 0, 0)),
        compiler_params=_cparams(("parallel", "parallel")),
        name=name,
    )(x, w1c, posr, w1.astype(BF16), w2.astype(BF16), gain.reshape(1, dh).astype(F32))


def _pack_kv_kernel(k_ref, v_ref, gain_ref, kk_ref, v1_ref):
    kp = k_ref[...]
    vp = v_ref[...]
    lane = lax.broadcasted_iota(jnp.int32, kp.shape, 1)
    low = lane < NSA_HEAD_DIM
    sq = kp * kp
    ss_lo = jnp.sum(jnp.where(low, sq, 0.0), axis=-1, keepdims=True)
    ss_hi = jnp.sum(jnp.where(low, 0.0, sq), axis=-1, keepdims=True)
    ms = jnp.where(low, ss_lo, ss_hi) * (1.0 / NSA_HEAD_DIM)
    kn = kp * lax.rsqrt(ms + NORM_EPS) * gain_ref[...]
    kr = pltpu.roll(kn, NSA_HEAD_DIM, 1)
    vr = pltpu.roll(vp, NSA_HEAD_DIM, 1)
    kk_ref[0] = jnp.where(low, kn, kr).astype(kk_ref.dtype)
    kk_ref[1] = jnp.where(low, kr, kn).astype(kk_ref.dtype)
    v1_ref[0] = jnp.where(low, vp, 1.0).astype(v1_ref.dtype)
    v1_ref[1] = jnp.where(low, vr, 1.0).astype(v1_ref.dtype)


def _pack_kv(kvproj, gain, branch, batch, seq, *, tm=512, name="pack_kv"):
    tm = min(tm, seq)
    nt = seq // tm
    pairs = NSA_KV_HEADS // 2
    kcol = branch * 2 * pairs
    vcol = kcol + pairs
    gain2 = jnp.tile(gain.reshape(1, NSA_HEAD_DIM), (1, 2)).astype(F32)
    out = jax.ShapeDtypeStruct((batch, NSA_KV_HEADS, seq, LANES), BF16)
    ospec = pl.BlockSpec((None, 2, tm, LANES), lambda b, t, p: (b, p, t, 0))
    return pl.pallas_call(
        _pack_kv_kernel,
        out_shape=(out, out),
        grid=(batch, nt, pairs),
        in_specs=[
            pl.BlockSpec((tm, LANES), lambda b, t, p: (b * nt + t, kcol + p)),
            pl.BlockSpec((tm, LANES), lambda b, t, p: (b * nt + t, vcol + p)),
            pl.BlockSpec((1, LANES), lambda b, t, p: (0, 0)),
        ],
        out_specs=(ospec, ospec),
        compiler_params=_cparams(("parallel", "parallel", "parallel")),
        name=name,
    )(kvproj, kvproj, gain2)


def _rel_bucket_np(dist):
    max_exact = REL_BUCKETS // 2
    d = np.maximum(dist, 0)
    large = max_exact + (np.log(np.maximum(d, 1).astype(np.float32) / max_exact)
                         / math.log(REL_MAX_DIST / max_exact) * (REL_BUCKETS - max_exact)).astype(np.int32)
    large = np.minimum(large, REL_BUCKETS - 1)
    return np.where(d < max_exact, d, large).astype(np.int32)


def _bias_table(rel_bias, dist, valid):
    onehot = jax.nn.one_hot(_rel_bucket_np(dist), REL_BUCKETS, dtype=F32)
    vals = jnp.einsum("qkb,bh->hqk", onehot, rel_bias.astype(F32), precision=lax.Precision.HIGHEST)
    vals = jnp.where(jnp.asarray(valid)[None], vals, NEG_INF)
    return vals.reshape(NSA_KV_HEADS, NSA_HPG * dist.shape[0], dist.shape[1])


def _head_masks():
    lane = lax.broadcasted_iota(jnp.int32, (Q_BLOCK, LANES), 1)
    return lane < NSA_HEAD_DIM


def _stack_heads(q_ref, low):
    parts = []
    for pair in range(NSA_HPG // 2):
        qp = q_ref[:, pair * LANES:(pair + 1) * LANES].astype(F32)
        parts.append(jnp.where(low, qp, 0.0))
        parts.append(jnp.where(low, 0.0, qp))
    return jnp.concatenate(parts, axis=0).astype(BF16)


def _unstack_heads(x, low):
    parts = []
    for pair in range(NSA_HPG // 2):
        a = x[(2 * pair) * Q_BLOCK:(2 * pair + 1) * Q_BLOCK]
        b = x[(2 * pair + 1) * Q_BLOCK:(2 * pair + 2) * Q_BLOCK]
        parts.append(jnp.where(low, a, pltpu.roll(b, NSA_HEAD_DIM, 1)))
    return jnp.concatenate(parts, axis=1)


def _normalize_pv(pv):
    den = pltpu.roll(pv, NSA_HEAD_DIM, 1)
    return pv * jnp.where(den > 0.0, 1.0 / den, 0.0)


def _nsa_cmp_kernel(q_ref, kk_ref, v1_ref, bias_ref, ov_ref, o_ref, imp_ref, *, nc):
    nsub = q_ref.shape[0] // Q_BLOCK
    low = _head_masks()
    jcol = lax.broadcasted_iota(jnp.int32, (Q_BLOCK, nc), 1)
    subs = range(nsub)
    qbs = [pl.program_id(2) * nsub + u for u in subs]
    starts = [pl.multiple_of(qb * CMP_LOOKBACK, SUBLANES) for qb in qbs]
    s4 = [_dot_nt(_stack_heads(q_ref.at[u * Q_BLOCK:(u + 1) * Q_BLOCK], low),
                  kk_ref[pl.ds(starts[u], nc), :].astype(BF16)) for u in subs]
    probs, psums = [], []
    for u in subs:
        exists = jcol >= (nc - CMP_LOOKBACK - CMP_LOOKBACK * qbs[u])
        psum = jnp.zeros((Q_BLOCK, nc), F32)
        ps = []
        for hp in range(NSA_HPG):
            rows = slice(hp * Q_BLOCK, (hp + 1) * Q_BLOCK)
            s = jnp.where(exists, s4[u][rows] + bias_ref[rows, :], NEG_INF)
            mx = jnp.max(s, axis=-1, keepdims=True)
            p = jnp.exp(s - mx)
            den = jnp.sum(p, axis=-1, keepdims=True)
            p = p * jnp.where(mx > 0.5 * NEG_INF, 1.0 / den, 0.0)
            psum = psum + p
            ps.append(p.astype(BF16))
        probs.append(jnp.concatenate(ps, axis=0))
        psums.append(psum)
    for u in subs:
        pv = _dot(probs[u], v1_ref[pl.ds(starts[u], nc), :].astype(BF16))
        o_ref[u * Q_BLOCK:(u + 1) * Q_BLOCK, :] = _unstack_heads(pv, low).astype(o_ref.dtype)
    for u in subs:
        hi, lo = _split_bf16(psums[u])
        imp_ref[:, u * Q_BLOCK:(u + 1) * Q_BLOCK] = _dot_nt(ov_ref[...], jnp.concatenate([hi, lo], axis=1))


def _nsa_cmp(q, kvc, vkc, bias_c, ov2, batch, seq):
    nc = seq // CMP_STRIDE
    ns = seq // SLC_BLOCK
    nqb = seq // Q_BLOCK
    rows = kvc.shape[2]
    gw = NSA_HPG * NSA_HEAD_DIM
    qrows = CMP_QBLOCKS_PER_STEP * Q_BLOCK
    assert nqb % CMP_QBLOCKS_PER_STEP == 0
    return pl.pallas_call(
        functools.partial(_nsa_cmp_kernel, nc=nc),
        out_shape=(jax.ShapeDtypeStruct((batch, seq, NSA_HEADS * NSA_HEAD_DIM), BF16),
                   jax.ShapeDtypeStruct((batch, NSA_KV_HEADS, ns, seq), F32)),
        grid=(batch, NSA_KV_HEADS, nqb // CMP_QBLOCKS_PER_STEP),
        in_specs=[
            pl.BlockSpec((None, qrows, gw), lambda b, g, i: (b, i, g)),
            pl.BlockSpec((None, None, rows, LANES), lambda b, g, i: (b, g, 0, 0)),
            pl.BlockSpec((None, None, rows, LANES), lambda b, g, i: (b, g, 0, 0)),
            pl.BlockSpec((None, NSA_HPG * Q_BLOCK, nc), lambda b, g, i: (g, 0, 0)),
            pl.BlockSpec((ns, 2 * nc), lambda b, g, i: (0, 0)),
        ],
        out_specs=(pl.BlockSpec((None, qrows, gw), lambda b, g, i: (b, i, g)),
                   pl.BlockSpec((None, None, ns, qrows), lambda b, g, i: (b, g, 0, i))),
        compiler_params=_cparams(("parallel", "parallel", "parallel")),
        name="nsa_compressed",
    )(q, kvc, vkc, bias_c, ov2)


def _topk_kernel(imp_ref, sel_ref, *, ns, ntop):
    toks = imp_ref.shape[1]
    t = pl.program_id(1) * toks + lax.broadcasted_iota(jnp.int32, (ns, toks), 1)
    qb = t >> int(math.log2(Q_BLOCK))
    js = lax.broadcasted_iota(jnp.int32, (ns, toks), 0)
    js_first = ns - 2 - 2 * qb
    js_cur = ns - 2 + ((t & (Q_BLOCK - 1)) >> int(math.log2(SLC_BLOCK)))
    causal = (js >= js_first) & (js <= js_cur)
    forced = (js == js_first) | (js == js_cur) | (js == js_cur - 1)
    n_forced = 1 + (js_cur - 1 >= js_first).astype(jnp.int32) + (js_first < js_cur - 1).astype(jnp.int32)
    n_pick = jnp.minimum(ntop, js_cur - js_first + 1) - n_forced
    ninf = -jnp.inf
    score = jnp.where(causal & jnp.logical_not(forced), imp_ref[...], ninf)
    sel = jnp.where(causal & forced, 1.0, 0.0)
    jsf = js.astype(F32)
    for it in range(ntop - 1):
        mx = jnp.max(score, axis=0, keepdims=True)
        first = jnp.min(jnp.where(score == mx, jsf, float(ns)), axis=0, keepdims=True)
        hit = (jsf == first) & (it < n_pick)
        sel = jnp.where(hit, 1.0, sel)
        score = jnp.where(hit, ninf, score)
    sel_ref[...] = sel.T.astype(sel_ref.dtype)


def _topk_select(imp_t, *, toks=512):
    b, g, ns, t = imp_t.shape
    toks = min(toks, t)
    assert t % toks == 0
    nt = t // toks
    sel = pl.pallas_call(
        functools.partial(_topk_kernel, ns=ns, ntop=min(SLC_TOP, ns)),
        out_shape=jax.ShapeDtypeStruct((b * g * t, ns), BF16),
        grid=(b * g, nt),
        in_specs=[pl.BlockSpec((None, ns, toks), lambda i, j: (i, 0, j))],
        out_specs=pl.BlockSpec((toks, ns), lambda i, j: (i * nt + j, 0)),
        compiler_params=_cparams(("parallel", "parallel")),
        name="nsa_topk",
    )(imp_t.reshape(b * g, ns, t))
    return sel.reshape(b, g, t, ns)


def _nsa_slc_win_kernel(safe_ref, q_ref, sel_ref, kks_ref, v1s_ref, kkw_ref, v1w_ref, bias_ref, bwin_ref,
                        eexp_ref, gates_ref, egate_ref, ocmp_ref, kmax_ref, bmax_ref, one0_ref,
                        o_ref, m_ref, acc_ref, p_ref, alpha_ref, s_ref, *, ns, kt_near, nkt):
    qb = pl.program_id(2)
    low = _head_masks()
    kw = SLC_KEY_TILE
    kt0 = (ns - 2 - 2 * qb) // SLC_BLOCKS_PER_TILE
    safe = safe_ref[pl.program_id(0), pl.program_id(1), qb] == 1

    acc_ref[...] = jnp.zeros_like(acc_ref)
    p_ref[...] = jnp.zeros_like(p_ref)
    q4 = _stack_heads(q_ref, low)
    selb = sel_ref[...]

    def row_start(kt):
        return pl.multiple_of(jnp.maximum(Q_BLOCK * qb + kw * kt, 0), Q_BLOCK)

    def block_mask(kt):
        return (_dot(selb, eexp_ref[kt]) - 1.0) * (-NEG_INF)

    def bias_index(kt):
        return jnp.maximum(kt - kt_near + 1, 0)

    @pl.when(safe)
    def _():
        qf = q4.astype(F32)
        qn = jnp.sqrt(jnp.sum(qf * qf, axis=-1, keepdims=True))
        bound = jnp.concatenate(
            [qn[hp * Q_BLOCK:(hp + 1) * Q_BLOCK] * kmax_ref[0:1, 0:1] + bmax_ref[hp][0:1, 0:1]
             for hp in range(NSA_HPG)], axis=0)
        lane = lax.broadcasted_iota(jnp.int32, (NSA_HPG * Q_BLOCK, LANES), 1)
        q4m = jnp.concatenate([q4, jnp.where(lane == 0, -bound, 0.0).astype(BF16)], axis=1)

        def scores(kt):
            return _dot_nt(q4m, jnp.concatenate([kks_ref[pl.ds(row_start(kt), kw), :], one0_ref[...]], axis=1))

        def probs(slot, kt):
            mb = block_mask(kt)
            bt = bias_index(kt)
            return jnp.concatenate(
                [jnp.exp(s_ref[slot, hp * Q_BLOCK:(hp + 1) * Q_BLOCK, :]
                         + bias_ref[bt, hp * Q_BLOCK:(hp + 1) * Q_BLOCK, :] + mb).astype(BF16)
                 for hp in range(NSA_HPG)], axis=0)

        kt_even = (kt0 // 2) * 2
        s_ref[0] = scores(kt_even)

        def body2(j, carry):
            kt = kt_even + 2 * j
            s_ref[1] = scores(kt + 1)
            pv_prev = _dot(p_ref[...], v1s_ref[pl.ds(row_start(kt - 1), kw), :])
            p_a = probs(0, kt)
            s_ref[0] = scores(jnp.minimum(kt + 2, nkt - 1))
            pv_a = _dot(p_a, v1s_ref[pl.ds(row_start(kt), kw), :])
            p_ref[...] = probs(1, kt + 1)
            acc_ref[...] += pv_prev + pv_a
            return carry

        lax.fori_loop(0, (nkt - kt_even) // 2, body2, 0)
        acc_ref[...] += _dot(p_ref[...], v1s_ref[pl.ds(row_start(nkt - 1), kw), :])

    @pl.when(jnp.logical_not(safe))
    def _():
        m_ref[...] = jnp.full_like(m_ref, NEG_INF)
        alpha_ref[...] = jnp.ones_like(alpha_ref)

        def accumulate(kt):
            pv = _dot(p_ref[...], v1s_ref[pl.ds(row_start(kt), kw), :])
            acc_ref[...] = alpha_ref[...] * acc_ref[...] + pv

        def body(kt):
            s4 = _dot_nt(q4, kks_ref[pl.ds(row_start(kt), kw), :])
            mb = block_mask(kt)
            accumulate(kt - 1)
            bt = bias_index(kt)
            for hp in range(NSA_HPG):
                rows = slice(hp * Q_BLOCK, (hp + 1) * Q_BLOCK)
                s = s4[rows] + bias_ref[bt, rows, :] + mb
                m_old = m_ref[rows, :]
                m_new = jnp.maximum(m_old, jnp.max(s, axis=-1, keepdims=True))
                alpha_ref[rows, :] = jnp.exp(m_old - m_new)
                p_ref[rows, :] = jnp.exp(s - m_new[:, 0:1]).astype(BF16)
                m_ref[rows, :] = m_new

        kt_even = (kt0 // 2) * 2

        def body2(j, carry):
            body(kt_even + 2 * j)
            body(kt_even + 2 * j + 1)
            return carry

        lax.fori_loop(0, (nkt - kt_even) // 2, body2, 0)
        accumulate(nkt - 1)

    o_slc = _unstack_heads(_normalize_pv(acc_ref[...]), low)

    wk = WINDOW + Q_BLOCK
    ws = pl.multiple_of(Q_BLOCK * qb, Q_BLOCK)
    sw = _dot_nt(q4, kkw_ref[pl.ds(ws, wk), :])
    wcol = lax.broadcasted_iota(jnp.int32, (Q_BLOCK, wk), 1)
    in_seq = wcol >= (WINDOW - Q_BLOCK * qb)
    ps = []
    for hp in range(NSA_HPG):
        rows = slice(hp * Q_BLOCK, (hp + 1) * Q_BLOCK)
        s = jnp.where(in_seq, sw[rows] + bwin_ref[rows, :], NEG_INF)
        ps.append(jnp.exp(s - jnp.max(s, axis=-1, keepdims=True)).astype(BF16))
    pvw = _dot(jnp.concatenate(ps, axis=0), v1w_ref[pl.ds(ws, wk), :])
    o_win = _unstack_heads(_normalize_pv(pvw), low)

    gh, gl = _split_bf16(gates_ref[...])
    g2 = jnp.concatenate([gh, gl], axis=1)
    o = (_dot(g2, egate_ref[0]) * ocmp_ref[...].astype(F32)
         + _dot(g2, egate_ref[1]) * o_slc
         + _dot(g2, egate_ref[2]) * o_win)
    o_ref[...] = o.astype(o_ref.dtype)


def _softmax_bounds(q, kk, rel_bias, batch, seq):
    nqb = seq // Q_BLOCK
    qf = q.astype(F32).reshape(batch, nqb, Q_BLOCK, NSA_KV_HEADS, NSA_HPG, NSA_HEAD_DIM)
    qmax = jnp.sqrt(jnp.max(jnp.sum(qf * qf, axis=-1), axis=(2, 4)))
    kf = kk.astype(F32)[..., :NSA_HEAD_DIM]
    kmax = jnp.sqrt(jnp.max(jnp.sum(kf * kf, axis=-1), axis=-1))
    tb = rel_bias.astype(F32).T.reshape(NSA_KV_HEADS, NSA_HPG, REL_BUCKETS)
    bmax = jnp.max(tb, axis=-1)
    spread = jnp.max(bmax - jnp.min(tb, axis=-1), axis=-1)
    safe = 2.0 * qmax.transpose(0, 2, 1) * kmax[:, :, None] + spread[None, :, None] < STABLE_LOGIT_SPREAD
    tile = lambda a: jnp.broadcast_to(a[..., None, None], a.shape + (SUBLANES, LANES))
    return safe.astype(jnp.int32), tile(kmax), tile(bmax)


def _nsa_slc_win(q, sel, kvs, vks, kvw, vkw, bias_s, bwin, eexp, gates, egate, ocmp, safe, kmax, bmax, batch, seq):
    ns = seq // SLC_BLOCK
    nqb = seq // Q_BLOCK
    nkt = (ns * SLC_BLOCK) // SLC_KEY_TILE
    kt_near = nkt - (bias_s.shape[1] - 1)
    gw = NSA_HPG * NSA_HEAD_DIM
    srows = kvs.shape[2]
    wrows = kvw.shape[2]
    slab = lambda r: pl.BlockSpec((None, None, r, LANES), lambda b, g, i, s: (b, g, 0, 0))
    one0 = jnp.asarray(np.arange(LANES)[None, :] == 0, BF16) * jnp.ones((SLC_KEY_TILE, 1), BF16)
    grid_spec = pltpu.PrefetchScalarGridSpec(
        num_scalar_prefetch=1,
        grid=(batch, NSA_KV_HEADS, nqb),
        in_specs=[
            pl.BlockSpec((None, Q_BLOCK, gw), lambda b, g, i, s: (b, i, g)),
            pl.BlockSpec((None, None, Q_BLOCK, ns), lambda b, g, i, s: (b, g, i, 0)),
            slab(srows), slab(srows), slab(wrows), slab(wrows),
            pl.BlockSpec((None,) + bias_s.shape[1:], lambda b, g, i, s: (g, 0, 0, 0)),
            pl.BlockSpec((None,) + bwin.shape[1:], lambda b, g, i, s: (g, 0, 0)),
            pl.BlockSpec(eexp.shape, lambda b, g, i, s: (0, 0, 0)),
            pl.BlockSpec((None, Q_BLOCK, LANES), lambda b, g, i, s: (b, i, 0)),
            pl.BlockSpec((N_BRANCH, 2 * LANES, gw), lambda b, g, i, s: (0, 0, g)),
            pl.BlockSpec((None, Q_BLOCK, gw), lambda b, g, i, s: (b, i, g)),
            pl.BlockSpec((None, None, SUBLANES, LANES), lambda b, g, i, s: (b, g, 0, 0)),
            pl.BlockSpec((None, NSA_HPG, SUBLANES, LANES), lambda b, g, i, s: (g, 0, 0, 0)),
            pl.BlockSpec((SLC_KEY_TILE, LANES), lambda b, g, i, s: (0, 0)),
        ],
        out_specs=pl.BlockSpec((None, Q_BLOCK, gw), lambda b, g, i, s: (b, i, g)),
        scratch_shapes=[pltpu.VMEM((NSA_HPG * Q_BLOCK, LANES), F32),
                        pltpu.VMEM((NSA_HPG * Q_BLOCK, LANES), F32),
                        pltpu.VMEM((NSA_HPG * Q_BLOCK, SLC_KEY_TILE), BF16),
                        pltpu.VMEM((NSA_HPG * Q_BLOCK, LANES), F32),
                        pltpu.VMEM((2, NSA_HPG * Q_BLOCK, SLC_KEY_TILE), F32)],
    )
    return pl.pallas_call(
        functools.partial(_nsa_slc_win_kernel, ns=ns, kt_near=kt_near, nkt=nkt),
        out_shape=jax.ShapeDtypeStruct((batch, seq, NSA_HEADS * NSA_HEAD_DIM), BF16),
        grid_spec=grid_spec,
        compiler_params=_cparams(("parallel", "parallel", "arbitrary")),
        name="nsa_selected_window",
    )(safe, q, sel, kvs, vks, kvw, vkw, bias_s, bwin, eexp, gates, egate, ocmp, kmax, bmax, one0)


def _nsa_tables(rel_bias, seq):
    nc = seq // CMP_STRIDE
    ns = seq // SLC_BLOCK
    i = np.arange(Q_BLOCK)[:, None]
    j = np.arange(nc)[None, :]
    dist_c = i - (CMP_BLOCK - 1) - CMP_STRIDE * (j - (nc - CMP_LOOKBACK))
    bias_c = _bias_table(rel_bias, dist_c, dist_c >= 0)
    cs = np.arange(nc)[:, None] * CMP_STRIDE
    ss = np.arange(ns)[None, :] * SLC_BLOCK
    ov = np.clip(np.minimum(cs + CMP_BLOCK, ss + SLC_BLOCK) - np.maximum(cs, ss), 0, None) / CMP_BLOCK
    ov2 = jnp.asarray(np.concatenate([ov, ov], axis=0).T, BF16)
    nkeys = ns * SLC_BLOCK
    nkt = nkeys // SLC_KEY_TILE
    pad_rows = (ns - 2) * SLC_BLOCK
    kt_near = max(0, pad_rows - REL_MAX_DIST) // SLC_KEY_TILE
    kr = np.arange(kt_near * SLC_KEY_TILE, nkeys)[None, :]
    dist_s = i + pad_rows - kr
    near = _bias_table(rel_bias, dist_s, dist_s >= 0)
    near = near.reshape(NSA_KV_HEADS, NSA_HPG * Q_BLOCK, nkt - kt_near, SLC_KEY_TILE).transpose(0, 2, 1, 3)
    far = jnp.broadcast_to(rel_bias.astype(F32)[REL_BUCKETS - 1].reshape(NSA_KV_HEADS, 1, NSA_HPG, 1, 1),
                           (NSA_KV_HEADS, 1, NSA_HPG, Q_BLOCK, SLC_KEY_TILE))
    bias_s = jnp.concatenate([far.reshape(NSA_KV_HEADS, 1, NSA_HPG * Q_BLOCK, SLC_KEY_TILE), near], axis=1)
    dist_w = i + WINDOW - np.arange(WINDOW + Q_BLOCK)[None, :]
    bwin = _bias_table(rel_bias, dist_w, (dist_w >= 0) & (dist_w < WINDOW))
    blk = np.arange(ns)[None, :, None]
    key = np.arange(SLC_KEY_TILE)[None, None, :]
    ktile = np.arange(nkt)[:, None, None]
    eexp = jnp.asarray(blk == ktile * SLC_BLOCKS_PER_TILE + key // SLC_BLOCK, BF16)
    col = np.arange(LANES)[:, None]
    head = (np.arange(NSA_HEADS * NSA_HEAD_DIM) // NSA_HEAD_DIM)[None, :]
    eg = np.stack([col == head * N_BRANCH + br for br in range(N_BRANCH)])
    egate = jnp.asarray(np.concatenate([eg, eg], axis=1), BF16)
    return bias_c, ov2, bias_s, bwin, eexp, egate


def _router_kernel(x_ref, g_ref, w_ref, o_ref):
    x = x_ref[...]
    ms = jnp.mean(x * x, axis=-1, keepdims=True)
    xn = x * lax.rsqrt(ms + NORM_EPS) * g_ref[...]
    logits = jnp.dot(xn, w_ref[...], preferred_element_type=F32, precision=lax.Precision.HIGHEST)
    lane = lax.broadcasted_iota(jnp.int32, logits.shape, 1).astype(F32)
    ninf = -jnp.inf
    s = jnp.where(lane < N_EXPERTS, logits, ninf)
    m1 = jnp.max(s, axis=-1, keepdims=True)
    i1 = jnp.min(jnp.where(s == m1, lane, float(LANES)), axis=-1, keepdims=True)
    s2 = jnp.where(lane == i1, ninf, s)
    m2 = jnp.max(s2, axis=-1, keepdims=True)
    i2 = jnp.min(jnp.where(s2 == m2, lane, float(LANES)), axis=-1, keepdims=True)
    e2 = jnp.exp(m2 - m1)
    w1 = 1.0 / (1.0 + e2)
    w2 = e2 * w1
    o_ref[...] = (jnp.where(lane == 0.0, i1, 0.0) + jnp.where(lane == 1.0, i2, 0.0)
                  + jnp.where(lane == 2.0, w1, 0.0) + jnp.where(lane == 3.0, w2, 0.0))


def _router(x, g, w_router, *, tm=512):
    n, d = x.shape
    tm = min(tm, n)
    wpad = jnp.zeros((d, LANES), F32).at[:, :N_EXPERTS].set(w_router.astype(F32))
    return pl.pallas_call(
        _router_kernel,
        out_shape=jax.ShapeDtypeStruct((n, LANES), F32),
        grid=(n // tm,),
        in_specs=[
            pl.BlockSpec((tm, d), lambda i: (i, 0)),
            pl.BlockSpec((1, d), lambda i: (0, 0)),
            pl.BlockSpec((d, LANES), lambda i: (0, 0)),
        ],
        out_specs=pl.BlockSpec((tm, LANES), lambda i: (i, 0)),
        compiler_params=_cparams(("parallel",)),
        name="moe_router",
    )(x, g.reshape(1, d).astype(F32), wpad)


def _moe_dispatch(route, tm):
    n = route.shape[0]
    pairs = 2 * n
    e = route[:, 0:2].astype(jnp.int32).reshape(pairs)
    onehot = (e[:, None] == jnp.arange(N_EXPERTS, dtype=jnp.int32)[None, :]).astype(jnp.int32)
    csum = jnp.cumsum(onehot, axis=0)
    rank = jnp.sum(csum * onehot, axis=1) - 1
    counts = csum[-1]
    cpad = ((counts + tm - 1) // tm) * tm
    gend = jnp.cumsum(cpad)
    dst = jnp.sum(onehot * (gend - cpad)[None, :], axis=1) + rank
    rows = pairs + N_EXPERTS * tm
    ntiles = rows // tm
    nvalid = gend[-1] // tm
    tile = jnp.arange(ntiles, dtype=jnp.int32)
    te = jnp.sum((tile[:, None] * tm >= gend[None, :]).astype(jnp.int32), axis=1)
    te = jnp.where(tile < nvalid, te, te[nvalid - 1])
    src = jnp.zeros((rows,), jnp.int32).at[dst].set(jnp.arange(pairs, dtype=jnp.int32) // 2)
    return src, dst, te.astype(jnp.int32), nvalid.reshape(1).astype(jnp.int32)


def _row_copy(src_hbm, row, dst_ref, j, sem):
    return pltpu.make_async_copy(src_hbm.at[pl.ds(row, 1), :], dst_ref.at[pl.ds(j, 1), :], sem)


def _gather_rows_kernel(idx_ref, src_hbm, o_ref, sem):
    rows = o_ref.shape[0]

    def issue(j, c):
        _row_copy(src_hbm, idx_ref[j], o_ref, j, sem).start()
        return c

    lax.fori_loop(0, rows, issue, 0, unroll=8)
    pltpu.make_async_copy(src_hbm.at[pl.ds(0, rows), :], o_ref, sem).wait()


def _gather_rows(x, idx, *, rows=1024):
    n, d = x.shape
    total = idx.shape[0]
    assert total % rows == 0
    return pl.pallas_call(
        _gather_rows_kernel,
        out_shape=jax.ShapeDtypeStruct((total, d), x.dtype),
        grid=(total // rows,),
        in_specs=[
            pl.BlockSpec((rows,), lambda i: (i,), memory_space=pltpu.SMEM),
            pl.BlockSpec(memory_space=pl.ANY),
        ],
        out_specs=pl.BlockSpec((rows, d), lambda i: (i, 0)),
        scratch_shapes=[pltpu.SemaphoreType.DMA],
        compiler_params=_cparams(("arbitrary",)),
        name="moe_gather",
    )(idx, x)


def _combine_kernel(i0_ref, i1_ref, y_hbm, h_ref, route_ref, o_ref, buf_ref, sem):
    rows = h_ref.shape[0]

    def issue(j, c):
        _row_copy(y_hbm, i0_ref[j], buf_ref.at[0], j, sem).start()
        _row_copy(y_hbm, i1_ref[j], buf_ref.at[1], j, sem).start()
        return c

    lax.fori_loop(0, rows, issue, 0, unroll=8)
    for k in range(2):
        pltpu.make_async_copy(y_hbm.at[pl.ds(0, rows), :], buf_ref.at[k], sem).wait()
    route = route_ref[...]
    o_ref[...] = h_ref[...] + route[:, 2:3] * buf_ref[0] + route[:, 3:4] * buf_ref[1]


def _combine(h, y, dst, route, *, rows=512):
    n, d = h.shape
    assert n % rows == 0
    dst2 = dst.reshape(n, 2)
    return pl.pallas_call(
        _combine_kernel,
        out_shape=jax.ShapeDtypeStruct((n, d), F32),
        grid=(n // rows,),
        in_specs=[
            pl.BlockSpec((rows,), lambda i: (i,), memory_space=pltpu.SMEM),
            pl.BlockSpec((rows,), lambda i: (i,), memory_space=pltpu.SMEM),
            pl.BlockSpec(memory_space=pl.ANY),
            pl.BlockSpec((rows, d), lambda i: (i, 0)),
            pl.BlockSpec((rows, LANES), lambda i: (i, 0)),
        ],
        out_specs=pl.BlockSpec((rows, d), lambda i: (i, 0)),
        scratch_shapes=[pltpu.VMEM((2, rows, d), F32), pltpu.SemaphoreType.DMA],
        compiler_params=_cparams(("arbitrary",)),
        name="moe_combine",
    )(dst2[:, 0], dst2[:, 1], y, h, route)


def _ffn_grouped_kernel(te_ref, nv_ref, x_ref, g_ref, wg_ref, wu_ref, wd_ref, o_ref, xn_ref, acc_ref):
    i = pl.program_id(0)
    f = pl.program_id(1)
    used = i < nv_ref[0]

    @pl.when(jnp.logical_and(used, f == 0))
    def _():
        x = x_ref[...]
        ms = jnp.mean(x * x, axis=-1, keepdims=True)
        xn_ref[...] = (x * lax.rsqrt(ms + NORM_EPS) * g_ref[...]).astype(BF16)
        acc_ref[...] = jnp.zeros_like(acc_ref)

    @pl.when(used)
    def _():
        xn = xn_ref[...]
        gate = _dot(xn, wg_ref[...])
        up = _dot(xn, wu_ref[...])
        act = (gate * jax.nn.sigmoid(gate) * up).astype(BF16)
        acc_ref[...] += _dot(act, wd_ref[...])

    last = f == pl.num_programs(1) - 1

    @pl.when(jnp.logical_and(used, last))
    def _():
        o_ref[...] = acc_ref[...]

    @pl.when(jnp.logical_and(jnp.logical_not(used), last))
    def _():
        o_ref[...] = jnp.zeros_like(o_ref)


def _ffn_grouped(x, g, w_gu, w_down, te, nvalid, *, tm, tf=None):
    rows, d = x.shape
    ff = w_down.shape[1]
    if tf is None:
        tf = ff // 2 if (ff // 2) % LANES == 0 else ff
    assert rows % tm == 0 and ff % tf == 0 and tf % LANES == 0
    nf = ff // tf

    def fcol(i, f, te_ref, nv_ref):
        return jnp.where(i < nv_ref[0], f, nf - 1)

    grid_spec = pltpu.PrefetchScalarGridSpec(
        num_scalar_prefetch=2,
        grid=(rows // tm, nf),
        in_specs=[
            pl.BlockSpec((tm, d), lambda i, f, te_ref, nv_ref: (i, 0)),
            pl.BlockSpec((1, d), lambda i, f, te_ref, nv_ref: (0, 0)),
            pl.BlockSpec((None, d, tf), lambda i, f, te_ref, nv_ref: (te_ref[i], 0, fcol(i, f, te_ref, nv_ref))),
            pl.BlockSpec((None, d, tf),
                         lambda i, f, te_ref, nv_ref: (te_ref[i], 0, nf + fcol(i, f, te_ref, nv_ref))),
            pl.BlockSpec((None, tf, d), lambda i, f, te_ref, nv_ref: (te_ref[i], fcol(i, f, te_ref, nv_ref), 0)),
        ],
        out_specs=pl.BlockSpec((tm, d), lambda i, f, te_ref, nv_ref: (i, 0)),
        scratch_shapes=[pltpu.VMEM((tm, d), BF16), pltpu.VMEM((tm, d), F32)],
    )
    return pl.pallas_call(
        _ffn_grouped_kernel,
        out_shape=jax.ShapeDtypeStruct((rows, d), F32),
        grid_spec=grid_spec,
        compiler_params=_cparams(("arbitrary", "arbitrary")),
        name="expert_ffn_grouped",
    )(te, nvalid, x, g.reshape(1, d).astype(F32), w_gu, w_gu, w_down)


def _nsa_layer(h, batch, seq, norm_g, kv_norm_g, w_kv, k_norm_g, cmp_pos_k, cmp_w1_k, cmp_w2_k,
               cmp_pos_v, cmp_w1_v, cmp_w2_v, w_in, w_out, q_norm_g, rel_bias):
    n, d = h.shape
    g_heads, dh = NSA_KV_HEADS, NSA_HEAD_DIM
    nc = seq // CMP_STRIDE
    ns = seq // SLC_BLOCK
    hq = NSA_HEADS * dh

    kvproj = _norm_matmul(h, kv_norm_g, w_kv.astype(BF16), name="nsa_kv_proj")

    gw_kv = g_heads * dh

    def chunks(col0):
        t = kvproj[:, col0:col0 + gw_kv].astype(BF16).reshape(batch, seq, g_heads, dh)
        return t.transpose(0, 2, 1, 3).reshape(batch, g_heads, nc, CMP_STRIDE * dh)

    k_cmp = _compress(chunks(0), cmp_w1_k, cmp_pos_k, cmp_w2_k, k_norm_g[0],
                      apply_norm=True, name="nsa_compress_k")
    v_cmp = _compress(chunks(gw_kv), cmp_w1_v, cmp_pos_v, cmp_w2_v, k_norm_g[0],
                      apply_norm=False, name="nsa_compress_v")
    cpad = nc - CMP_LOOKBACK
    padc = lambda t: jnp.pad(t, ((0, 0), (0, 0), (cpad, 0), (0, 0)))
    kvc = padc(jnp.concatenate([k_cmp, k_cmp], axis=-1))
    vkc = padc(jnp.concatenate([v_cmp, jnp.ones_like(v_cmp)], axis=-1))

    kvs, vks = _pack_kv(kvproj, k_norm_g[1], 1, batch, seq, name="nsa_pack_selected")
    kvw, vkw = _pack_kv(kvproj, k_norm_g[2], 2, batch, seq, name="nsa_pack_window")
    kk_selected = kvs
    spad = (ns - 2) * SLC_BLOCK
    pads = lambda t, r: jnp.pad(t, ((0, 0), (0, 0), (r, 0), (0, 0)))
    kvs, vks = pads(kvs, spad), pads(vks, spad)
    kvw, vkw = pads(kvw, WINDOW), pads(vkw, WINDOW)

    qgain = jnp.tile(q_norm_g.astype(F32), NSA_HEADS) * (dh ** -0.5)
    q = _norm_matmul(h, norm_g, w_in[:, :hq].astype(BF16), epilogue="headnorm", gain=qgain,
                     flag=jnp.ones((hq,), F32), out_dtype=BF16, name="nsa_q_proj")
    ngate = N_BRANCH * NSA_HEADS
    wg = jnp.zeros((d, LANES), BF16).at[:, :ngate].set(w_in[:, hq:].astype(BF16))
    gates = _norm_matmul(h, norm_g, wg, epilogue="sigmoid", name="nsa_gate_proj")

    bias_c, ov2, bias_s, bwin, eexp, egate = _nsa_tables(rel_bias, seq)
    q3 = q.reshape(batch, seq, hq)
    o_cmp, imp_t = _nsa_cmp(q3, kvc, vkc, bias_c, ov2, batch, seq)
    sel = _topk_select(imp_t)
    safe, kmax, bmax = _softmax_bounds(q3, kk_selected, rel_bias, batch, seq)
    o = _nsa_slc_win(q3, sel, kvs, vks, kvw, vkw, bias_s, bwin, eexp,
                     gates.reshape(batch, seq, LANES), egate, o_cmp, safe, kmax, bmax, batch, seq)
    return _matmul_res(o.reshape(n, hq), w_out.astype(BF16), h, name="nsa_out_proj")


def kernel(x, norm_mix_g, norm_ffn_g, a_w_in, a_w_out, a_onorm_g, lb_param, kv_norm_g, w_kv, k_norm_g,
           cmp_pos_k, cmp_w1_k, cmp_w2_k, cmp_pos_v, cmp_w1_v, cmp_w2_v, b_w_in, b_w_out, b_qnorm_g,
           rel_bias, ffn_w_gu, ffn_w_down, moe_router, moe_w_gu, moe_w_down):
    batch, seq, d = x.shape
    n = batch * seq
    h = x.reshape(n, d).astype(F32)

    lower = jnp.cumsum(jax.nn.softmax(lb_param.astype(F32), axis=0), axis=0)[0]
    proj = _norm_matmul(h, norm_mix_g[0], a_w_in[0].astype(BF16), name="hgrn_in_proj")
    og = _hgrn_mixer(proj, lower, a_onorm_g[0], batch, seq)
    h = _matmul_res(og, a_w_out[0].astype(BF16), h, name="hgrn_out_proj")
    h = _ffn(h, norm_ffn_g[0], ffn_w_gu[0].astype(BF16), ffn_w_down[0].astype(BF16), name="dense_ffn")

    h = _nsa_layer(h, batch, seq, norm_mix_g[1], kv_norm_g, w_kv, k_norm_g, cmp_pos_k, cmp_w1_k, cmp_w2_k,
                   cmp_pos_v, cmp_w1_v, cmp_w2_v, b_w_in[0], b_w_out[0], b_qnorm_g[0], rel_bias)
    route = _router(h, norm_ffn_g[1], moe_router[0])
    src, dst, tile_expert, tiles_used = _moe_dispatch(route, MOE_ROW_TILE)
    ys = _ffn_grouped(_gather_rows(h, src), norm_ffn_g[1], moe_w_gu[0].astype(BF16), moe_w_down[0].astype(BF16),
                      tile_expert, tiles_used, tm=MOE_ROW_TILE)
    out = _combine(h, ys, dst, route)
    return out.reshape(batch, seq, d).astype(x.dtype)
```

```python
import functools
import math

import jax
import jax.numpy as jnp
import numpy as np
from jax import lax
from jax.experimental import pallas as pl
from jax.experimental.pallas import tpu as pltpu

F32 = jnp.float32
BF16 = jnp.bfloat16

NORM_EPS = 1e-6
NEG_INF = -1e30
FORCE_BONUS = 1e4
HGRN_HEAD_DIM = 128
NSA_HEADS = 16
NSA_KV_HEADS = 4
NSA_HPG = NSA_HEADS // NSA_KV_HEADS
NSA_HEAD_DIM = 64
N_BRANCH = 3
CMP_BLOCK = 32
CMP_STRIDE = 16
SLC_BLOCK = 64
SLC_TOP = 16
WINDOW = 512
Q_BLOCK = 128
REL_BUCKETS = 32
REL_MAX_DIST = 2048
N_EXPERTS = 8

LANES = 128
SUBLANES = 8
VMEM_LIMIT_BYTES = 56 * 1024 * 1024

HGRN_CHUNK = 128
HGRN_ROWS_PER_STEP = 512
HGRN_HEADS_PER_STEP = 8
SLC_KEY_TILE = 512
SLC_BLOCKS_PER_TILE = SLC_KEY_TILE // SLC_BLOCK
CMP_LOOKBACK = Q_BLOCK // CMP_STRIDE
CMP_QBLOCKS_PER_STEP = 4
MOE_ROW_TILE = 512
DMA_ISSUE_UNROLL = 8


def _cparams(sem):
    return pltpu.CompilerParams(dimension_semantics=sem, vmem_limit_bytes=VMEM_LIMIT_BYTES)


def _dot(a, b):
    return jnp.dot(a, b, preferred_element_type=F32)


def _dot_nt(a, b):
    return lax.dot_general(a, b, (((1,), (1,)), ((), ())), preferred_element_type=F32)


def _split_bf16(x):
    hi = x.astype(BF16)
    lo = (x - hi.astype(F32)).astype(BF16)
    return hi, lo


def _norm_matmul_kernel(x_ref, g_ref, w_ref, *rest, epilogue, tn):
    if epilogue == "headnorm":
        gain_ref, flag_ref, bd_ref, o_ref = rest
    else:
        (o_ref,) = rest
    x = x_ref[...]
    ms = jnp.mean(x * x, axis=-1, keepdims=True)
    xn = (x * lax.rsqrt(ms + NORM_EPS) * g_ref[...]).astype(BF16)
    for c in range(o_ref.shape[1] // tn):
        cols = slice(c * tn, (c + 1) * tn)
        acc = _dot(xn, w_ref[:, cols])
        if epilogue == "headnorm":
            ss = _dot((acc * acc).astype(BF16), bd_ref[...])
            normed = acc * lax.rsqrt(ss * (1.0 / NSA_HEAD_DIM) + NORM_EPS) * gain_ref[:, cols]
            acc = jnp.where(flag_ref[:, cols] > 0.0, normed, acc)
        elif epilogue == "sigmoid":
            acc = jax.nn.sigmoid(acc)
        o_ref[:, cols] = acc.astype(o_ref.dtype)


def _norm_matmul(x, g, w, *, epilogue="none", gain=None, flag=None, out_dtype=F32, tm=512, tn=512,
                 name="norm_matmul"):
    n, d = x.shape
    m = w.shape[1]
    tm = min(tm, n)
    tn = min(tn, m)
    assert n % tm == 0 and m % tn == 0, (n, tm, m, tn)
    const = lambda i: (0, 0)
    in_specs = [pl.BlockSpec((tm, d), lambda i: (i, 0)), pl.BlockSpec((1, d), const), pl.BlockSpec((d, m), const)]
    args = [x, g.reshape(1, d).astype(F32), w]
    if epilogue == "headnorm":
        assert tn % NSA_HEAD_DIM == 0
        grp = np.arange(tn) // NSA_HEAD_DIM
        bd = jnp.asarray(grp[:, None] == grp[None, :], BF16)
        in_specs += [pl.BlockSpec((1, m), const), pl.BlockSpec((1, m), const), pl.BlockSpec((tn, tn), const)]
        args += [gain.reshape(1, m).astype(F32), flag.reshape(1, m).astype(F32), bd]
    return pl.pallas_call(
        functools.partial(_norm_matmul_kernel, epilogue=epilogue, tn=tn),
        out_shape=jax.ShapeDtypeStruct((n, m), out_dtype),
        grid=(n // tm,),
        in_specs=in_specs,
        out_specs=pl.BlockSpec((tm, m), lambda i: (i, 0)),
        compiler_params=_cparams(("parallel",)),
        name=name,
    )(*args)


def _matmul_res_kernel(a_ref, w_ref, r_ref, o_ref, *, tn):
    a = a_ref[...]
    for c in range(o_ref.shape[1] // tn):
        cols = slice(c * tn, (c + 1) * tn)
        o_ref[:, cols] = r_ref[:, cols] + _dot(a, w_ref[:, cols])


def _matmul_res(a, w, res, *, tm=512, tn=512, name="matmul_res"):
    n, k = a.shape
    m = w.shape[1]
    tm = min(tm, n)
    tn = min(tn, m)
    assert n % tm == 0 and m % tn == 0
    return pl.pallas_call(
        functools.partial(_matmul_res_kernel, tn=tn),
        out_shape=jax.ShapeDtypeStruct((n, m), F32),
        grid=(n // tm,),
        in_specs=[
            pl.BlockSpec((tm, k), lambda i: (i, 0)),
            pl.BlockSpec((k, m), lambda i: (0, 0)),
            pl.BlockSpec((tm, m), lambda i: (i, 0)),
        ],
        out_specs=pl.BlockSpec((tm, m), lambda i: (i, 0)),
        compiler_params=_cparams(("parallel",)),
        name=name,
    )(a, w, res)


def _ffn_kernel(x_ref, g_ref, wg_ref, wu_ref, wd_ref, o_ref, xn_ref, acc_ref):
    f = pl.program_id(1)

    @pl.when(f == 0)
    def _():
        x = x_ref[...]
        ms = jnp.mean(x * x, axis=-1, keepdims=True)
        xn_ref[...] = (x * lax.rsqrt(ms + NORM_EPS) * g_ref[...]).astype(BF16)
        acc_ref[...] = jnp.zeros_like(acc_ref)

    xn = xn_ref[...]
    gate = _dot(xn, wg_ref[...])
    up = _dot(xn, wu_ref[...])
    act = (gate * jax.nn.sigmoid(gate) * up).astype(BF16)
    acc_ref[...] += _dot(act, wd_ref[...])

    @pl.when(f == pl.num_programs(1) - 1)
    def _():
        o_ref[...] = x_ref[...] + acc_ref[...]


def _ffn(x, g, w_gu, w_down, *, tm=512, tf=None, name="ffn"):
    n, d = x.shape
    ff = w_down.shape[0]
    if tf is None:
        tf = ff // 2 if (ff // 2) % LANES == 0 else ff
    tm = min(tm, n)
    assert n % tm == 0 and ff % tf == 0 and tf % LANES == 0
    nf = ff // tf
    return pl.pallas_call(
        _ffn_kernel,
        out_shape=jax.ShapeDtypeStruct((n, d), F32),
        grid=(n // tm, nf),
        in_specs=[
            pl.BlockSpec((tm, d), lambda i, f: (i, 0)),
            pl.BlockSpec((1, d), lambda i, f: (0, 0)),
            pl.BlockSpec((d, tf), lambda i, f: (0, f)),
            pl.BlockSpec((d, tf), lambda i, f: (0, nf + f)),
            pl.BlockSpec((tf, d), lambda i, f: (f, 0)),
        ],
        out_specs=pl.BlockSpec((tm, d), lambda i, f: (i, 0)),
        scratch_shapes=[pltpu.VMEM((tm, d), BF16), pltpu.VMEM((tm, d), F32)],
        compiler_params=_cparams(("parallel", "arbitrary")),
        name=name,
    )(x, g.reshape(1, d).astype(F32), w_gu, w_gu, w_down)


def _hgrn_decay_matrix(c):
    levels = int(math.log2(c))
    out = np.zeros(((levels + 2) * c, c), np.float32)
    for l in range(levels):
        m = c >> (l + 1)
        for r in range(c):
            mid = (r // (2 * m)) * 2 * m + m - 1
            if r % (2 * m) >= m:
                out[l * c + r, mid + 1:r + 1] = 1.0
            else:
                out[l * c + r, r + 1:mid + 1] = 1.0
    for r in range(c):
        out[levels * c + r, :r + 1] = 1.0
        out[(levels + 1) * c + r, r + 1:] = 1.0
    return out


def _hgrn_kernel(q_ref, f_ref, v_ref, g_ref, lb_ref, gn_ref, m_ref, o_ref, st_ref, *, chunk, nchunk):
    c = chunk
    dh = HGRN_HEAD_DIM
    levels = int(math.log2(c))

    @pl.when(pl.program_id(2) == 0)
    def _():
        st_ref[...] = jnp.zeros_like(st_ref)

    gn = gn_ref[...]
    row = lax.broadcasted_iota(jnp.int32, (c, c), 0)
    col = lax.broadcasted_iota(jnp.int32, (c, c), 1)
    rowv = lax.broadcasted_iota(jnp.int32, (c, dh), 0)
    nheads = st_ref.shape[0]

    def body(ci, carry):
        sl = pl.ds(pl.multiple_of(ci * c, c), c)
        lb = lb_ref[...]
        fg = lb + (1.0 - lb) * jax.nn.sigmoid(f_ref[sl, :])
        kall = 1.0 - fg
        hi, lo = _split_bf16(jnp.log(fg))
        dall = _dot(m_ref[...], jnp.concatenate([hi, lo], axis=0))
        hs = range(nheads)
        cols = [slice(hh * dh, (hh + 1) * dh) for hh in hs]
        q = [q_ref[sl, cols[hh]] for hh in hs]
        k = [kall[:, cols[hh]] for hh in hs]
        v = [v_ref[sl, cols[hh]] for hh in hs]

        a = [jnp.where(row == col, _dot_nt(q[hh].astype(BF16), k[hh].astype(BF16)), 0.0) for hh in hs]
        for l in range(levels):
            sh = levels - 1 - l
            upper = ((rowv >> sh) & 1) == 1
            same = (row >> (sh + 1)) == (col >> (sh + 1))
            for hh in hs:
                e = jnp.exp(dall[l * c:(l + 1) * c, cols[hh]])
                qe = jnp.where(upper, q[hh] * e, 0.0).astype(BF16)
                ke = jnp.where(upper, 0.0, k[hh] * e).astype(BF16)
                a[hh] = a[hh] + jnp.where(same, _dot_nt(qe, ke), 0.0)
        b = [dall[levels * c:(levels + 1) * c, cols[hh]] for hh in hs]
        st = [st_ref[hh] for hh in hs]
        o = [_dot(a[hh].astype(BF16), v[hh].astype(BF16))
             + _dot_nt((q[hh] * jnp.exp(b[hh])).astype(BF16), st[hh].astype(BF16)) for hh in hs]
        for hh in hs:
            kr = (k[hh] * jnp.exp(dall[(levels + 1) * c:, cols[hh]])).astype(BF16)
            st_ref[hh] = st[hh] * jnp.exp(b[hh][c - 1:c, :]) + _dot(v[hh].T.astype(BF16), kr)
        for hh in hs:
            ms = jnp.mean(o[hh] * o[hh], axis=-1, keepdims=True)
            on = o[hh] * lax.rsqrt(ms + NORM_EPS) * gn
            gg = g_ref[sl, cols[hh]]
            o_ref[sl, cols[hh]] = (on * (gg * jax.nn.sigmoid(gg))).astype(o_ref.dtype)
        return carry

    lax.fori_loop(0, nchunk, body, 0)


def _hgrn_mixer(proj, lb, gn, batch, seq):
    n, d4 = proj.shape
    d = d4 // 4
    heads = d // HGRN_HEAD_DIM
    rows = min(HGRN_ROWS_PER_STEP, seq)
    chunk = min(HGRN_CHUNK, rows)
    assert seq % rows == 0 and rows % chunk == 0
    nt = seq // rows
    m1 = _hgrn_decay_matrix(chunk)
    m = jnp.asarray(np.concatenate([m1, m1], axis=1), BF16)

    hps = HGRN_HEADS_PER_STEP
    assert heads % hps == 0
    groups = heads // hps
    width = hps * HGRN_HEAD_DIM

    def col_spec(s):
        return pl.BlockSpec((rows, width), lambda b, h, t, s=s: (b * nt + t, s * groups + h))

    return pl.pallas_call(
        functools.partial(_hgrn_kernel, chunk=chunk, nchunk=rows // chunk),
        out_shape=jax.ShapeDtypeStruct((n, d), BF16),
        grid=(batch, groups, nt),
        in_specs=[
            col_spec(0), col_spec(1), col_spec(2), col_spec(3),
            pl.BlockSpec((1, width), lambda b, h, t: (0, h)),
            pl.BlockSpec((1, HGRN_HEAD_DIM), lambda b, h, t: (0, 0)),
            pl.BlockSpec(m.shape, lambda b, h, t: (0, 0)),
        ],
        out_specs=pl.BlockSpec((rows, width), lambda b, h, t: (b * nt + t, h)),
        scratch_shapes=[pltpu.VMEM((hps, HGRN_HEAD_DIM, HGRN_HEAD_DIM), F32)],
        compiler_params=_cparams(("parallel", "parallel", "arbitrary")),
        name="hgrn2_recurrence",
    )(proj, proj, proj, proj, lb.reshape(1, d).astype(F32), gn.reshape(1, HGRN_HEAD_DIM).astype(F32), m)


def _compress_kernel(x_ref, w1c_ref, pos_ref, w1_ref, w2_ref, gain_ref, o_ref, *, apply_norm):
    hid = w2_ref.shape[0]
    nc = x_ref.shape[0]
    uv = _dot(x_ref[...], w1c_ref[...])
    posb = _dot(pos_ref[...], w1_ref[...])[0:1, :]
    pre = uv[:, :hid] + pltpu.roll(uv[:, hid:], nc - 1, 0) + posb
    out = _dot(jax.nn.gelu(pre).astype(BF16), w2_ref[...])
    if apply_norm:
        ms = jnp.mean(out * out, axis=-1, keepdims=True)
        out = out * lax.rsqrt(ms + NORM_EPS) * gain_ref[...]
    o_ref[...] = out


def _compress(x, w1, pos, w2, gain, *, apply_norm, name):
    b, g, nc, half = x.shape
    hid = w1.shape[1]
    dh = w2.shape[1]
    w1c = jnp.concatenate([w1[:half], w1[half:]], axis=1).astype(BF16)
    posr = jnp.broadcast_to(pos.reshape(1, -1), (SUBLANES, pos.size)).astype(BF16)
    return pl.pallas_call(
        functools.partial(_compress_kernel, apply_norm=apply_norm),
        out_shape=jax.ShapeDtypeStruct((b, g, nc, dh), F32),
        grid=(b, g),
        in_specs=[
            pl.BlockSpec((None, None, nc, half), lambda i, j: (i, j, 0, 0)),
            pl.BlockSpec((half, 2 * hid), lambda i, j: (0, 0)),
            pl.BlockSpec((SUBLANES, 2 * half), lambda i, j: (0, 0)),
            pl.BlockSpec((2 * half, hid), lambda i, j: (0, 0)),
            pl.BlockSpec((hid, dh), lambda i, j: (0, 0)),
            pl.BlockSpec((1, dh), lambda i, j: (0, 0)),
        ],
        out_specs=pl.BlockSpec((None, None, nc, dh), lambda i, j: (i, j, 0, 0)),
        compiler_params=_cparams(("parallel", "parallel")),
        name=name,
    )(x, w1c, posr, w1.astype(BF16), w2.astype(BF16), gain.reshape(1, dh).astype(F32))


def _pack_kv_kernel(k_ref, v_ref, gain_ref, kk_ref, v1_ref):
    kp = k_ref[...]
    vp = v_ref[...]
    lane = lax.broadcasted_iota(jnp.int32, kp.shape, 1)
    low = lane < NSA_HEAD_DIM
    sq = kp * kp
    ss_lo = jnp.sum(jnp.where(low, sq, 0.0), axis=-1, keepdims=True)
    ss_hi = jnp.sum(jnp.where(low, 0.0, sq), axis=-1, keepdims=True)
    ms = jnp.where(low, ss_lo, ss_hi) * (1.0 / NSA_HEAD_DIM)
    kn = kp * lax.rsqrt(ms + NORM_EPS) * gain_ref[...]
    kr = pltpu.roll(kn, NSA_HEAD_DIM, 1)
    vr = pltpu.roll(vp, NSA_HEAD_DIM, 1)
    kk_ref[0] = jnp.where(low, kn, kr).astype(kk_ref.dtype)
    kk_ref[1] = jnp.where(low, kr, kn).astype(kk_ref.dtype)
    v1_ref[0] = jnp.where(low, vp, 1.0).astype(v1_ref.dtype)
    v1_ref[1] = jnp.where(low, vr, 1.0).astype(v1_ref.dtype)


def _pack_kv(kvproj, gain, branch, batch, seq, *, tm=512, name="pack_kv"):
    tm = min(tm, seq)
    nt = seq // tm
    pairs = NSA_KV_HEADS // 2
    kcol = branch * 2 * pairs
    vcol = kcol + pairs
    gain2 = jnp.tile(gain.reshape(1, NSA_HEAD_DIM), (1, 2)).astype(F32)
    out = jax.ShapeDtypeStruct((batch, NSA_KV_HEADS, seq, LANES), BF16)
    ospec = pl.BlockSpec((None, 2, tm, LANES), lambda b, t, p: (b, p, t, 0))
    return pl.pallas_call(
        _pack_kv_kernel,
        out_shape=(out, out),
        grid=(batch, nt, pairs),
        in_specs=[
            pl.BlockSpec((tm, LANES), lambda b, t, p: (b * nt + t, kcol + p)),
            pl.BlockSpec((tm, LANES), lambda b, t, p: (b * nt + t, vcol + p)),
            pl.BlockSpec((1, LANES), lambda b, t, p: (0, 0)),
        ],
        out_specs=(ospec, ospec),
        compiler_params=_cparams(("parallel", "parallel", "parallel")),
        name=name,
    )(kvproj, kvproj, gain2)


def _rel_bucket_np(dist):
    max_exact = REL_BUCKETS // 2
    d = np.maximum(dist, 0)
    large = max_exact + (np.log(np.maximum(d, 1).astype(np.float32) / max_exact)
                         / math.log(REL_MAX_DIST / max_exact) * (REL_BUCKETS - max_exact)).astype(np.int32)
    large = np.minimum(large, REL_BUCKETS - 1)
    return np.where(d < max_exact, d, large).astype(np.int32)


def _bias_table(rel_bias, dist, valid):
    onehot = jax.nn.one_hot(_rel_bucket_np(dist), REL_BUCKETS, dtype=F32)
    vals = jnp.einsum("qkb,bh->hqk", onehot, rel_bias.astype(F32), precision=lax.Precision.HIGHEST)
    vals = jnp.where(jnp.asarray(valid)[None], vals, NEG_INF)
    return vals.reshape(NSA_KV_HEADS, NSA_HPG * dist.shape[0], dist.shape[1])


def _head_masks():
    lane = lax.broadcasted_iota(jnp.int32, (Q_BLOCK, LANES), 1)
    return lane < NSA_HEAD_DIM


def _stack_heads(q_ref, low):
    parts = []
    for pair in range(NSA_HPG // 2):
        qp = q_ref[:, pair * LANES:(pair + 1) * LANES].astype(F32)
        parts.append(jnp.where(low, qp, 0.0))
        parts.append(jnp.where(low, 0.0, qp))
    return jnp.concatenate(parts, axis=0).astype(BF16)


def _unstack_heads(x, low):
    parts = []
    for pair in range(NSA_HPG // 2):
        a = x[(2 * pair) * Q_BLOCK:(2 * pair + 1) * Q_BLOCK]
        b = x[(2 * pair + 1) * Q_BLOCK:(2 * pair + 2) * Q_BLOCK]
        parts.append(jnp.where(low, a, pltpu.roll(b, NSA_HEAD_DIM, 1)))
    return jnp.concatenate(parts, axis=1)


def _normalize_pv(pv):
    den = pltpu.roll(pv, NSA_HEAD_DIM, 1)
    return pv * jnp.where(den > 0.0, 1.0 / den, 0.0)


def _nsa_cmp_kernel(q_ref, kk_ref, v1_ref, bias_ref, ov_ref, o_ref, imp_ref, *, nc):
    nsub = q_ref.shape[0] // Q_BLOCK
    low = _head_masks()
    jcol = lax.broadcasted_iota(jnp.int32, (Q_BLOCK, nc), 1)
    subs = range(nsub)
    qbs = [pl.program_id(2) * nsub + u for u in subs]
    starts = [pl.multiple_of(qb * CMP_LOOKBACK, SUBLANES) for qb in qbs]
    s4 = [_dot_nt(_stack_heads(q_ref.at[u * Q_BLOCK:(u + 1) * Q_BLOCK], low),
                  kk_ref[pl.ds(starts[u], nc), :].astype(BF16)) for u in subs]
    probs, psums = [], []
    for u in subs:
        exists = jcol >= (nc - CMP_LOOKBACK - CMP_LOOKBACK * qbs[u])
        psum = jnp.zeros((Q_BLOCK, nc), F32)
        ps = []
        for hp in range(NSA_HPG):
            rows = slice(hp * Q_BLOCK, (hp + 1) * Q_BLOCK)
            s = jnp.where(exists, s4[u][rows] + bias_ref[rows, :], NEG_INF)
            mx = jnp.max(s, axis=-1, keepdims=True)
            p = jnp.exp(s - mx)
            den = jnp.sum(p, axis=-1, keepdims=True)
            p = p * jnp.where(mx > 0.5 * NEG_INF, 1.0 / den, 0.0)
            psum = psum + p
            ps.append(p.astype(BF16))
        probs.append(jnp.concatenate(ps, axis=0))
        psums.append(psum)
    for u in subs:
        pv = _dot(probs[u], v1_ref[pl.ds(starts[u], nc), :].astype(BF16))
        o_ref[u * Q_BLOCK:(u + 1) * Q_BLOCK, :] = _unstack_heads(pv, low).astype(o_ref.dtype)
    for u in subs:
        hi, lo = _split_bf16(psums[u])
        imp_ref[:, u * Q_BLOCK:(u + 1) * Q_BLOCK] = _dot_nt(ov_ref[...], jnp.concatenate([hi, lo], axis=1))


def _nsa_cmp(q, kvc, vkc, bias_c, ov2, batch, seq):
    nc = seq // CMP_STRIDE
    ns = seq // SLC_BLOCK
    nqb = seq // Q_BLOCK
    rows = kvc.shape[2]
    gw = NSA_HPG * NSA_HEAD_DIM
    qrows = CMP_QBLOCKS_PER_STEP * Q_BLOCK
    assert nqb % CMP_QBLOCKS_PER_STEP == 0
    return pl.pallas_call(
        functools.partial(_nsa_cmp_kernel, nc=nc),
        out_shape=(jax.ShapeDtypeStruct((batch, seq, NSA_HEADS * NSA_HEAD_DIM), BF16),
                   jax.ShapeDtypeStruct((batch, NSA_KV_HEADS, ns, seq), F32)),
        grid=(batch, NSA_KV_HEADS, nqb // CMP_QBLOCKS_PER_STEP),
        in_specs=[
            pl.BlockSpec((None, qrows, gw), lambda b, g, i: (b, i, g)),
            pl.BlockSpec((None, None, rows, LANES), lambda b, g, i: (b, g, 0, 0)),
            pl.BlockSpec((None, None, rows, LANES), lambda b, g, i: (b, g, 0, 0)),
            pl.BlockSpec((None, NSA_HPG * Q_BLOCK, nc), lambda b, g, i: (g, 0, 0)),
            pl.BlockSpec((ns, 2 * nc), lambda b, g, i: (0, 0)),
        ],
        out_specs=(pl.BlockSpec((None, qrows, gw), lambda b, g, i: (b, i, g)),
                   pl.BlockSpec((None, None, ns, qrows), lambda b, g, i: (b, g, 0, i))),
        compiler_params=_cparams(("parallel", "parallel", "parallel")),
        name="nsa_compressed",
    )(q, kvc, vkc, bias_c, ov2)


def _topk_kernel(imp_ref, sel_ref, *, ns, ntop):
    toks = imp_ref.shape[1]
    t = pl.program_id(1) * toks + lax.broadcasted_iota(jnp.int32, (ns, toks), 1)
    qb = t >> int(math.log2(Q_BLOCK))
    js = lax.broadcasted_iota(jnp.int32, (ns, toks), 0)
    js_first = ns - 2 - 2 * qb
    js_cur = ns - 2 + ((t & (Q_BLOCK - 1)) >> int(math.log2(SLC_BLOCK)))
    causal = (js >= js_first) & (js <= js_cur)
    forced = (js == js_first) | (js == js_cur) | (js == js_cur - 1)
    n_forced = 1 + (js_cur - 1 >= js_first).astype(jnp.int32) + (js_first < js_cur - 1).astype(jnp.int32)
    n_pick = jnp.minimum(ntop, js_cur - js_first + 1) - n_forced
    ninf = -jnp.inf
    score = jnp.where(causal & jnp.logical_not(forced), imp_ref[...], ninf)
    sel = jnp.where(causal & forced, 1.0, 0.0)
    jsf = js.astype(F32)
    for it in range(ntop - 1):
        mx = jnp.max(score, axis=0, keepdims=True)
        first = jnp.min(jnp.where(score == mx, jsf, float(ns)), axis=0, keepdims=True)
        hit = (jsf == first) & (it < n_pick)
        sel = jnp.where(hit, 1.0, sel)
        score = jnp.where(hit, ninf, score)
    sel_ref[...] = sel.T.astype(sel_ref.dtype)


def _topk_select(imp_t, *, toks=512):
    b, g, ns, t = imp_t.shape
    toks = min(toks, t)
    assert t % toks == 0
    nt = t // toks
    sel = pl.pallas_call(
        functools.partial(_topk_kernel, ns=ns, ntop=min(SLC_TOP, ns)),
        out_shape=jax.ShapeDtypeStruct((b * g * t, ns), BF16),
        grid=(b * g, nt),
        in_specs=[pl.BlockSpec((None, ns, toks), lambda i, j: (i, 0, j))],
        out_specs=pl.BlockSpec((toks, ns), lambda i, j: (i * nt + j, 0)),
        compiler_params=_cparams(("parallel", "parallel")),
        name="nsa_topk",
    )(imp_t.reshape(b * g, ns, t))
    return sel.reshape(b, g, t, ns)


def _nsa_slc_win_kernel(q_ref, sel_ref, kks_ref, v1s_ref, kkw_ref, v1w_ref, bias_ref, bwin_ref,
                        eexp_ref, gates_ref, egate_ref, ocmp_ref, o_ref, m_ref, acc_ref, p_ref, alpha_ref,
                        *, ns, kt_near, nkt):
    qb = pl.program_id(2)
    low = _head_masks()
    kw = SLC_KEY_TILE
    kt0 = (ns - 2 - 2 * qb) // SLC_BLOCKS_PER_TILE

    m_ref[...] = jnp.full_like(m_ref, NEG_INF)
    acc_ref[...] = jnp.zeros_like(acc_ref)
    p_ref[...] = jnp.zeros_like(p_ref)
    alpha_ref[...] = jnp.ones_like(alpha_ref)
    q4 = _stack_heads(q_ref, low)
    selb = sel_ref[...]

    def row_start(kt):
        return pl.multiple_of(jnp.maximum(Q_BLOCK * qb + kw * kt, 0), Q_BLOCK)

    def accumulate(kt):
        pv = _dot(p_ref[...], v1s_ref[pl.ds(row_start(kt), kw), :])
        acc_ref[...] = alpha_ref[...] * acc_ref[...] + pv

    def body(kt):
        s4 = _dot_nt(q4, kks_ref[pl.ds(row_start(kt), kw), :])
        mb = (_dot(selb, eexp_ref[kt]) - 1.0) * (-NEG_INF)
        accumulate(kt - 1)
        bt = jnp.maximum(kt - kt_near + 1, 0)
        for hp in range(NSA_HPG):
            rows = slice(hp * Q_BLOCK, (hp + 1) * Q_BLOCK)
            s = s4[rows] + bias_ref[bt, rows, :] + mb
            m_old = m_ref[rows, :]
            m_new = jnp.maximum(m_old, jnp.max(s, axis=-1, keepdims=True))
            alpha_ref[rows, :] = jnp.exp(m_old - m_new)
            p_ref[rows, :] = jnp.exp(s - m_new[:, 0:1]).astype(BF16)
            m_ref[rows, :] = m_new

    kt_even = (kt0 // 2) * 2

    def body2(j, carry):
        body(kt_even + 2 * j)
        body(kt_even + 2 * j + 1)
        return carry

    lax.fori_loop(0, (nkt - kt_even) // 2, body2, 0)
    accumulate(nkt - 1)
    o_slc = _unstack_heads(_normalize_pv(acc_ref[...]), low)

    wk = WINDOW + Q_BLOCK
    ws = pl.multiple_of(Q_BLOCK * qb, Q_BLOCK)
    sw = _dot_nt(q4, kkw_ref[pl.ds(ws, wk), :])
    wcol = lax.broadcasted_iota(jnp.int32, (Q_BLOCK, wk), 1)
    in_seq = wcol >= (WINDOW - Q_BLOCK * qb)
    ps = []
    for hp in range(NSA_HPG):
        rows = slice(hp * Q_BLOCK, (hp + 1) * Q_BLOCK)
        s = jnp.where(in_seq, sw[rows] + bwin_ref[rows, :], NEG_INF)
        ps.append(jnp.exp(s - jnp.max(s, axis=-1, keepdims=True)).astype(BF16))
    pvw = _dot(jnp.concatenate(ps, axis=0), v1w_ref[pl.ds(ws, wk), :])
    o_win = _unstack_heads(_normalize_pv(pvw), low)

    gh, gl = _split_bf16(gates_ref[...])
    g2 = jnp.concatenate([gh, gl], axis=1)
    o = (_dot(g2, egate_ref[0]) * ocmp_ref[...].astype(F32)
         + _dot(g2, egate_ref[1]) * o_slc
         + _dot(g2, egate_ref[2]) * o_win)
    o_ref[...] = o.astype(o_ref.dtype)


def _nsa_slc_win(q, sel, kvs, vks, kvw, vkw, bias_s, bwin, eexp, gates, egate, ocmp, batch, seq):
    ns = seq // SLC_BLOCK
    nqb = seq // Q_BLOCK
    nkt = (ns * SLC_BLOCK) // SLC_KEY_TILE
    kt_near = nkt - (bias_s.shape[1] - 1)
    gw = NSA_HPG * NSA_HEAD_DIM
    srows = kvs.shape[2]
    wrows = kvw.shape[2]
    slab = lambda r: pl.BlockSpec((None, None, r, LANES), lambda b, g, i: (b, g, 0, 0))
    return pl.pallas_call(
        functools.partial(_nsa_slc_win_kernel, ns=ns, kt_near=kt_near, nkt=nkt),
        out_shape=jax.ShapeDtypeStruct((batch, seq, NSA_HEADS * NSA_HEAD_DIM), BF16),
        grid=(batch, NSA_KV_HEADS, nqb),
        in_specs=[
            pl.BlockSpec((None, Q_BLOCK, gw), lambda b, g, i: (b, i, g)),
            pl.BlockSpec((None, None, Q_BLOCK, ns), lambda b, g, i: (b, g, i, 0)),
            slab(srows), slab(srows), slab(wrows), slab(wrows),
            pl.BlockSpec((None,) + bias_s.shape[1:], lambda b, g, i: (g, 0, 0, 0)),
            pl.BlockSpec((None,) + bwin.shape[1:], lambda b, g, i: (g, 0, 0)),
            pl.BlockSpec(eexp.shape, lambda b, g, i: (0, 0, 0)),
            pl.BlockSpec((None, Q_BLOCK, LANES), lambda b, g, i: (b, i, 0)),
            pl.BlockSpec((N_BRANCH, 2 * LANES, gw), lambda b, g, i: (0, 0, g)),
            pl.BlockSpec((None, Q_BLOCK, gw), lambda b, g, i: (b, i, g)),
        ],
        out_specs=pl.BlockSpec((None, Q_BLOCK, gw), lambda b, g, i: (b, i, g)),
        scratch_shapes=[pltpu.VMEM((NSA_HPG * Q_BLOCK, LANES), F32),
                        pltpu.VMEM((NSA_HPG * Q_BLOCK, LANES), F32),
                        pltpu.VMEM((NSA_HPG * Q_BLOCK, SLC_KEY_TILE), BF16),
                        pltpu.VMEM((NSA_HPG * Q_BLOCK, LANES), F32)],
        compiler_params=_cparams(("parallel", "parallel", "arbitrary")),
        name="nsa_selected_window",
    )(q, sel, kvs, vks, kvw, vkw, bias_s, bwin, eexp, gates, egate, ocmp)


def _nsa_tables(rel_bias, seq):
    nc = seq // CMP_STRIDE
    ns = seq // SLC_BLOCK
    i = np.arange(Q_BLOCK)[:, None]
    j = np.arange(nc)[None, :]
    dist_c = i - (CMP_BLOCK - 1) - CMP_STRIDE * (j - (nc - CMP_LOOKBACK))
    bias_c = _bias_table(rel_bias, dist_c, dist_c >= 0)
    cs = np.arange(nc)[:, None] * CMP_STRIDE
    ss = np.arange(ns)[None, :] * SLC_BLOCK
    ov = np.clip(np.minimum(cs + CMP_BLOCK, ss + SLC_BLOCK) - np.maximum(cs, ss), 0, None) / CMP_BLOCK
    ov2 = jnp.asarray(np.concatenate([ov, ov], axis=0).T, BF16)
    nkeys = ns * SLC_BLOCK
    nkt = nkeys // SLC_KEY_TILE
    pad_rows = (ns - 2) * SLC_BLOCK
    kt_near = max(0, pad_rows - REL_MAX_DIST) // SLC_KEY_TILE
    kr = np.arange(kt_near * SLC_KEY_TILE, nkeys)[None, :]
    dist_s = i + pad_rows - kr
    near = _bias_table(rel_bias, dist_s, dist_s >= 0)
    near = near.reshape(NSA_KV_HEADS, NSA_HPG * Q_BLOCK, nkt - kt_near, SLC_KEY_TILE).transpose(0, 2, 1, 3)
    far = jnp.broadcast_to(rel_bias.astype(F32)[REL_BUCKETS - 1].reshape(NSA_KV_HEADS, 1, NSA_HPG, 1, 1),
                           (NSA_KV_HEADS, 1, NSA_HPG, Q_BLOCK, SLC_KEY_TILE))
    bias_s = jnp.concatenate([far.reshape(NSA_KV_HEADS, 1, NSA_HPG * Q_BLOCK, SLC_KEY_TILE), near], axis=1)
    dist_w = i + WINDOW - np.arange(WINDOW + Q_BLOCK)[None, :]
    bwin = _bias_table(rel_bias, dist_w, (dist_w >= 0) & (dist_w < WINDOW))
    blk = np.arange(ns)[None, :, None]
    key = np.arange(SLC_KEY_TILE)[None, None, :]
    ktile = np.arange(nkt)[:, None, None]
    eexp = jnp.asarray(blk == ktile * SLC_BLOCKS_PER_TILE + key // SLC_BLOCK, BF16)
    col = np.arange(LANES)[:, None]
    head = (np.arange(NSA_HEADS * NSA_HEAD_DIM) // NSA_HEAD_DIM)[None, :]
    eg = np.stack([col == head * N_BRANCH + br for br in range(N_BRANCH)])
    egate = jnp.asarray(np.concatenate([eg, eg], axis=1), BF16)
    return bias_c, ov2, bias_s, bwin, eexp, egate


def _router_kernel(x_ref, g_ref, w_ref, o_ref):
    x = x_ref[...]
    ms = jnp.mean(x * x, axis=-1, keepdims=True)
    xn = x * lax.rsqrt(ms + NORM_EPS) * g_ref[...]
    logits = jnp.dot(xn, w_ref[...], preferred_element_type=F32, precision=lax.Precision.HIGHEST)
    lane = lax.broadcasted_iota(jnp.int32, logits.shape, 1).astype(F32)
    ninf = -jnp.inf
    s = jnp.where(lane < N_EXPERTS, logits, ninf)
    m1 = jnp.max(s, axis=-1, keepdims=True)
    i1 = jnp.min(jnp.where(s == m1, lane, float(LANES)), axis=-1, keepdims=True)
    s2 = jnp.where(lane == i1, ninf, s)
    m2 = jnp.max(s2, axis=-1, keepdims=True)
    i2 = jnp.min(jnp.where(s2 == m2, lane, float(LANES)), axis=-1, keepdims=True)
    e2 = jnp.exp(m2 - m1)
    w1 = 1.0 / (1.0 + e2)
    w2 = e2 * w1
    o_ref[...] = (jnp.where(lane == 0.0, i1, 0.0) + jnp.where(lane == 1.0, i2, 0.0)
                  + jnp.where(lane == 2.0, w1, 0.0) + jnp.where(lane == 3.0, w2, 0.0))


def _router(x, g, w_router, *, tm=512):
    n, d = x.shape
    tm = min(tm, n)
    wpad = jnp.zeros((d, LANES), F32).at[:, :N_EXPERTS].set(w_router.astype(F32))
    return pl.pallas_call(
        _router_kernel,
        out_shape=jax.ShapeDtypeStruct((n, LANES), F32),
        grid=(n // tm,),
        in_specs=[
            pl.BlockSpec((tm, d), lambda i: (i, 0)),
            pl.BlockSpec((1, d), lambda i: (0, 0)),
            pl.BlockSpec((d, LANES), lambda i: (0, 0)),
        ],
        out_specs=pl.BlockSpec((tm, LANES), lambda i: (i, 0)),
        compiler_params=_cparams(("parallel",)),
        name="moe_router",
    )(x, g.reshape(1, d).astype(F32), wpad)


def _moe_dispatch(route, tm):
    n = route.shape[0]
    pairs = 2 * n
    e = route[:, 0:2].astype(jnp.int32).reshape(pairs)
    onehot = (e[:, None] == jnp.arange(N_EXPERTS, dtype=jnp.int32)[None, :]).astype(jnp.int32)
    csum = jnp.cumsum(onehot, axis=0)
    rank = jnp.sum(csum * onehot, axis=1) - 1
    counts = csum[-1]
    cpad = ((counts + tm - 1) // tm) * tm
    gend = jnp.cumsum(cpad)
    dst = jnp.sum(onehot * (gend - cpad)[None, :], axis=1) + rank
    rows = pairs + N_EXPERTS * tm
    ntiles = rows // tm
    nvalid = gend[-1] // tm
    tile = jnp.arange(ntiles, dtype=jnp.int32)
    te = jnp.sum((tile[:, None] * tm >= gend[None, :]).astype(jnp.int32), axis=1)
    te = jnp.where(tile < nvalid, te, te[nvalid - 1])
    src = jnp.zeros((rows,), jnp.int32).at[dst].set(jnp.arange(pairs, dtype=jnp.int32) // 2)
    return src, dst, te.astype(jnp.int32), nvalid.reshape(1).astype(jnp.int32)


def _row_copy(src_hbm, row, dst_ref, j, sem):
    return pltpu.make_async_copy(src_hbm.at[pl.ds(row, 1), :], dst_ref.at[pl.ds(j, 1), :], sem)


def _gather_rows_kernel(idx_ref, src_hbm, o_ref, sem):
    rows = o_ref.shape[0]

    def issue(g, c):
        for u in range(DMA_ISSUE_UNROLL):
            j = g * DMA_ISSUE_UNROLL + u
            _row_copy(src_hbm, idx_ref[j], o_ref, j, sem).start(priority=u % 2)
        return c

    lax.fori_loop(0, rows // DMA_ISSUE_UNROLL, issue, 0)
    pltpu.make_async_copy(src_hbm.at[pl.ds(0, rows), :], o_ref, sem).wait()


def _gather_rows(x, idx, *, rows=1024):
    n, d = x.shape
    total = idx.shape[0]
    assert total % rows == 0
    return pl.pallas_call(
        _gather_rows_kernel,
        out_shape=jax.ShapeDtypeStruct((total, d), x.dtype),
        grid=(total // rows,),
        in_specs=[
            pl.BlockSpec((rows,), lambda i: (i,), memory_space=pltpu.SMEM),
            pl.BlockSpec(memory_space=pl.ANY),
        ],
        out_specs=pl.BlockSpec((rows, d), lambda i: (i, 0)),
        scratch_shapes=[pltpu.SemaphoreType.DMA],
        compiler_params=_cparams(("arbitrary",)),
        name="moe_gather",
    )(idx, x)


def _combine_kernel(i0_ref, i1_ref, y_hbm, h_ref, route_ref, o_ref, buf_ref, sem):
    rows = h_ref.shape[0]

    def issue(g, c):
        for u in range(DMA_ISSUE_UNROLL):
            j = g * DMA_ISSUE_UNROLL + u
            _row_copy(y_hbm, i0_ref[j], buf_ref.at[0], j, sem).start(priority=0)
            _row_copy(y_hbm, i1_ref[j], buf_ref.at[1], j, sem).start(priority=1)
        return c

    lax.fori_loop(0, rows // DMA_ISSUE_UNROLL, issue, 0)
    for k in range(2):
        pltpu.make_async_copy(y_hbm.at[pl.ds(0, rows), :], buf_ref.at[k], sem).wait()
    route = route_ref[...]
    o_ref[...] = h_ref[...] + route[:, 2:3] * buf_ref[0] + route[:, 3:4] * buf_ref[1]


def _combine(h, y, dst, route, *, rows=512):
    n, d = h.shape
    assert n % rows == 0
    dst2 = dst.reshape(n, 2)
    return pl.pallas_call(
        _combine_kernel,
        out_shape=jax.ShapeDtypeStruct((n, d), F32),
        grid=(n // rows,),
        in_specs=[
            pl.BlockSpec((rows,), lambda i: (i,), memory_space=pltpu.SMEM),
            pl.BlockSpec((rows,), lambda i: (i,), memory_space=pltpu.SMEM),
            pl.BlockSpec(memory_space=pl.ANY),
            pl.BlockSpec((rows, d), lambda i: (i, 0)),
            pl.BlockSpec((rows, LANES), lambda i: (i, 0)),
        ],
        out_specs=pl.BlockSpec((rows, d), lambda i: (i, 0)),
        scratch_shapes=[pltpu.VMEM((2, rows, d), F32), pltpu.SemaphoreType.DMA],
        compiler_params=_cparams(("arbitrary",)),
        name="moe_combine",
    )(dst2[:, 0], dst2[:, 1], y, h, route)


def _ffn_grouped_kernel(te_ref, nv_ref, x_ref, g_ref, wg_ref, wu_ref, wd_ref, o_ref, xn_ref, acc_ref):
    i = pl.program_id(0)
    f = pl.program_id(1)
    used = i < nv_ref[0]

    @pl.when(jnp.logical_and(used, f == 0))
    def _():
        x = x_ref[...]
        ms = jnp.mean(x * x, axis=-1, keepdims=True)
        xn_ref[...] = (x * lax.rsqrt(ms + NORM_EPS) * g_ref[...]).astype(BF16)
        acc_ref[...] = jnp.zeros_like(acc_ref)

    @pl.when(used)
    def _():
        xn = xn_ref[...]
        gate = _dot(xn, wg_ref[...])
        up = _dot(xn, wu_ref[...])
        act = (gate * jax.nn.sigmoid(gate) * up).astype(BF16)
        acc_ref[...] += _dot(act, wd_ref[...])

    last = f == pl.num_programs(1) - 1

    @pl.when(jnp.logical_and(used, last))
    def _():
        o_ref[...] = acc_ref[...]

    @pl.when(jnp.logical_and(jnp.logical_not(used), last))
    def _():
        o_ref[...] = jnp.zeros_like(o_ref)


def _ffn_grouped(x, g, w_gu, w_down, te, nvalid, *, tm, tf=None):
    rows, d = x.shape
    ff = w_down.shape[1]
    if tf is None:
        tf = ff // 2 if (ff // 2) % LANES == 0 else ff
    assert rows % tm == 0 and ff % tf == 0 and tf % LANES == 0
    nf = ff // tf

    def fcol(i, f, te_ref, nv_ref):
        return jnp.where(i < nv_ref[0], f, nf - 1)

    grid_spec = pltpu.PrefetchScalarGridSpec(
        num_scalar_prefetch=2,
        grid=(rows // tm, nf),
        in_specs=[
            pl.BlockSpec((tm, d), lambda i, f, te_ref, nv_ref: (i, 0)),
            pl.BlockSpec((1, d), lambda i, f, te_ref, nv_ref: (0, 0)),
            pl.BlockSpec((None, d, tf), lambda i, f, te_ref, nv_ref: (te_ref[i], 0, fcol(i, f, te_ref, nv_ref))),
            pl.BlockSpec((None, d, tf),
                         lambda i, f, te_ref, nv_ref: (te_ref[i], 0, nf + fcol(i, f, te_ref, nv_ref))),
            pl.BlockSpec((None, tf, d), lambda i, f, te_ref, nv_ref: (te_ref[i], fcol(i, f, te_ref, nv_ref), 0)),
        ],
        out_specs=pl.BlockSpec((tm, d), lambda i, f, te_ref, nv_ref: (i, 0)),
        scratch_shapes=[pltpu.VMEM((tm, d), BF16), pltpu.VMEM((tm, d), F32)],
    )
    return pl.pallas_call(
        _ffn_grouped_kernel,
        out_shape=jax.ShapeDtypeStruct((rows, d), F32),
        grid_spec=grid_spec,
        compiler_params=_cparams(("arbitrary", "arbitrary")),
        name="expert_ffn_grouped",
    )(te, nvalid, x, g.reshape(1, d).astype(F32), w_gu, w_gu, w_down)


def _nsa_layer(h, batch, seq, norm_g, kv_norm_g, w_kv, k_norm_g, cmp_pos_k, cmp_w1_k, cmp_w2_k,
               cmp_pos_v, cmp_w1_v, cmp_w2_v, w_in, w_out, q_norm_g, rel_bias):
    n, d = h.shape
    g_heads, dh = NSA_KV_HEADS, NSA_HEAD_DIM
    nc = seq // CMP_STRIDE
    ns = seq // SLC_BLOCK
    hq = NSA_HEADS * dh

    kvproj = _norm_matmul(h, kv_norm_g, w_kv.astype(BF16), name="nsa_kv_proj")

    gw_kv = g_heads * dh

    def chunks(col0):
        t = kvproj[:, col0:col0 + gw_kv].astype(BF16).reshape(batch, seq, g_heads, dh)
        return t.transpose(0, 2, 1, 3).reshape(batch, g_heads, nc, CMP_STRIDE * dh)

    k_cmp = _compress(chunks(0), cmp_w1_k, cmp_pos_k, cmp_w2_k, k_norm_g[0],
                      apply_norm=True, name="nsa_compress_k")
    v_cmp = _compress(chunks(gw_kv), cmp_w1_v, cmp_pos_v, cmp_w2_v, k_norm_g[0],
                      apply_norm=False, name="nsa_compress_v")
    cpad = nc - CMP_LOOKBACK
    padc = lambda t: jnp.pad(t, ((0, 0), (0, 0), (cpad, 0), (0, 0)))
    kvc = padc(jnp.concatenate([k_cmp, k_cmp], axis=-1))
    vkc = padc(jnp.concatenate([v_cmp, jnp.ones_like(v_cmp)], axis=-1))

    kvs, vks = _pack_kv(kvproj, k_norm_g[1], 1, batch, seq, name="nsa_pack_selected")
    kvw, vkw = _pack_kv(kvproj, k_norm_g[2], 2, batch, seq, name="nsa_pack_window")
    spad = (ns - 2) * SLC_BLOCK
    pads = lambda t, r: jnp.pad(t, ((0, 0), (0, 0), (r, 0), (0, 0)))
    kvs, vks = pads(kvs, spad), pads(vks, spad)
    kvw, vkw = pads(kvw, WINDOW), pads(vkw, WINDOW)

    qgain = jnp.tile(q_norm_g.astype(F32), NSA_HEADS) * (dh ** -0.5)
    q = _norm_matmul(h, norm_g, w_in[:, :hq].astype(BF16), epilogue="headnorm", gain=qgain,
                     flag=jnp.ones((hq,), F32), out_dtype=BF16, name="nsa_q_proj")
    ngate = N_BRANCH * NSA_HEADS
    wg = jnp.zeros((d, LANES), BF16).at[:, :ngate].set(w_in[:, hq:].astype(BF16))
    gates = _norm_matmul(h, norm_g, wg, epilogue="sigmoid", name="nsa_gate_proj")

    bias_c, ov2, bias_s, bwin, eexp, egate = _nsa_tables(rel_bias, seq)
    q3 = q.reshape(batch, seq, hq)
    o_cmp, imp_t = _nsa_cmp(q3, kvc, vkc, bias_c, ov2, batch, seq)
    sel = _topk_select(imp_t)
    o = _nsa_slc_win(q3, sel, kvs, vks, kvw, vkw, bias_s, bwin, eexp,
                     gates.reshape(batch, seq, LANES), egate, o_cmp, batch, seq)
    return _matmul_res(o.reshape(n, hq), w_out.astype(BF16), h, name="nsa_out_proj")


def kernel(x, norm_mix_g, norm_ffn_g, a_w_in, a_w_out, a_onorm_g, lb_param, kv_norm_g, w_kv, k_norm_g,
           cmp_pos_k, cmp_w1_k, cmp_w2_k, cmp_pos_v, cmp_w1_v, cmp_w2_v, b_w_in, b_w_out, b_qnorm_g,
           rel_bias, ffn_w_gu, ffn_w_down, moe_router, moe_w_gu, moe_w_down):
    batch, seq, d = x.shape
    n = batch * seq
    h = x.reshape(n, d).astype(F32)

    lower = jnp.cumsum(jax.nn.softmax(lb_param.astype(F32), axis=0), axis=0)[0]
    proj = _norm_matmul(h, norm_mix_g[0], a_w_in[0].astype(BF16), name="hgrn_in_proj")
    og = _hgrn_mixer(proj, lower, a_onorm_g[0], batch, seq)
    h = _matmul_res(og, a_w_out[0].astype(BF16), h, name="hgrn_out_proj")
    h = _ffn(h, norm_ffn_g[0], ffn_w_gu[0].astype(BF16), ffn_w_down[0].astype(BF16), tm=1024, name="dense_ffn")

    h = _nsa_layer(h, batch, seq, norm_mix_g[1], kv_norm_g, w_kv, k_norm_g, cmp_pos_k, cmp_w1_k, cmp_w2_k,
                   cmp_pos_v, cmp_w1_v, cmp_w2_v, b_w_in[0], b_w_out[0], b_qnorm_g[0], rel_bias)
    route = _router(h, norm_ffn_g[1], moe_router[0])
    src, dst, tile_expert, tiles_used = _moe_dispatch(route, MOE_ROW_TILE)
    ys = _ffn_grouped(_gather_rows(h, src), norm_ffn_g[1], moe_w_gu[0].astype(BF16), moe_w_down[0].astype(BF16),
                      tile_expert, tiles_used, tm=MOE_ROW_TILE)
    out = _combine(h, ys, dst, route)
    return out.reshape(batch, seq, d).astype(x.dtype)
```

```python
import functools
import math

import jax
import jax.numpy as jnp
import numpy as np
from jax import lax
from jax.experimental import pallas as pl
from jax.experimental.pallas import tpu as pltpu

F32 = jnp.float32
BF16 = jnp.bfloat16

NORM_EPS = 1e-6
NEG_INF = -1e30
FORCE_BONUS = 1e4
HGRN_HEAD_DIM = 128
NSA_HEADS = 16
NSA_KV_HEADS = 4
NSA_HPG = NSA_HEADS // NSA_KV_HEADS
NSA_HEAD_DIM = 64
N_BRANCH = 3
CMP_BLOCK = 32
CMP_STRIDE = 16
SLC_BLOCK = 64
SLC_TOP = 16
WINDOW = 512
Q_BLOCK = 128
REL_BUCKETS = 32
REL_MAX_DIST = 2048
N_EXPERTS = 8

LANES = 128
SUBLANES = 8
VMEM_LIMIT_BYTES = 56 * 1024 * 1024

HGRN_CHUNK = 128
HGRN_ROWS_PER_STEP = 512
HGRN_HEADS_PER_STEP = 8
SLC_KEY_TILE = 512
SLC_BLOCKS_PER_TILE = SLC_KEY_TILE // SLC_BLOCK
CMP_LOOKBACK = Q_BLOCK // CMP_STRIDE
CMP_QBLOCKS_PER_STEP = 4
MOE_ROW_TILE = 512
DMA_ISSUE_UNROLL = 8


def _cparams(sem):
    return pltpu.CompilerParams(dimension_semantics=sem, vmem_limit_bytes=VMEM_LIMIT_BYTES)


def _dot(a, b):
    return jnp.dot(a, b, preferred_element_type=F32)


def _dot_nt(a, b):
    return lax.dot_general(a, b, (((1,), (1,)), ((), ())), preferred_element_type=F32)


def _split_bf16(x):
    hi = x.astype(BF16)
    lo = (x - hi.astype(F32)).astype(BF16)
    return hi, lo


def _norm_matmul_kernel(x_ref, g_ref, w_ref, *rest, epilogue, tn):
    if epilogue == "headnorm":
        gain_ref, flag_ref, bd_ref, o_ref = rest
    else:
        (o_ref,) = rest
    x = x_ref[...]
    ms = jnp.mean(x * x, axis=-1, keepdims=True)
    xn = (x * lax.rsqrt(ms + NORM_EPS) * g_ref[...]).astype(BF16)
    for c in range(o_ref.shape[1] // tn):
        cols = slice(c * tn, (c + 1) * tn)
        acc = _dot(xn, w_ref[:, cols])
        if epilogue == "headnorm":
            ss = _dot((acc * acc).astype(BF16), bd_ref[...])
            normed = acc * lax.rsqrt(ss * (1.0 / NSA_HEAD_DIM) + NORM_EPS) * gain_ref[:, cols]
            acc = jnp.where(flag_ref[:, cols] > 0.0, normed, acc)
        elif epilogue == "sigmoid":
            acc = jax.nn.sigmoid(acc)
        o_ref[:, cols] = acc.astype(o_ref.dtype)


def _norm_matmul(x, g, w, *, epilogue="none", gain=None, flag=None, out_dtype=F32, tm=512, tn=512,
                 name="norm_matmul"):
    n, d = x.shape
    m = w.shape[1]
    tm = min(tm, n)
    tn = min(tn, m)
    assert n % tm == 0 and m % tn == 0, (n, tm, m, tn)
    const = lambda i: (0, 0)
    in_specs = [pl.BlockSpec((tm, d), lambda i: (i, 0)), pl.BlockSpec((1, d), const), pl.BlockSpec((d, m), const)]
    args = [x, g.reshape(1, d).astype(F32), w]
    if epilogue == "headnorm":
        assert tn % NSA_HEAD_DIM == 0
        grp = np.arange(tn) // NSA_HEAD_DIM
        bd = jnp.asarray(grp[:, None] == grp[None, :], BF16)
        in_specs += [pl.BlockSpec((1, m), const), pl.BlockSpec((1, m), const), pl.BlockSpec((tn, tn), const)]
        args += [gain.reshape(1, m).astype(F32), flag.reshape(1, m).astype(F32), bd]
    return pl.pallas_call(
        functools.partial(_norm_matmul_kernel, epilogue=epilogue, tn=tn),
        out_shape=jax.ShapeDtypeStruct((n, m), out_dtype),
        grid=(n // tm,),
        in_specs=in_specs,
        out_specs=pl.BlockSpec((tm, m), lambda i: (i, 0)),
        compiler_params=_cparams(("parallel",)),
        name=name,
    )(*args)


def _matmul_res_kernel(a_ref, w_ref, r_ref, o_ref, *, tn):
    a = a_ref[...]
    for c in range(o_ref.shape[1] // tn):
        cols = slice(c * tn, (c + 1) * tn)
        o_ref[:, cols] = r_ref[:, cols] + _dot(a, w_ref[:, cols])


def _matmul_res(a, w, res, *, tm=512, tn=512, name="matmul_res"):
    n, k = a.shape
    m = w.shape[1]
    tm = min(tm, n)
    tn = min(tn, m)
    assert n % tm == 0 and m % tn == 0
    return pl.pallas_call(
        functools.partial(_matmul_res_kernel, tn=tn),
        out_shape=jax.ShapeDtypeStruct((n, m), F32),
        grid=(n // tm,),
        in_specs=[
            pl.BlockSpec((tm, k), lambda i: (i, 0)),
            pl.BlockSpec((k, m), lambda i: (0, 0)),
            pl.BlockSpec((tm, m), lambda i: (i, 0)),
        ],
        out_specs=pl.BlockSpec((tm, m), lambda i: (i, 0)),
        compiler_params=_cparams(("parallel",)),
        name=name,
    )(a, w, res)


def _ffn_kernel(x_ref, g_ref, wg_ref, wu_ref, wd_ref, o_ref, xn_ref, acc_ref):
    f = pl.program_id(1)

    @pl.when(f == 0)
    def _():
        x = x_ref[...]
        ms = jnp.mean(x * x, axis=-1, keepdims=True)
        xn_ref[...] = (x * lax.rsqrt(ms + NORM_EPS) * g_ref[...]).astype(BF16)
        acc_ref[...] = jnp.zeros_like(acc_ref)

    xn = xn_ref[...]
    gate = _dot(xn, wg_ref[...])
    up = _dot(xn, wu_ref[...])
    act = (gate * jax.nn.sigmoid(gate) * up).astype(BF16)
    acc_ref[...] += _dot(act, wd_ref[...])

    @pl.when(f == pl.num_programs(1) - 1)
    def _():
        o_ref[...] = x_ref[...] + acc_ref[...]


def _ffn(x, g, w_gu, w_down, *, tm=512, tf=None, name="ffn"):
    n, d = x.shape
    ff = w_down.shape[0]
    if tf is None:
        tf = ff // 2 if (ff // 2) % LANES == 0 else ff
    tm = min(tm, n)
    assert n % tm == 0 and ff % tf == 0 and tf % LANES == 0
    nf = ff // tf
    return pl.pallas_call(
        _ffn_kernel,
        out_shape=jax.ShapeDtypeStruct((n, d), F32),
        grid=(n // tm, nf),
        in_specs=[
            pl.BlockSpec((tm, d), lambda i, f: (i, 0)),
            pl.BlockSpec((1, d), lambda i, f: (0, 0)),
            pl.BlockSpec((d, tf), lambda i, f: (0, f)),
            pl.BlockSpec((d, tf), lambda i, f: (0, nf + f)),
            pl.BlockSpec((tf, d), lambda i, f: (f, 0)),
        ],
        out_specs=pl.BlockSpec((tm, d), lambda i, f: (i, 0)),
        scratch_shapes=[pltpu.VMEM((tm, d), BF16), pltpu.VMEM((tm, d), F32)],
        compiler_params=_cparams(("parallel", "arbitrary")),
        name=name,
    )(x, g.reshape(1, d).astype(F32), w_gu, w_gu, w_down)


def _hgrn_decay_matrix(c):
    levels = int(math.log2(c))
    out = np.zeros(((levels + 2) * c, c), np.float32)
    for l in range(levels):
        m = c >> (l + 1)
        for r in range(c):
            mid = (r // (2 * m)) * 2 * m + m - 1
            if r % (2 * m) >= m:
                out[l * c + r, mid + 1:r + 1] = 1.0
            else:
                out[l * c + r, r + 1:mid + 1] = 1.0
    for r in range(c):
        out[levels * c + r, :r + 1] = 1.0
        out[(levels + 1) * c + r, r + 1:] = 1.0
    return out


def _hgrn_kernel(q_ref, f_ref, v_ref, g_ref, lb_ref, gn_ref, m_ref, o_ref, st_ref, *, chunk, nchunk):
    c = chunk
    dh = HGRN_HEAD_DIM
    levels = int(math.log2(c))

    @pl.when(pl.program_id(2) == 0)
    def _():
        st_ref[...] = jnp.zeros_like(st_ref)

    gn = gn_ref[...]
    row = lax.broadcasted_iota(jnp.int32, (c, c), 0)
    col = lax.broadcasted_iota(jnp.int32, (c, c), 1)
    rowv = lax.broadcasted_iota(jnp.int32, (c, dh), 0)
    nheads = st_ref.shape[0]

    def body(ci, carry):
        sl = pl.ds(pl.multiple_of(ci * c, c), c)
        lb = lb_ref[...]
        fg = lb + (1.0 - lb) * jax.nn.sigmoid(f_ref[sl, :])
        kall = 1.0 - fg
        hi, lo = _split_bf16(jnp.log(fg))
        dall = _dot(m_ref[...], jnp.concatenate([hi, lo], axis=0))
        hs = range(nheads)
        cols = [slice(hh * dh, (hh + 1) * dh) for hh in hs]
        q = [q_ref[sl, cols[hh]] for hh in hs]
        k = [kall[:, cols[hh]] for hh in hs]
        v = [v_ref[sl, cols[hh]] for hh in hs]

        a = [jnp.where(row == col, _dot_nt(q[hh].astype(BF16), k[hh].astype(BF16)), 0.0) for hh in hs]
        for l in range(levels):
            sh = levels - 1 - l
            upper = ((rowv >> sh) & 1) == 1
            same = (row >> (sh + 1)) == (col >> (sh + 1))
            for hh in hs:
                e = jnp.exp(dall[l * c:(l + 1) * c, cols[hh]])
                qe = jnp.where(upper, q[hh] * e, 0.0).astype(BF16)
                ke = jnp.where(upper, 0.0, k[hh] * e).astype(BF16)
                a[hh] = a[hh] + jnp.where(same, _dot_nt(qe, ke), 0.0)
        b = [dall[levels * c:(levels + 1) * c, cols[hh]] for hh in hs]
        st = [st_ref[hh] for hh in hs]
        o = [_dot(a[hh].astype(BF16), v[hh].astype(BF16))
             + _dot_nt((q[hh] * jnp.exp(b[hh])).astype(BF16), st[hh].astype(BF16)) for hh in hs]
        for hh in hs:
            kr = (k[hh] * jnp.exp(dall[(levels + 1) * c:, cols[hh]])).astype(BF16)
            st_ref[hh] = st[hh] * jnp.exp(b[hh][c - 1:c, :]) + _dot(v[hh].T.astype(BF16), kr)
        for hh in hs:
            ms = jnp.mean(o[hh] * o[hh], axis=-1, keepdims=True)
            on = o[hh] * lax.rsqrt(ms + NORM_EPS) * gn
            gg = g_ref[sl, cols[hh]]
            o_ref[sl, cols[hh]] = (on * (gg * jax.nn.sigmoid(gg))).astype(o_ref.dtype)
        return carry

    lax.fori_loop(0, nchunk, body, 0)


def _hgrn_mixer(proj, lb, gn, batch, seq):
    n, d4 = proj.shape
    d = d4 // 4
    heads = d // HGRN_HEAD_DIM
    rows = min(HGRN_ROWS_PER_STEP, seq)
    chunk = min(HGRN_CHUNK, rows)
    assert seq % rows == 0 and rows % chunk == 0
    nt = seq // rows
    m1 = _hgrn_decay_matrix(chunk)
    m = jnp.asarray(np.concatenate([m1, m1], axis=1), BF16)

    hps = HGRN_HEADS_PER_STEP
    assert heads % hps == 0
    groups = heads // hps
    width = hps * HGRN_HEAD_DIM

    def col_spec(s):
        return pl.BlockSpec((rows, width), lambda b, h, t, s=s: (b * nt + t, s * groups + h))

    return pl.pallas_call(
        functools.partial(_hgrn_kernel, chunk=chunk, nchunk=rows // chunk),
        out_shape=jax.ShapeDtypeStruct((n, d), BF16),
        grid=(batch, groups, nt),
        in_specs=[
            col_spec(0), col_spec(1), col_spec(2), col_spec(3),
            pl.BlockSpec((1, width), lambda b, h, t: (0, h)),
            pl.BlockSpec((1, HGRN_HEAD_DIM), lambda b, h, t: (0, 0)),
            pl.BlockSpec(m.shape, lambda b, h, t: (0, 0)),
        ],
        out_specs=pl.BlockSpec((rows, width), lambda b, h, t: (b * nt + t, h)),
        scratch_shapes=[pltpu.VMEM((hps, HGRN_HEAD_DIM, HGRN_HEAD_DIM), F32)],
        compiler_params=_cparams(("parallel", "parallel", "arbitrary")),
        name="hgrn2_recurrence",
    )(proj, proj, proj, proj, lb.reshape(1, d).astype(F32), gn.reshape(1, HGRN_HEAD_DIM).astype(F32), m)


def _compress_kernel(x_ref, w1c_ref, pos_ref, w1_ref, w2_ref, gain_ref, o_ref, *, apply_norm):
    hid = w2_ref.shape[0]
    nc = x_ref.shape[0]
    uv = _dot(x_ref[...], w1c_ref[...])
    posb = _dot(pos_ref[...], w1_ref[...])[0:1, :]
    pre = uv[:, :hid] + pltpu.roll(uv[:, hid:], nc - 1, 0) + posb
    out = _dot(jax.nn.gelu(pre).astype(BF16), w2_ref[...])
    if apply_norm:
        ms = jnp.mean(out * out, axis=-1, keepdims=True)
        out = out * lax.rsqrt(ms + NORM_EPS) * gain_ref[...]
    o_ref[...] = out


def _compress(x, w1, pos, w2, gain, *, apply_norm, name):
    b, g, nc, half = x.shape
    hid = w1.shape[1]
    dh = w2.shape[1]
    w1c = jnp.concatenate([w1[:half], w1[half:]], axis=1).astype(BF16)
    posr = jnp.broadcast_to(pos.reshape(1, -1), (SUBLANES, pos.size)).astype(BF16)
    return pl.pallas_call(
        functools.partial(_compress_kernel, apply_norm=apply_norm),
        out_shape=jax.ShapeDtypeStruct((b, g, nc, dh), F32),
        grid=(b, g),
        in_specs=[
            pl.BlockSpec((None, None, nc, half), lambda i, j: (i, j, 0, 0)),
            pl.BlockSpec((half, 2 * hid), lambda i, j: (0, 0)),
            pl.BlockSpec((SUBLANES, 2 * half), lambda i, j: (0, 0)),
            pl.BlockSpec((2 * half, hid), lambda i, j: (0, 0)),
            pl.BlockSpec((hid, dh), lambda i, j: (0, 0)),
            pl.BlockSpec((1, dh), lambda i, j: (0, 0)),
        ],
        out_specs=pl.BlockSpec((None, None, nc, dh), lambda i, j: (i, j, 0, 0)),
        compiler_params=_cparams(("parallel", "parallel")),
        name=name,
    )(x, w1c, posr, w1.astype(BF16), w2.astype(BF16), gain.reshape(1, dh).astype(F32))


def _pack_kv_kernel(k_ref, v_ref, gain_ref, kk_ref, v1_ref):
    kp = k_ref[...]
    vp = v_ref[...]
    lane = lax.broadcasted_iota(jnp.int32, kp.shape, 1)
    low = lane < NSA_HEAD_DIM
    sq = kp * kp
    ss_lo = jnp.sum(jnp.where(low, sq, 0.0), axis=-1, keepdims=True)
    ss_hi = jnp.sum(jnp.where(low, 0.0, sq), axis=-1, keepdims=True)
    ms = jnp.where(low, ss_lo, ss_hi) * (1.0 / NSA_HEAD_DIM)
    kn = kp * lax.rsqrt(ms + NORM_EPS) * gain_ref[...]
    kr = pltpu.roll(kn, NSA_HEAD_DIM, 1)
    vr = pltpu.roll(vp, NSA_HEAD_DIM, 1)
    kk_ref[0] = jnp.where(low, kn, kr).astype(kk_ref.dtype)
    kk_ref[1] = jnp.where(low, kr, kn).astype(kk_ref.dtype)
    v1_ref[0] = jnp.where(low, vp, 1.0).astype(v1_ref.dtype)
    v1_ref[1] = jnp.where(low, vr, 1.0).astype(v1_ref.dtype)


def _pack_kv(kvproj, gain, branch, batch, seq, *, tm=512, name="pack_kv"):
    tm = min(tm, seq)
    nt = seq // tm
    pairs = NSA_KV_HEADS // 2
    kcol = branch * 2 * pairs
    vcol = kcol + pairs
    gain2 = jnp.tile(gain.reshape(1, NSA_HEAD_DIM), (1, 2)).astype(F32)
    out = jax.ShapeDtypeStruct((batch, NSA_KV_HEADS, seq, LANES), BF16)
    ospec = pl.BlockSpec((None, 2, tm, LANES), lambda b, t, p: (b, p, t, 0))
    return pl.pallas_call(
        _pack_kv_kernel,
        out_shape=(out, out),
        grid=(batch, nt, pairs),
        in_specs=[
            pl.BlockSpec((tm, LANES), lambda b, t, p: (b * nt + t, kcol + p)),
            pl.BlockSpec((tm, LANES), lambda b, t, p: (b * nt + t, vcol + p)),
            pl.BlockSpec((1, LANES), lambda b, t, p: (0, 0)),
        ],
        out_specs=(ospec, ospec),
        compiler_params=_cparams(("parallel", "parallel", "parallel")),
        name=name,
    )(kvproj, kvproj, gain2)


def _rel_bucket_np(dist):
    max_exact = REL_BUCKETS // 2
    d = np.maximum(dist, 0)
    large = max_exact + (np.log(np.maximum(d, 1).astype(np.float32) / max_exact)
                         / math.log(REL_MAX_DIST / max_exact) * (REL_BUCKETS - max_exact)).astype(np.int32)
    large = np.minimum(large, REL_BUCKETS - 1)
    return np.where(d < max_exact, d, large).astype(np.int32)


def _bias_table(rel_bias, dist, valid):
    onehot = jax.nn.one_hot(_rel_bucket_np(dist), REL_BUCKETS, dtype=F32)
    vals = jnp.einsum("qkb,bh->hqk", onehot, rel_bias.astype(F32), precision=lax.Precision.HIGHEST)
    vals = jnp.where(jnp.asarray(valid)[None], vals, NEG_INF)
    return vals.reshape(NSA_KV_HEADS, NSA_HPG * dist.shape[0], dist.shape[1])


def _head_masks():
    lane = lax.broadcasted_iota(jnp.int32, (Q_BLOCK, LANES), 1)
    return lane < NSA_HEAD_DIM


def _stack_heads(q_ref, low):
    parts = []
    for pair in range(NSA_HPG // 2):
        qp = q_ref[:, pair * LANES:(pair + 1) * LANES].astype(F32)
        parts.append(jnp.where(low, qp, 0.0))
        parts.append(jnp.where(low, 0.0, qp))
    return jnp.concatenate(parts, axis=0).astype(BF16)


def _unstack_heads(x, low):
    parts = []
    for pair in range(NSA_HPG // 2):
        a = x[(2 * pair) * Q_BLOCK:(2 * pair + 1) * Q_BLOCK]
        b = x[(2 * pair + 1) * Q_BLOCK:(2 * pair + 2) * Q_BLOCK]
        parts.append(jnp.where(low, a, pltpu.roll(b, NSA_HEAD_DIM, 1)))
    return jnp.concatenate(parts, axis=1)


def _normalize_pv(pv):
    den = pltpu.roll(pv, NSA_HEAD_DIM, 1)
    return pv * jnp.where(den > 0.0, 1.0 / den, 0.0)


def _nsa_cmp_kernel(q_ref, kk_ref, v1_ref, bias_ref, ov_ref, o_ref, imp_ref, *, nc):
    nsub = q_ref.shape[0] // Q_BLOCK
    low = _head_masks()
    jcol = lax.broadcasted_iota(jnp.int32, (Q_BLOCK, nc), 1)
    subs = range(nsub)
    qbs = [pl.program_id(2) * nsub + u for u in subs]
    starts = [pl.multiple_of(qb * CMP_LOOKBACK, SUBLANES) for qb in qbs]
    s4 = [_dot_nt(_stack_heads(q_ref.at[u * Q_BLOCK:(u + 1) * Q_BLOCK], low),
                  kk_ref[pl.ds(starts[u], nc), :].astype(BF16)) for u in subs]
    probs, psums = [], []
    for u in subs:
        exists = jcol >= (nc - CMP_LOOKBACK - CMP_LOOKBACK * qbs[u])
        psum = jnp.zeros((Q_BLOCK, nc), F32)
        ps = []
        for hp in range(NSA_HPG):
            rows = slice(hp * Q_BLOCK, (hp + 1) * Q_BLOCK)
            s = jnp.where(exists, s4[u][rows] + bias_ref[rows, :], NEG_INF)
            mx = jnp.max(s, axis=-1, keepdims=True)
            p = jnp.exp(s - mx)
            den = jnp.sum(p, axis=-1, keepdims=True)
            p = p * jnp.where(mx > 0.5 * NEG_INF, 1.0 / den, 0.0)
            psum = psum + p
            ps.append(p.astype(BF16))
        probs.append(jnp.concatenate(ps, axis=0))
        psums.append(psum)
    for u in subs:
        pv = _dot(probs[u], v1_ref[pl.ds(starts[u], nc), :].astype(BF16))
        o_ref[u * Q_BLOCK:(u + 1) * Q_BLOCK, :] = _unstack_heads(pv, low).astype(o_ref.dtype)
    for u in subs:
        hi, lo = _split_bf16(psums[u])
        imp_ref[:, u * Q_BLOCK:(u + 1) * Q_BLOCK] = _dot_nt(ov_ref[...], jnp.concatenate([hi, lo], axis=1))


def _nsa_cmp(q, kvc, vkc, bias_c, ov2, batch, seq):
    nc = seq // CMP_STRIDE
    ns = seq // SLC_BLOCK
    nqb = seq // Q_BLOCK
    rows = kvc.shape[2]
    gw = NSA_HPG * NSA_HEAD_DIM
    qrows = CMP_QBLOCKS_PER_STEP * Q_BLOCK
    assert nqb % CMP_QBLOCKS_PER_STEP == 0
    return pl.pallas_call(
        functools.partial(_nsa_cmp_kernel, nc=nc),
        out_shape=(jax.ShapeDtypeStruct((batch, seq, NSA_HEADS * NSA_HEAD_DIM), BF16),
                   jax.ShapeDtypeStruct((batch, NSA_KV_HEADS, ns, seq), F32)),
        grid=(batch, NSA_KV_HEADS, nqb // CMP_QBLOCKS_PER_STEP),
        in_specs=[
            pl.BlockSpec((None, qrows, gw), lambda b, g, i: (b, i, g)),
            pl.BlockSpec((None, None, rows, LANES), lambda b, g, i: (b, g, 0, 0)),
            pl.BlockSpec((None, None, rows, LANES), lambda b, g, i: (b, g, 0, 0)),
            pl.BlockSpec((None, NSA_HPG * Q_BLOCK, nc), lambda b, g, i: (g, 0, 0)),
            pl.BlockSpec((ns, 2 * nc), lambda b, g, i: (0, 0)),
        ],
        out_specs=(pl.BlockSpec((None, qrows, gw), lambda b, g, i: (b, i, g)),
                   pl.BlockSpec((None, None, ns, qrows), lambda b, g, i: (b, g, 0, i))),
        compiler_params=_cparams(("parallel", "parallel", "parallel")),
        name="nsa_compressed",
    )(q, kvc, vkc, bias_c, ov2)


def _topk_kernel(imp_ref, sel_ref, *, ns, ntop):
    toks = imp_ref.shape[1]
    t = pl.program_id(1) * toks + lax.broadcasted_iota(jnp.int32, (ns, toks), 1)
    qb = t >> int(math.log2(Q_BLOCK))
    js = lax.broadcasted_iota(jnp.int32, (ns, toks), 0)
    js_first = ns - 2 - 2 * qb
    js_cur = ns - 2 + ((t & (Q_BLOCK - 1)) >> int(math.log2(SLC_BLOCK)))
    causal = (js >= js_first) & (js <= js_cur)
    forced = (js == js_first) | (js == js_cur) | (js == js_cur - 1)
    n_forced = 1 + (js_cur - 1 >= js_first).astype(jnp.int32) + (js_first < js_cur - 1).astype(jnp.int32)
    n_pick = jnp.minimum(ntop, js_cur - js_first + 1) - n_forced
    ninf = -jnp.inf
    score = jnp.where(causal & jnp.logical_not(forced), imp_ref[...], ninf)
    sel = jnp.where(causal & forced, 1.0, 0.0)
    jsf = js.astype(F32)
    for it in range(ntop - 1):
        mx = jnp.max(score, axis=0, keepdims=True)
        first = jnp.min(jnp.where(score == mx, jsf, float(ns)), axis=0, keepdims=True)
        hit = (jsf == first) & (it < n_pick)
        sel = jnp.where(hit, 1.0, sel)
        score = jnp.where(hit, ninf, score)
    sel_ref[...] = sel.T.astype(sel_ref.dtype)


def _topk_select(imp_t, *, toks=512):
    b, g, ns, t = imp_t.shape
    toks = min(toks, t)
    assert t % toks == 0
    nt = t // toks
    sel = pl.pallas_call(
        functools.partial(_topk_kernel, ns=ns, ntop=min(SLC_TOP, ns)),
        out_shape=jax.ShapeDtypeStruct((b * g * t, ns), BF16),
        grid=(b * g, nt),
        in_specs=[pl.BlockSpec((None, ns, toks), lambda i, j: (i, 0, j))],
        out_specs=pl.BlockSpec((toks, ns), lambda i, j: (i * nt + j, 0)),
        compiler_params=_cparams(("parallel", "parallel")),
        name="nsa_topk",
    )(imp_t.reshape(b * g, ns, t))
    return sel.reshape(b, g, t, ns)


def _nsa_slc_win_kernel(q_ref, sel_ref, kks_ref, v1s_ref, kkw_ref, v1w_ref, bias_ref, bwin_ref,
                        eexp_ref, gates_ref, egate_ref, ocmp_ref, o_ref, m_ref, acc_ref, p_ref, alpha_ref,
                        *, ns, kt_near, nkt):
    qb = pl.program_id(2)
    low = _head_masks()
    kw = SLC_KEY_TILE
    kt0 = (ns - 2 - 2 * qb) // SLC_BLOCKS_PER_TILE

    m_ref[...] = jnp.full_like(m_ref, NEG_INF)
    acc_ref[...] = jnp.zeros_like(acc_ref)
    p_ref[...] = jnp.zeros_like(p_ref)
    alpha_ref[...] = jnp.ones_like(alpha_ref)
    q4 = _stack_heads(q_ref, low)
    selb = sel_ref[...]

    def row_start(kt):
        return pl.multiple_of(jnp.maximum(Q_BLOCK * qb + kw * kt, 0), Q_BLOCK)

    def accumulate(kt):
        pv = _dot(p_ref[...], v1s_ref[pl.ds(row_start(kt), kw), :])
        acc_ref[...] = alpha_ref[...] * acc_ref[...] + pv

    def body(kt):
        s4 = _dot_nt(q4, kks_ref[pl.ds(row_start(kt), kw), :])
        mb = (_dot(selb, eexp_ref[kt]) - 1.0) * (-NEG_INF)
        accumulate(kt - 1)
        bt = jnp.maximum(kt - kt_near + 1, 0)
        for hp in range(NSA_HPG):
            rows = slice(hp * Q_BLOCK, (hp + 1) * Q_BLOCK)
            s = s4[rows] + bias_ref[bt, rows, :] + mb
            m_old = m_ref[rows, :]
            m_new = jnp.maximum(m_old, jnp.max(s, axis=-1, keepdims=True))
            alpha_ref[rows, :] = jnp.exp(m_old - m_new)
            p_ref[rows, :] = jnp.exp(s - m_new[:, 0:1]).astype(BF16)
            m_ref[rows, :] = m_new

    kt_even = (kt0 // 2) * 2

    def body2(j, carry):
        body(kt_even + 2 * j)
        body(kt_even + 2 * j + 1)
        return carry

    lax.fori_loop(0, (nkt - kt_even) // 2, body2, 0)
    accumulate(nkt - 1)
    o_slc = _unstack_heads(_normalize_pv(acc_ref[...]), low)

    wk = WINDOW + Q_BLOCK
    ws = pl.multiple_of(Q_BLOCK * qb, Q_BLOCK)
    sw = _dot_nt(q4, kkw_ref[pl.ds(ws, wk), :])
    wcol = lax.broadcasted_iota(jnp.int32, (Q_BLOCK, wk), 1)
    in_seq = wcol >= (WINDOW - Q_BLOCK * qb)
    ps = []
    for hp in range(NSA_HPG):
        rows = slice(hp * Q_BLOCK, (hp + 1) * Q_BLOCK)
        s = jnp.where(in_seq, sw[rows] + bwin_ref[rows, :], NEG_INF)
        ps.append(jnp.exp(s - jnp.max(s, axis=-1, keepdims=True)).astype(BF16))
    pvw = _dot(jnp.concatenate(ps, axis=0), v1w_ref[pl.ds(ws, wk), :])
    o_win = _unstack_heads(_normalize_pv(pvw), low)

    gh, gl = _split_bf16(gates_ref[...])
    g2 = jnp.concatenate([gh, gl], axis=1)
    o = (_dot(g2, egate_ref[0]) * ocmp_ref[...].astype(F32)
         + _dot(g2, egate_ref[1]) * o_slc
         + _dot(g2, egate_ref[2]) * o_win)
    o_ref[...] = o.astype(o_ref.dtype)


def _nsa_slc_win(q, sel, kvs, vks, kvw, vkw, bias_s, bwin, eexp, gates, egate, ocmp, batch, seq):
    ns = seq // SLC_BLOCK
    nqb = seq // Q_BLOCK
    nkt = (ns * SLC_BLOCK) // SLC_KEY_TILE
    kt_near = nkt - (bias_s.shape[1] - 1)
    gw = NSA_HPG * NSA_HEAD_DIM
    srows = kvs.shape[2]
    wrows = kvw.shape[2]
    slab = lambda r: pl.BlockSpec((None, None, r, LANES), lambda b, g, i: (b, g, 0, 0))
    return pl.pallas_call(
        functools.partial(_nsa_slc_win_kernel, ns=ns, kt_near=kt_near, nkt=nkt),
        out_shape=jax.ShapeDtypeStruct((batch, seq, NSA_HEADS * NSA_HEAD_DIM), BF16),
        grid=(batch, NSA_KV_HEADS, nqb),
        in_specs=[
            pl.BlockSpec((None, Q_BLOCK, gw), lambda b, g, i: (b, i, g)),
            pl.BlockSpec((None, None, Q_BLOCK, ns), lambda b, g, i: (b, g, i, 0)),
            slab(srows), slab(srows), slab(wrows), slab(wrows),
            pl.BlockSpec((None,) + bias_s.shape[1:], lambda b, g, i: (g, 0, 0, 0)),
            pl.BlockSpec((None,) + bwin.shape[1:], lambda b, g, i: (g, 0, 0)),
            pl.BlockSpec(eexp.shape, lambda b, g, i: (0, 0, 0)),
            pl.BlockSpec((None, Q_BLOCK, LANES), lambda b, g, i: (b, i, 0)),
            pl.BlockSpec((N_BRANCH, 2 * LANES, gw), lambda b, g, i: (0, 0, g)),
            pl.BlockSpec((None, Q_BLOCK, gw), lambda b, g, i: (b, i, g)),
        ],
        out_specs=pl.BlockSpec((None, Q_BLOCK, gw), lambda b, g, i: (b, i, g)),
        scratch_shapes=[pltpu.VMEM((NSA_HPG * Q_BLOCK, LANES), F32),
                        pltpu.VMEM((NSA_HPG * Q_BLOCK, LANES), F32),
                        pltpu.VMEM((NSA_HPG * Q_BLOCK, SLC_KEY_TILE), BF16),
                        pltpu.VMEM((NSA_HPG * Q_BLOCK, LANES), F32)],
        compiler_params=_cparams(("parallel", "parallel", "arbitrary")),
        name="nsa_selected_window",
    )(q, sel, kvs, vks, kvw, vkw, bias_s, bwin, eexp, gates, egate, ocmp)


def _nsa_tables(rel_bias, seq):
    nc = seq // CMP_STRIDE
    ns = seq // SLC_BLOCK
    i = np.arange(Q_BLOCK)[:, None]
    j = np.arange(nc)[None, :]
    dist_c = i - (CMP_BLOCK - 1) - CMP_STRIDE * (j - (nc - CMP_LOOKBACK))
    bias_c = _bias_table(rel_bias, dist_c, dist_c >= 0)
    cs = np.arange(nc)[:, None] * CMP_STRIDE
    ss = np.arange(ns)[None, :] * SLC_BLOCK
    ov = np.clip(np.minimum(cs + CMP_BLOCK, ss + SLC_BLOCK) - np.maximum(cs, ss), 0, None) / CMP_BLOCK
    ov2 = jnp.asarray(np.concatenate([ov, ov], axis=0).T, BF16)
    nkeys = ns * SLC_BLOCK
    nkt = nkeys // SLC_KEY_TILE
    pad_rows = (ns - 2) * SLC_BLOCK
    kt_near = max(0, pad_rows - REL_MAX_DIST) // SLC_KEY_TILE
    kr = np.arange(kt_near * SLC_KEY_TILE, nkeys)[None, :]
    dist_s = i + pad_rows - kr
    near = _bias_table(rel_bias, dist_s, dist_s >= 0)
    near = near.reshape(NSA_KV_HEADS, NSA_HPG * Q_BLOCK, nkt - kt_near, SLC_KEY_TILE).transpose(0, 2, 1, 3)
    far = jnp.broadcast_to(rel_bias.astype(F32)[REL_BUCKETS - 1].reshape(NSA_KV_HEADS, 1, NSA_HPG, 1, 1),
                           (NSA_KV_HEADS, 1, NSA_HPG, Q_BLOCK, SLC_KEY_TILE))
    bias_s = jnp.concatenate([far.reshape(NSA_KV_HEADS, 1, NSA_HPG * Q_BLOCK, SLC_KEY_TILE), near], axis=1)
    dist_w = i + WINDOW - np.arange(WINDOW + Q_BLOCK)[None, :]
    bwin = _bias_table(rel_bias, dist_w, (dist_w >= 0) & (dist_w < WINDOW))
    blk = np.arange(ns)[None, :, None]
    key = np.arange(SLC_KEY_TILE)[None, None, :]
    ktile = np.arange(nkt)[:, None, None]
    eexp = jnp.asarray(blk == ktile * SLC_BLOCKS_PER_TILE + key // SLC_BLOCK, BF16)
    col = np.arange(LANES)[:, None]
    head = (np.arange(NSA_HEADS * NSA_HEAD_DIM) // NSA_HEAD_DIM)[None, :]
    eg = np.stack([col == head * N_BRANCH + br for br in range(N_BRANCH)])
    egate = jnp.asarray(np.concatenate([eg, eg], axis=1), BF16)
    return bias_c, ov2, bias_s, bwin, eexp, egate


def _router_kernel(x_ref, g_ref, w_ref, o_ref):
    x = x_ref[...]
    ms = jnp.mean(x * x, axis=-1, keepdims=True)
    xn = x * lax.rsqrt(ms + NORM_EPS) * g_ref[...]
    logits = jnp.dot(xn, w_ref[...], preferred_element_type=F32, precision=lax.Precision.HIGHEST)
    lane = lax.broadcasted_iota(jnp.int32, logits.shape, 1).astype(F32)
    ninf = -jnp.inf
    s = jnp.where(lane < N_EXPERTS, logits, ninf)
    m1 = jnp.max(s, axis=-1, keepdims=True)
    i1 = jnp.min(jnp.where(s == m1, lane, float(LANES)), axis=-1, keepdims=True)
    s2 = jnp.where(lane == i1, ninf, s)
    m2 = jnp.max(s2, axis=-1, keepdims=True)
    i2 = jnp.min(jnp.where(s2 == m2, lane, float(LANES)), axis=-1, keepdims=True)
    e2 = jnp.exp(m2 - m1)
    w1 = 1.0 / (1.0 + e2)
    w2 = e2 * w1
    o_ref[...] = (jnp.where(lane == 0.0, i1, 0.0) + jnp.where(lane == 1.0, i2, 0.0)
                  + jnp.where(lane == 2.0, w1, 0.0) + jnp.where(lane == 3.0, w2, 0.0))


def _router(x, g, w_router, *, tm=512):
    n, d = x.shape
    tm = min(tm, n)
    wpad = jnp.zeros((d, LANES), F32).at[:, :N_EXPERTS].set(w_router.astype(F32))
    return pl.pallas_call(
        _router_kernel,
        out_shape=jax.ShapeDtypeStruct((n, LANES), F32),
        grid=(n // tm,),
        in_specs=[
            pl.BlockSpec((tm, d), lambda i: (i, 0)),
            pl.BlockSpec((1, d), lambda i: (0, 0)),
            pl.BlockSpec((d, LANES), lambda i: (0, 0)),
        ],
        out_specs=pl.BlockSpec((tm, LANES), lambda i: (i, 0)),
        compiler_params=_cparams(("parallel",)),
        name="moe_router",
    )(x, g.reshape(1, d).astype(F32), wpad)


def _moe_dispatch(route, tm):
    n = route.shape[0]
    pairs = 2 * n
    e = route[:, 0:2].astype(jnp.int32).reshape(pairs)
    onehot = (e[:, None] == jnp.arange(N_EXPERTS, dtype=jnp.int32)[None, :]).astype(jnp.int32)
    csum = jnp.cumsum(onehot, axis=0)
    rank = jnp.sum(csum * onehot, axis=1) - 1
    counts = csum[-1]
    cpad = ((counts + tm - 1) // tm) * tm
    gend = jnp.cumsum(cpad)
    dst = jnp.sum(onehot * (gend - cpad)[None, :], axis=1) + rank
    rows = pairs + N_EXPERTS * tm
    ntiles = rows // tm
    nvalid = gend[-1] // tm
    tile = jnp.arange(ntiles, dtype=jnp.int32)
    te = jnp.sum((tile[:, None] * tm >= gend[None, :]).astype(jnp.int32), axis=1)
    te = jnp.where(tile < nvalid, te, te[nvalid - 1])
    src = jnp.zeros((rows,), jnp.int32).at[dst].set(jnp.arange(pairs, dtype=jnp.int32) // 2)
    return src, dst, te.astype(jnp.int32), nvalid.reshape(1).astype(jnp.int32)


def _row_copy(src_hbm, row, dst_ref, j, sem):
    return pltpu.make_async_copy(src_hbm.at[pl.ds(row, 1), :], dst_ref.at[pl.ds(j, 1), :], sem)


def _combine_kernel(i0_ref, i1_ref, y_hbm, h_ref, route_ref, o_ref, buf_ref, sem):
    rows = h_ref.shape[0]

    def issue(g, c):
        for u in range(DMA_ISSUE_UNROLL):
            j = g * DMA_ISSUE_UNROLL + u
            _row_copy(y_hbm, i0_ref[j], buf_ref.at[0], j, sem).start(priority=0)
            _row_copy(y_hbm, i1_ref[j], buf_ref.at[1], j, sem).start(priority=1)
        return c

    lax.fori_loop(0, rows // DMA_ISSUE_UNROLL, issue, 0)
    for k in range(2):
        pltpu.make_async_copy(y_hbm.at[pl.ds(0, rows), :], buf_ref.at[k], sem).wait()
    route = route_ref[...]
    o_ref[...] = h_ref[...] + route[:, 2:3] * buf_ref[0] + route[:, 3:4] * buf_ref[1]


def _combine(h, y, dst, route, *, rows=512):
    n, d = h.shape
    assert n % rows == 0
    dst2 = dst.reshape(n, 2)
    return pl.pallas_call(
        _combine_kernel,
        out_shape=jax.ShapeDtypeStruct((n, d), F32),
        grid=(n // rows,),
        in_specs=[
            pl.BlockSpec((rows,), lambda i: (i,), memory_space=pltpu.SMEM),
            pl.BlockSpec((rows,), lambda i: (i,), memory_space=pltpu.SMEM),
            pl.BlockSpec(memory_space=pl.ANY),
            pl.BlockSpec((rows, d), lambda i: (i, 0)),
            pl.BlockSpec((rows, LANES), lambda i: (i, 0)),
        ],
        out_specs=pl.BlockSpec((rows, d), lambda i: (i, 0)),
        scratch_shapes=[pltpu.VMEM((2, rows, d), F32), pltpu.SemaphoreType.DMA],
        compiler_params=_cparams(("arbitrary",)),
        name="moe_combine",
    )(dst2[:, 0], dst2[:, 1], y, h, route)


def _ffn_grouped_kernel(te_ref, nv_ref, cur_ref, nxt_ref, h_hbm, g_ref, wg_ref, wu_ref, wd_ref, o_ref,
                        xbuf_ref, sem_ref, xn_ref, acc_ref):
    i = pl.program_id(0)
    f = pl.program_id(1)
    tm = xbuf_ref.shape[1]
    used = i < nv_ref[0]
    slot = lax.rem(i, 2)

    def gather(idx_ref, s):
        def issue(q, c):
            for u in range(DMA_ISSUE_UNROLL):
                j = q * DMA_ISSUE_UNROLL + u
                _row_copy(h_hbm, idx_ref[j], xbuf_ref.at[s], j, sem_ref.at[s]).start()
            return c

        lax.fori_loop(0, tm // DMA_ISSUE_UNROLL, issue, 0)

    @pl.when(jnp.logical_and(used, f == 0))
    def _():
        @pl.when(i == 0)
        def _():
            gather(cur_ref, 0)

        @pl.when(i + 1 < nv_ref[0])
        def _():
            gather(nxt_ref, 1 - slot)

        pltpu.make_async_copy(h_hbm.at[pl.ds(0, tm), :], xbuf_ref.at[slot], sem_ref.at[slot]).wait()
        x = xbuf_ref[slot]
        ms = jnp.mean(x * x, axis=-1, keepdims=True)
        xn_ref[...] = (x * lax.rsqrt(ms + NORM_EPS) * g_ref[...]).astype(BF16)
        acc_ref[...] = jnp.zeros_like(acc_ref)

    @pl.when(used)
    def _():
        xn = xn_ref[...]
        gate = _dot(xn, wg_ref[...])
        up = _dot(xn, wu_ref[...])
        act = (gate * jax.nn.sigmoid(gate) * up).astype(BF16)
        acc_ref[...] += _dot(act, wd_ref[...])

    last = f == pl.num_programs(1) - 1

    @pl.when(jnp.logical_and(used, last))
    def _():
        o_ref[...] = acc_ref[...]

    @pl.when(jnp.logical_and(jnp.logical_not(used), last))
    def _():
        o_ref[...] = jnp.zeros_like(o_ref)


def _ffn_grouped(h, src, g, w_gu, w_down, te, nvalid, *, tm, tf=None):
    rows = src.shape[0]
    d = h.shape[1]
    ff = w_down.shape[1]
    if tf is None:
        tf = ff // 2 if (ff // 2) % LANES == 0 else ff
    assert rows % tm == 0 and ff % tf == 0 and tf % LANES == 0 and tm % DMA_ISSUE_UNROLL == 0
    nf = ff // tf
    ntiles = rows // tm

    def fcol(i, f, te_ref, nv_ref):
        return jnp.where(i < nv_ref[0], f, nf - 1)

    grid_spec = pltpu.PrefetchScalarGridSpec(
        num_scalar_prefetch=2,
        grid=(ntiles, nf),
        in_specs=[
            pl.BlockSpec((tm,), lambda i, f, te_ref, nv_ref: (i,), memory_space=pltpu.SMEM),
            pl.BlockSpec((tm,), lambda i, f, te_ref, nv_ref: (jnp.minimum(i + 1, ntiles - 1),),
                         memory_space=pltpu.SMEM),
            pl.BlockSpec(memory_space=pl.ANY),
            pl.BlockSpec((1, d), lambda i, f, te_ref, nv_ref: (0, 0)),
            pl.BlockSpec((None, d, tf), lambda i, f, te_ref, nv_ref: (te_ref[i], 0, fcol(i, f, te_ref, nv_ref))),
            pl.BlockSpec((None, d, tf),
                         lambda i, f, te_ref, nv_ref: (te_ref[i], 0, nf + fcol(i, f, te_ref, nv_ref))),
            pl.BlockSpec((None, tf, d), lambda i, f, te_ref, nv_ref: (te_ref[i], fcol(i, f, te_ref, nv_ref), 0)),
        ],
        out_specs=pl.BlockSpec((tm, d), lambda i, f, te_ref, nv_ref: (i, 0)),
        scratch_shapes=[pltpu.VMEM((2, tm, d), F32), pltpu.SemaphoreType.DMA((2,)),
                        pltpu.VMEM((tm, d), BF16), pltpu.VMEM((tm, d), F32)],
    )
    return pl.pallas_call(
        _ffn_grouped_kernel,
        out_shape=jax.ShapeDtypeStruct((rows, d), F32),
        grid_spec=grid_spec,
        compiler_params=_cparams(("arbitrary", "arbitrary")),
        name="expert_ffn_grouped",
    )(te, nvalid, src, src, h, g.reshape(1, d).astype(F32), w_gu, w_gu, w_down)


def _nsa_layer(h, batch, seq, norm_g, kv_norm_g, w_kv, k_norm_g, cmp_pos_k, cmp_w1_k, cmp_w2_k,
               cmp_pos_v, cmp_w1_v, cmp_w2_v, w_in, w_out, q_norm_g, rel_bias):
    n, d = h.shape
    g_heads, dh = NSA_KV_HEADS, NSA_HEAD_DIM
    nc = seq // CMP_STRIDE
    ns = seq // SLC_BLOCK
    hq = NSA_HEADS * dh

    kvproj = _norm_matmul(h, kv_norm_g, w_kv.astype(BF16), name="nsa_kv_proj")

    gw_kv = g_heads * dh

    def chunks(col0):
        t = kvproj[:, col0:col0 + gw_kv].astype(BF16).reshape(batch, seq, g_heads, dh)
        return t.transpose(0, 2, 1, 3).reshape(batch, g_heads, nc, CMP_STRIDE * dh)

    k_cmp = _compress(chunks(0), cmp_w1_k, cmp_pos_k, cmp_w2_k, k_norm_g[0],
                      apply_norm=True, name="nsa_compress_k")
    v_cmp = _compress(chunks(gw_kv), cmp_w1_v, cmp_pos_v, cmp_w2_v, k_norm_g[0],
                      apply_norm=False, name="nsa_compress_v")
    cpad = nc - CMP_LOOKBACK
    padc = lambda t: jnp.pad(t, ((0, 0), (0, 0), (cpad, 0), (0, 0)))
    kvc = padc(jnp.concatenate([k_cmp, k_cmp], axis=-1))
    vkc = padc(jnp.concatenate([v_cmp, jnp.ones_like(v_cmp)], axis=-1))

    kvs, vks = _pack_kv(kvproj, k_norm_g[1], 1, batch, seq, name="nsa_pack_selected")
    kvw, vkw = _pack_kv(kvproj, k_norm_g[2], 2, batch, seq, name="nsa_pack_window")
    spad = (ns - 2) * SLC_BLOCK
    pads = lambda t, r: jnp.pad(t, ((0, 0), (0, 0), (r, 0), (0, 0)))
    kvs, vks = pads(kvs, spad), pads(vks, spad)
    kvw, vkw = pads(kvw, WINDOW), pads(vkw, WINDOW)

    qgain = jnp.tile(q_norm_g.astype(F32), NSA_HEADS) * (dh ** -0.5)
    q = _norm_matmul(h, norm_g, w_in[:, :hq].astype(BF16), epilogue="headnorm", gain=qgain,
                     flag=jnp.ones((hq,), F32), out_dtype=BF16, name="nsa_q_proj")
    ngate = N_BRANCH * NSA_HEADS
    wg = jnp.zeros((d, LANES), BF16).at[:, :ngate].set(w_in[:, hq:].astype(BF16))
    gates = _norm_matmul(h, norm_g, wg, epilogue="sigmoid", name="nsa_gate_proj")

    bias_c, ov2, bias_s, bwin, eexp, egate = _nsa_tables(rel_bias, seq)
    q3 = q.reshape(batch, seq, hq)
    o_cmp, imp_t = _nsa_cmp(q3, kvc, vkc, bias_c, ov2, batch, seq)
    sel = _topk_select(imp_t)
    o = _nsa_slc_win(q3, sel, kvs, vks, kvw, vkw, bias_s, bwin, eexp,
                     gates.reshape(batch, seq, LANES), egate, o_cmp, batch, seq)
    return _matmul_res(o.reshape(n, hq), w_out.astype(BF16), h, name="nsa_out_proj")


def kernel(x, norm_mix_g, norm_ffn_g, a_w_in, a_w_out, a_onorm_g, lb_param, kv_norm_g, w_kv, k_norm_g,
           cmp_pos_k, cmp_w1_k, cmp_w2_k, cmp_pos_v, cmp_w1_v, cmp_w2_v, b_w_in, b_w_out, b_qnorm_g,
           rel_bias, ffn_w_gu, ffn_w_down, moe_router, moe_w_gu, moe_w_down):
    batch, seq, d = x.shape
    n = batch * seq
    h = x.reshape(n, d).astype(F32)

    lower = jnp.cumsum(jax.nn.softmax(lb_param.astype(F32), axis=0), axis=0)[0]
    proj = _norm_matmul(h, norm_mix_g[0], a_w_in[0].astype(BF16), name="hgrn_in_proj")
    og = _hgrn_mixer(proj, lower, a_onorm_g[0], batch, seq)
    h = _matmul_res(og, a_w_out[0].astype(BF16), h, name="hgrn_out_proj")
    h = _ffn(h, norm_ffn_g[0], ffn_w_gu[0].astype(BF16), ffn_w_down[0].astype(BF16), tm=1024, name="dense_ffn")

    h = _nsa_layer(h, batch, seq, norm_mix_g[1], kv_norm_g, w_kv, k_norm_g, cmp_pos_k, cmp_w1_k, cmp_w2_k,
                   cmp_pos_v, cmp_w1_v, cmp_w2_v, b_w_in[0], b_w_out[0], b_qnorm_g[0], rel_bias)
    route = _router(h, norm_ffn_g[1], moe_router[0])
    src, dst, tile_expert, tiles_used = _moe_dispatch(route, MOE_ROW_TILE)
    ys = _ffn_grouped(h, src, norm_ffn_g[1], moe_w_gu[0].astype(BF16), moe_w_down[0].astype(BF16),
                      tile_expert, tiles_used, tm=MOE_ROW_TILE)
    out = _combine(h, ys, dst, route)
    return out.reshape(batch, seq, d).astype(x.dtype)
```

```python
import functools
import math

import jax
import jax.numpy as jnp
import numpy as np
from jax import lax
from jax.experimental import pallas as pl
from jax.experimental.pallas import tpu as pltpu

F32 = jnp.float32
BF16 = jnp.bfloat16

NORM_EPS = 1e-6
NEG_INF = -1e30
FORCE_BONUS = 1e4
HGRN_HEAD_DIM = 128
NSA_HEADS = 16
NSA_KV_HEADS = 4
NSA_HPG = NSA_HEADS // NSA_KV_HEADS
NSA_HEAD_DIM = 64
N_BRANCH = 3
CMP_BLOCK = 32
CMP_STRIDE = 16
SLC_BLOCK = 64
SLC_TOP = 16
WINDOW = 512
Q_BLOCK = 128
REL_BUCKETS = 32
REL_MAX_DIST = 2048
N_EXPERTS = 8

LANES = 128
SUBLANES = 8
VMEM_LIMIT_BYTES = 56 * 1024 * 1024

HGRN_CHUNK = 128
HGRN_ROWS_PER_STEP = 512
HGRN_HEADS_PER_STEP = 8
SLC_KEY_TILE = 512
SLC_BLOCKS_PER_TILE = SLC_KEY_TILE // SLC_BLOCK
CMP_LOOKBACK = Q_BLOCK // CMP_STRIDE
CMP_QBLOCKS_PER_STEP = 2
MOE_ROW_TILE = 512
DMA_ISSUE_UNROLL = 8


def _cparams(sem):
    return pltpu.CompilerParams(dimension_semantics=sem, vmem_limit_bytes=VMEM_LIMIT_BYTES)


def _dot(a, b):
    return jnp.dot(a, b, preferred_element_type=F32)


def _dot_nt(a, b):
    return lax.dot_general(a, b, (((1,), (1,)), ((), ())), preferred_element_type=F32)


def _split_bf16(x):
    hi = x.astype(BF16)
    lo = (x - hi.astype(F32)).astype(BF16)
    return hi, lo


def _norm_matmul_kernel(x_ref, g_ref, w_ref, *rest, epilogue, tn):
    if epilogue == "headnorm":
        gain_ref, flag_ref, bd_ref, o_ref = rest
    else:
        (o_ref,) = rest
    x = x_ref[...]
    ms = jnp.mean(x * x, axis=-1, keepdims=True)
    xn = (x * lax.rsqrt(ms + NORM_EPS) * g_ref[...]).astype(BF16)
    for c in range(o_ref.shape[1] // tn):
        cols = slice(c * tn, (c + 1) * tn)
        acc = _dot(xn, w_ref[:, cols])
        if epilogue == "headnorm":
            ss = _dot((acc * acc).astype(BF16), bd_ref[...])
            normed = acc * lax.rsqrt(ss * (1.0 / NSA_HEAD_DIM) + NORM_EPS) * gain_ref[:, cols]
            acc = jnp.where(flag_ref[:, cols] > 0.0, normed, acc)
        elif epilogue == "sigmoid":
            acc = jax.nn.sigmoid(acc)
        o_ref[:, cols] = acc.astype(o_ref.dtype)


def _norm_matmul(x, g, w, *, epilogue="none", gain=None, flag=None, out_dtype=F32, tm=512, tn=512,
                 name="norm_matmul"):
    n, d = x.shape
    m = w.shape[1]
    tm = min(tm, n)
    tn = min(tn, m)
    assert n % tm == 0 and m % tn == 0, (n, tm, m, tn)
    const = lambda i: (0, 0)
    in_specs = [pl.BlockSpec((tm, d), lambda i: (i, 0)), pl.BlockSpec((1, d), const), pl.BlockSpec((d, m), const)]
    args = [x, g.reshape(1, d).astype(F32), w]
    if epilogue == "headnorm":
        assert tn % NSA_HEAD_DIM == 0
        grp = np.arange(tn) // NSA_HEAD_DIM
        bd = jnp.asarray(grp[:, None] == grp[None, :], BF16)
        in_specs += [pl.BlockSpec((1, m), const), pl.BlockSpec((1, m), const), pl.BlockSpec((tn, tn), const)]
        args += [gain.reshape(1, m).astype(F32), flag.reshape(1, m).astype(F32), bd]
    return pl.pallas_call(
        functools.partial(_norm_matmul_kernel, epilogue=epilogue, tn=tn),
        out_shape=jax.ShapeDtypeStruct((n, m), out_dtype),
        grid=(n // tm,),
        in_specs=in_specs,
        out_specs=pl.BlockSpec((tm, m), lambda i: (i, 0)),
        compiler_params=_cparams(("parallel",)),
        name=name,
    )(*args)


def _matmul_res_kernel(a_ref, w_ref, r_ref, o_ref, *, tn):
    a = a_ref[...]
    for c in range(o_ref.shape[1] // tn):
        cols = slice(c * tn, (c + 1) * tn)
        o_ref[:, cols] = r_ref[:, cols] + _dot(a, w_ref[:, cols])


def _matmul_res(a, w, res, *, tm=512, tn=512, name="matmul_res"):
    n, k = a.shape
    m = w.shape[1]
    tm = min(tm, n)
    tn = min(tn, m)
    assert n % tm == 0 and m % tn == 0
    return pl.pallas_call(
        functools.partial(_matmul_res_kernel, tn=tn),
        out_shape=jax.ShapeDtypeStruct((n, m), F32),
        grid=(n // tm,),
        in_specs=[
            pl.BlockSpec((tm, k), lambda i: (i, 0)),
            pl.BlockSpec((k, m), lambda i: (0, 0)),
            pl.BlockSpec((tm, m), lambda i: (i, 0)),
        ],
        out_specs=pl.BlockSpec((tm, m), lambda i: (i, 0)),
        compiler_params=_cparams(("parallel",)),
        name=name,
    )(a, w, res)


def _ffn_kernel(x_ref, g_ref, wg_ref, wu_ref, wd_ref, o_ref, xn_ref, acc_ref):
    f = pl.program_id(1)

    @pl.when(f == 0)
    def _():
        x = x_ref[...]
        ms = jnp.mean(x * x, axis=-1, keepdims=True)
        xn_ref[...] = (x * lax.rsqrt(ms + NORM_EPS) * g_ref[...]).astype(BF16)
        acc_ref[...] = jnp.zeros_like(acc_ref)

    xn = xn_ref[...]
    gate = _dot(xn, wg_ref[...])
    up = _dot(xn, wu_ref[...])
    act = (gate * jax.nn.sigmoid(gate) * up).astype(BF16)
    acc_ref[...] += _dot(act, wd_ref[...])

    @pl.when(f == pl.num_programs(1) - 1)
    def _():
        o_ref[...] = x_ref[...] + acc_ref[...]


def _ffn(x, g, w_gu, w_down, *, tm=512, tf=None, name="ffn"):
    n, d = x.shape
    ff = w_down.shape[0]
    if tf is None:
        tf = ff // 2 if (ff // 2) % LANES == 0 else ff
    tm = min(tm, n)
    assert n % tm == 0 and ff % tf == 0 and tf % LANES == 0
    nf = ff // tf
    return pl.pallas_call(
        _ffn_kernel,
        out_shape=jax.ShapeDtypeStruct((n, d), F32),
        grid=(n // tm, nf),
        in_specs=[
            pl.BlockSpec((tm, d), lambda i, f: (i, 0)),
            pl.BlockSpec((1, d), lambda i, f: (0, 0)),
            pl.BlockSpec((d, tf), lambda i, f: (0, f)),
            pl.BlockSpec((d, tf), lambda i, f: (0, nf + f)),
            pl.BlockSpec((tf, d), lambda i, f: (f, 0)),
        ],
        out_specs=pl.BlockSpec((tm, d), lambda i, f: (i, 0)),
        scratch_shapes=[pltpu.VMEM((tm, d), BF16), pltpu.VMEM((tm, d), F32)],
        compiler_params=_cparams(("parallel", "arbitrary")),
        name=name,
    )(x, g.reshape(1, d).astype(F32), w_gu, w_gu, w_down)


def _hgrn_decay_matrix(c):
    levels = int(math.log2(c))
    out = np.zeros(((levels + 2) * c, c), np.float32)
    for l in range(levels):
        m = c >> (l + 1)
        for r in range(c):
            mid = (r // (2 * m)) * 2 * m + m - 1
            if r % (2 * m) >= m:
                out[l * c + r, mid + 1:r + 1] = 1.0
            else:
                out[l * c + r, r + 1:mid + 1] = 1.0
    for r in range(c):
        out[levels * c + r, :r + 1] = 1.0
        out[(levels + 1) * c + r, r + 1:] = 1.0
    return out


def _hgrn_kernel(q_ref, f_ref, v_ref, g_ref, lb_ref, gn_ref, m_ref, o_ref, st_ref, *, chunk, nchunk):
    c = chunk
    dh = HGRN_HEAD_DIM
    levels = int(math.log2(c))

    @pl.when(pl.program_id(2) == 0)
    def _():
        st_ref[...] = jnp.zeros_like(st_ref)

    gn = gn_ref[...]
    row = lax.broadcasted_iota(jnp.int32, (c, c), 0)
    col = lax.broadcasted_iota(jnp.int32, (c, c), 1)
    rowv = lax.broadcasted_iota(jnp.int32, (c, dh), 0)
    nheads = st_ref.shape[0]

    def body(ci, carry):
        sl = pl.ds(pl.multiple_of(ci * c, c), c)
        lb = lb_ref[...]
        fg = lb + (1.0 - lb) * jax.nn.sigmoid(f_ref[sl, :])
        kall = 1.0 - fg
        hi, lo = _split_bf16(jnp.log(fg))
        dall = _dot(m_ref[...], jnp.concatenate([hi, lo], axis=0))
        hs = range(nheads)
        cols = [slice(hh * dh, (hh + 1) * dh) for hh in hs]
        q = [q_ref[sl, cols[hh]] for hh in hs]
        k = [kall[:, cols[hh]] for hh in hs]
        v = [v_ref[sl, cols[hh]] for hh in hs]

        a = [jnp.where(row == col, _dot_nt(q[hh].astype(BF16), k[hh].astype(BF16)), 0.0) for hh in hs]
        for l in range(levels):
            sh = levels - 1 - l
            upper = ((rowv >> sh) & 1) == 1
            same = (row >> (sh + 1)) == (col >> (sh + 1))
            for hh in hs:
                e = jnp.exp(dall[l * c:(l + 1) * c, cols[hh]])
                qe = jnp.where(upper, q[hh] * e, 0.0).astype(BF16)
                ke = jnp.where(upper, 0.0, k[hh] * e).astype(BF16)
                a[hh] = a[hh] + jnp.where(same, _dot_nt(qe, ke), 0.0)
        b = [dall[levels * c:(levels + 1) * c, cols[hh]] for hh in hs]
        st = [st_ref[hh] for hh in hs]
        o = [_dot(a[hh].astype(BF16), v[hh].astype(BF16))
             + _dot_nt((q[hh] * jnp.exp(b[hh])).astype(BF16), st[hh].astype(BF16)) for hh in hs]
        for hh in hs:
            kr = (k[hh] * jnp.exp(dall[(levels + 1) * c:, cols[hh]])).astype(BF16)
            st_ref[hh] = st[hh] * jnp.exp(b[hh][c - 1:c, :]) + _dot(v[hh].T.astype(BF16), kr)
        for hh in hs:
            ms = jnp.mean(o[hh] * o[hh], axis=-1, keepdims=True)
            on = o[hh] * lax.rsqrt(ms + NORM_EPS) * gn
            gg = g_ref[sl, cols[hh]]
            o_ref[sl, cols[hh]] = (on * (gg * jax.nn.sigmoid(gg))).astype(o_ref.dtype)
        return carry

    lax.fori_loop(0, nchunk, body, 0)


def _hgrn_mixer(proj, lb, gn, batch, seq):
    n, d4 = proj.shape
    d = d4 // 4
    heads = d // HGRN_HEAD_DIM
    rows = min(HGRN_ROWS_PER_STEP, seq)
    chunk = min(HGRN_CHUNK, rows)
    assert seq % rows == 0 and rows % chunk == 0
    nt = seq // rows
    m1 = _hgrn_decay_matrix(chunk)
    m = jnp.asarray(np.concatenate([m1, m1], axis=1), BF16)

    hps = HGRN_HEADS_PER_STEP
    assert heads % hps == 0
    groups = heads // hps
    width = hps * HGRN_HEAD_DIM

    def col_spec(s):
        return pl.BlockSpec((rows, width), lambda b, h, t, s=s: (b * nt + t, s * groups + h))

    return pl.pallas_call(
        functools.partial(_hgrn_kernel, chunk=chunk, nchunk=rows // chunk),
        out_shape=jax.ShapeDtypeStruct((n, d), BF16),
        grid=(batch, groups, nt),
        in_specs=[
            col_spec(0), col_spec(1), col_spec(2), col_spec(3),
            pl.BlockSpec((1, width), lambda b, h, t: (0, h)),
            pl.BlockSpec((1, HGRN_HEAD_DIM), lambda b, h, t: (0, 0)),
            pl.BlockSpec(m.shape, lambda b, h, t: (0, 0)),
        ],
        out_specs=pl.BlockSpec((rows, width), lambda b, h, t: (b * nt + t, h)),
        scratch_shapes=[pltpu.VMEM((hps, HGRN_HEAD_DIM, HGRN_HEAD_DIM), F32)],
        compiler_params=_cparams(("parallel", "parallel", "arbitrary")),
        name="hgrn2_recurrence",
    )(proj, proj, proj, proj, lb.reshape(1, d).astype(F32), gn.reshape(1, HGRN_HEAD_DIM).astype(F32), m)


def _compress_kernel(x_ref, w1c_ref, pos_ref, w1_ref, w2_ref, gain_ref, o_ref, *, apply_norm):
    hid = w2_ref.shape[0]
    nc = x_ref.shape[0]
    uv = _dot(x_ref[...], w1c_ref[...])
    posb = _dot(pos_ref[...], w1_ref[...])[0:1, :]
    pre = uv[:, :hid] + pltpu.roll(uv[:, hid:], nc - 1, 0) + posb
    out = _dot(jax.nn.gelu(pre).astype(BF16), w2_ref[...])
    if apply_norm:
        ms = jnp.mean(out * out, axis=-1, keepdims=True)
        out = out * lax.rsqrt(ms + NORM_EPS) * gain_ref[...]
    o_ref[...] = out


def _compress(x, w1, pos, w2, gain, *, apply_norm, name):
    b, g, nc, half = x.shape
    hid = w1.shape[1]
    dh = w2.shape[1]
    w1c = jnp.concatenate([w1[:half], w1[half:]], axis=1).astype(BF16)
    posr = jnp.broadcast_to(pos.reshape(1, -1), (SUBLANES, pos.size)).astype(BF16)
    return pl.pallas_call(
        functools.partial(_compress_kernel, apply_norm=apply_norm),
        out_shape=jax.ShapeDtypeStruct((b, g, nc, dh), F32),
        grid=(b, g),
        in_specs=[
            pl.BlockSpec((None, None, nc, half), lambda i, j: (i, j, 0, 0)),
            pl.BlockSpec((half, 2 * hid), lambda i, j: (0, 0)),
            pl.BlockSpec((SUBLANES, 2 * half), lambda i, j: (0, 0)),
            pl.BlockSpec((2 * half, hid), lambda i, j: (0, 0)),
            pl.BlockSpec((hid, dh), lambda i, j: (0, 0)),
            pl.BlockSpec((1, dh), lambda i, j: (0, 0)),
        ],
        out_specs=pl.BlockSpec((None, None, nc, dh), lambda i, j: (i, j, 0, 0)),
        compiler_params=_cparams(("parallel", "parallel")),
        name=name,
    )(x, w1c, posr, w1.astype(BF16), w2.astype(BF16), gain.reshape(1, dh).astype(F32))


def _pack_kv_kernel(k_ref, v_ref, gain_ref, kk_ref, v1_ref):
    kp = k_ref[...]
    vp = v_ref[...]
    lane = lax.broadcasted_iota(jnp.int32, kp.shape, 1)
    low = lane < NSA_HEAD_DIM
    sq = kp * kp
    ss_lo = jnp.sum(jnp.where(low, sq, 0.0), axis=-1, keepdims=True)
    ss_hi = jnp.sum(jnp.where(low, 0.0, sq), axis=-1, keepdims=True)
    ms = jnp.where(low, ss_lo, ss_hi) * (1.0 / NSA_HEAD_DIM)
    kn = kp * lax.rsqrt(ms + NORM_EPS) * gain_ref[...]
    kr = pltpu.roll(kn, NSA_HEAD_DIM, 1)
    vr = pltpu.roll(vp, NSA_HEAD_DIM, 1)
    kk_ref[0] = jnp.where(low, kn, kr).astype(kk_ref.dtype)
    kk_ref[1] = jnp.where(low, kr, kn).astype(kk_ref.dtype)
    v1_ref[0] = jnp.where(low, vp, 1.0).astype(v1_ref.dtype)
    v1_ref[1] = jnp.where(low, vr, 1.0).astype(v1_ref.dtype)


def _pack_kv(kvproj, gain, branch, batch, seq, *, tm=512, name="pack_kv"):
    tm = min(tm, seq)
    nt = seq // tm
    pairs = NSA_KV_HEADS // 2
    kcol = branch * 2 * pairs
    vcol = kcol + pairs
    gain2 = jnp.tile(gain.reshape(1, NSA_HEAD_DIM), (1, 2)).astype(F32)
    out = jax.ShapeDtypeStruct((batch, NSA_KV_HEADS, seq, LANES), BF16)
    ospec = pl.BlockSpec((None, 2, tm, LANES), lambda b, t, p: (b, p, t, 0))
    return pl.pallas_call(
        _pack_kv_kernel,
        out_shape=(out, out),
        grid=(batch, nt, pairs),
        in_specs=[
            pl.BlockSpec((tm, LANES), lambda b, t, p: (b * nt + t, kcol + p)),
            pl.BlockSpec((tm, LANES), lambda b, t, p: (b * nt + t, vcol + p)),
            pl.BlockSpec((1, LANES), lambda b, t, p: (0, 0)),
        ],
        out_specs=(ospec, ospec),
        compiler_params=_cparams(("parallel", "parallel", "parallel")),
        name=name,
    )(kvproj, kvproj, gain2)


def _rel_bucket_np(dist):
    max_exact = REL_BUCKETS // 2
    d = np.maximum(dist, 0)
    large = max_exact + (np.log(np.maximum(d, 1).astype(np.float32) / max_exact)
                         / math.log(REL_MAX_DIST / max_exact) * (REL_BUCKETS - max_exact)).astype(np.int32)
    large = np.minimum(large, REL_BUCKETS - 1)
    return np.where(d < max_exact, d, large).astype(np.int32)


def _bias_table(rel_bias, dist, valid):
    onehot = jax.nn.one_hot(_rel_bucket_np(dist), REL_BUCKETS, dtype=F32)
    vals = jnp.einsum("qkb,bh->hqk", onehot, rel_bias.astype(F32), precision=lax.Precision.HIGHEST)
    vals = jnp.where(jnp.asarray(valid)[None], vals, NEG_INF)
    return vals.reshape(NSA_KV_HEADS, NSA_HPG * dist.shape[0], dist.shape[1])


def _head_masks():
    lane = lax.broadcasted_iota(jnp.int32, (Q_BLOCK, LANES), 1)
    return lane < NSA_HEAD_DIM


def _stack_heads(q_ref, low):
    parts = []
    for pair in range(NSA_HPG // 2):
        qp = q_ref[:, pair * LANES:(pair + 1) * LANES].astype(F32)
        parts.append(jnp.where(low, qp, 0.0))
        parts.append(jnp.where(low, 0.0, qp))
    return jnp.concatenate(parts, axis=0).astype(BF16)


def _unstack_heads(x, low):
    parts = []
    for pair in range(NSA_HPG // 2):
        a = x[(2 * pair) * Q_BLOCK:(2 * pair + 1) * Q_BLOCK]
        b = x[(2 * pair + 1) * Q_BLOCK:(2 * pair + 2) * Q_BLOCK]
        parts.append(jnp.where(low, a, pltpu.roll(b, NSA_HEAD_DIM, 1)))
    return jnp.concatenate(parts, axis=1)


def _normalize_pv(pv):
    den = pltpu.roll(pv, NSA_HEAD_DIM, 1)
    return pv * jnp.where(den > 0.0, 1.0 / den, 0.0)


def _nsa_cmp_kernel(q_ref, kk_ref, v1_ref, bias_ref, ov_ref, kkw_ref, v1w_ref, bwin_ref, o_ref, imp_ref, ow_ref,
                    *, nc):
    nsub = q_ref.shape[0] // Q_BLOCK
    low = _head_masks()
    jcol = lax.broadcasted_iota(jnp.int32, (Q_BLOCK, nc), 1)
    subs = range(nsub)
    qbs = [pl.program_id(2) * nsub + u for u in subs]
    starts = [pl.multiple_of(qb * CMP_LOOKBACK, SUBLANES) for qb in qbs]
    q4 = [_stack_heads(q_ref.at[u * Q_BLOCK:(u + 1) * Q_BLOCK], low) for u in subs]
    s4 = [_dot_nt(q4[u], kk_ref[pl.ds(starts[u], nc), :].astype(BF16)) for u in subs]
    wk = WINDOW + Q_BLOCK
    wstarts = [pl.multiple_of(Q_BLOCK * qb, Q_BLOCK) for qb in qbs]
    sw = [_dot_nt(q4[u], kkw_ref[pl.ds(wstarts[u], wk), :]) for u in subs]
    wcol = lax.broadcasted_iota(jnp.int32, (Q_BLOCK, wk), 1)
    wprobs = []
    for u in subs:
        in_seq = wcol >= (WINDOW - Q_BLOCK * qbs[u])
        ps = []
        for hp in range(NSA_HPG):
            rows = slice(hp * Q_BLOCK, (hp + 1) * Q_BLOCK)
            s = jnp.where(in_seq, sw[u][rows] + bwin_ref[rows, :], NEG_INF)
            ps.append(jnp.exp(s - jnp.max(s, axis=-1, keepdims=True)).astype(BF16))
        wprobs.append(jnp.concatenate(ps, axis=0))
    probs, psums = [], []
    for u in subs:
        exists = jcol >= (nc - CMP_LOOKBACK - CMP_LOOKBACK * qbs[u])
        psum = jnp.zeros((Q_BLOCK, nc), F32)
        ps = []
        for hp in range(NSA_HPG):
            rows = slice(hp * Q_BLOCK, (hp + 1) * Q_BLOCK)
            s = jnp.where(exists, s4[u][rows] + bias_ref[rows, :], NEG_INF)
            mx = jnp.max(s, axis=-1, keepdims=True)
            p = jnp.exp(s - mx)
            den = jnp.sum(p, axis=-1, keepdims=True)
            p = p * jnp.where(mx > 0.5 * NEG_INF, 1.0 / den, 0.0)
            psum = psum + p
            ps.append(p.astype(BF16))
        probs.append(jnp.concatenate(ps, axis=0))
        psums.append(psum)
    for u in subs:
        pvw = _dot(wprobs[u], v1w_ref[pl.ds(wstarts[u], wk), :])
        ow_ref[u * Q_BLOCK:(u + 1) * Q_BLOCK, :] = _unstack_heads(_normalize_pv(pvw), low).astype(ow_ref.dtype)
    for u in subs:
        pv = _dot(probs[u], v1_ref[pl.ds(starts[u], nc), :].astype(BF16))
        o_ref[u * Q_BLOCK:(u + 1) * Q_BLOCK, :] = _unstack_heads(pv, low).astype(o_ref.dtype)
    for u in subs:
        hi, lo = _split_bf16(psums[u])
        imp_ref[:, u * Q_BLOCK:(u + 1) * Q_BLOCK] = _dot_nt(ov_ref[...], jnp.concatenate([hi, lo], axis=1))


def _nsa_cmp_win(q, kvc, vkc, bias_c, ov2, kvw, vkw, bwin, batch, seq):
    nc = seq // CMP_STRIDE
    ns = seq // SLC_BLOCK
    nqb = seq // Q_BLOCK
    rows = kvc.shape[2]
    wrows = kvw.shape[2]
    gw = NSA_HPG * NSA_HEAD_DIM
    qrows = CMP_QBLOCKS_PER_STEP * Q_BLOCK
    assert nqb % CMP_QBLOCKS_PER_STEP == 0
    slab = lambda r: pl.BlockSpec((None, None, r, LANES), lambda b, g, i: (b, g, 0, 0))
    head_out = jax.ShapeDtypeStruct((batch, seq, NSA_HEADS * NSA_HEAD_DIM), BF16)
    head_spec = pl.BlockSpec((None, qrows, gw), lambda b, g, i: (b, i, g))
    return pl.pallas_call(
        functools.partial(_nsa_cmp_kernel, nc=nc),
        out_shape=(head_out, jax.ShapeDtypeStruct((batch, NSA_KV_HEADS, ns, seq), F32), head_out),
        grid=(batch, NSA_KV_HEADS, nqb // CMP_QBLOCKS_PER_STEP),
        in_specs=[
            head_spec,
            slab(rows), slab(rows),
            pl.BlockSpec((None, NSA_HPG * Q_BLOCK, nc), lambda b, g, i: (g, 0, 0)),
            pl.BlockSpec((ns, 2 * nc), lambda b, g, i: (0, 0)),
            slab(wrows), slab(wrows),
            pl.BlockSpec((None,) + bwin.shape[1:], lambda b, g, i: (g, 0, 0)),
        ],
        out_specs=(head_spec, pl.BlockSpec((None, None, ns, qrows), lambda b, g, i: (b, g, 0, i)), head_spec),
        compiler_params=_cparams(("parallel", "parallel", "parallel")),
        name="nsa_compressed_window",
    )(q, kvc, vkc, bias_c, ov2, kvw, vkw, bwin)


def _topk_kernel(imp_ref, sel_ref, *, ns, ntop):
    toks = imp_ref.shape[1]
    t = pl.program_id(1) * toks + lax.broadcasted_iota(jnp.int32, (ns, toks), 1)
    qb = t >> int(math.log2(Q_BLOCK))
    js = lax.broadcasted_iota(jnp.int32, (ns, toks), 0)
    js_first = ns - 2 - 2 * qb
    js_cur = ns - 2 + ((t & (Q_BLOCK - 1)) >> int(math.log2(SLC_BLOCK)))
    causal = (js >= js_first) & (js <= js_cur)
    forced = (js == js_first) | (js == js_cur) | (js == js_cur - 1)
    n_forced = 1 + (js_cur - 1 >= js_first).astype(jnp.int32) + (js_first < js_cur - 1).astype(jnp.int32)
    n_pick = jnp.minimum(ntop, js_cur - js_first + 1) - n_forced
    ninf = -jnp.inf
    score = jnp.where(causal & jnp.logical_not(forced), imp_ref[...], ninf)
    sel = jnp.where(causal & forced, 1.0, 0.0)
    jsf = js.astype(F32)
    for it in range(ntop - 1):
        mx = jnp.max(score, axis=0, keepdims=True)
        first = jnp.min(jnp.where(score == mx, jsf, float(ns)), axis=0, keepdims=True)
        hit = (jsf == first) & (it < n_pick)
        sel = jnp.where(hit, 1.0, sel)
        score = jnp.where(hit, ninf, score)
    sel_ref[...] = sel.T.astype(sel_ref.dtype)


def _topk_select(imp_t, *, toks=512):
    b, g, ns, t = imp_t.shape
    toks = min(toks, t)
    assert t % toks == 0
    nt = t // toks
    sel = pl.pallas_call(
        functools.partial(_topk_kernel, ns=ns, ntop=min(SLC_TOP, ns)),
        out_shape=jax.ShapeDtypeStruct((b * g * t, ns), BF16),
        grid=(b * g, nt),
        in_specs=[pl.BlockSpec((None, ns, toks), lambda i, j: (i, 0, j))],
        out_specs=pl.BlockSpec((toks, ns), lambda i, j: (i * nt + j, 0)),
        compiler_params=_cparams(("parallel", "parallel")),
        name="nsa_topk",
    )(imp_t.reshape(b * g, ns, t))
    return sel.reshape(b, g, t, ns)


def _nsa_slc_kernel(q_ref, sel_ref, kks_ref, v1s_ref, bias_ref, eexp_ref, gates_ref, egate_ref, ocmp_ref, owin_ref,
                    o_ref, m_ref, acc_ref, p_ref, alpha_ref, *, ns, kt_near, nkt):
    qb = pl.program_id(2)
    low = _head_masks()
    kw = SLC_KEY_TILE
    kt0 = (ns - 2 - 2 * qb) // SLC_BLOCKS_PER_TILE

    m_ref[...] = jnp.full_like(m_ref, NEG_INF)
    acc_ref[...] = jnp.zeros_like(acc_ref)
    p_ref[...] = jnp.zeros_like(p_ref)
    alpha_ref[...] = jnp.ones_like(alpha_ref)
    q4 = _stack_heads(q_ref, low)
    selb = sel_ref[...]

    def row_start(kt):
        return pl.multiple_of(jnp.maximum(Q_BLOCK * qb + kw * kt, 0), Q_BLOCK)

    def accumulate(kt):
        pv = _dot(p_ref[...], v1s_ref[pl.ds(row_start(kt), kw), :])
        acc_ref[...] = alpha_ref[...] * acc_ref[...] + pv

    def body(kt):
        s4 = _dot_nt(q4, kks_ref[pl.ds(row_start(kt), kw), :])
        mb = (_dot(selb, eexp_ref[kt]) - 1.0) * (-NEG_INF)
        accumulate(kt - 1)
        bt = jnp.maximum(kt - kt_near + 1, 0)
        for hp in range(NSA_HPG):
            rows = slice(hp * Q_BLOCK, (hp + 1) * Q_BLOCK)
            s = s4[rows] + bias_ref[bt, rows, :] + mb
            m_old = m_ref[rows, :]
            m_new = jnp.maximum(m_old, jnp.max(s, axis=-1, keepdims=True))
            alpha_ref[rows, :] = jnp.exp(m_old - m_new)
            p_ref[rows, :] = jnp.exp(s - m_new[:, 0:1]).astype(BF16)
            m_ref[rows, :] = m_new

    kt_even = (kt0 // 2) * 2

    def body2(j, carry):
        body(kt_even + 2 * j)
        body(kt_even + 2 * j + 1)
        return carry

    lax.fori_loop(0, (nkt - kt_even) // 2, body2, 0)
    accumulate(nkt - 1)
    o_slc = _unstack_heads(_normalize_pv(acc_ref[...]), low)

    gh, gl = _split_bf16(gates_ref[...])
    g2 = jnp.concatenate([gh, gl], axis=1)
    o = (_dot(g2, egate_ref[0]) * ocmp_ref[...].astype(F32)
         + _dot(g2, egate_ref[1]) * o_slc
         + _dot(g2, egate_ref[2]) * owin_ref[...].astype(F32))
    o_ref[...] = o.astype(o_ref.dtype)


def _nsa_slc(q, sel, kvs, vks, bias_s, eexp, gates, egate, ocmp, owin, batch, seq):
    ns = seq // SLC_BLOCK
    nqb = seq // Q_BLOCK
    nkt = (ns * SLC_BLOCK) // SLC_KEY_TILE
    kt_near = nkt - (bias_s.shape[1] - 1)
    gw = NSA_HPG * NSA_HEAD_DIM
    srows = kvs.shape[2]
    slab = lambda r: pl.BlockSpec((None, None, r, LANES), lambda b, g, i: (b, g, 0, 0))
    head_spec = pl.BlockSpec((None, Q_BLOCK, gw), lambda b, g, i: (b, i, g))
    return pl.pallas_call(
        functools.partial(_nsa_slc_kernel, ns=ns, kt_near=kt_near, nkt=nkt),
        out_shape=jax.ShapeDtypeStruct((batch, seq, NSA_HEADS * NSA_HEAD_DIM), BF16),
        grid=(batch, NSA_KV_HEADS, nqb),
        in_specs=[
            head_spec,
            pl.BlockSpec((None, None, Q_BLOCK, ns), lambda b, g, i: (b, g, i, 0)),
            slab(srows), slab(srows),
            pl.BlockSpec((None,) + bias_s.shape[1:], lambda b, g, i: (g, 0, 0, 0)),
            pl.BlockSpec(eexp.shape, lambda b, g, i: (0, 0, 0)),
            pl.BlockSpec((None, Q_BLOCK, LANES), lambda b, g, i: (b, i, 0)),
            pl.BlockSpec((N_BRANCH, 2 * LANES, gw), lambda b, g, i: (0, 0, g)),
            head_spec, head_spec,
        ],
        out_specs=head_spec,
        scratch_shapes=[pltpu.VMEM((NSA_HPG * Q_BLOCK, LANES), F32),
                        pltpu.VMEM((NSA_HPG * Q_BLOCK, LANES), F32),
                        pltpu.VMEM((NSA_HPG * Q_BLOCK, SLC_KEY_TILE), BF16),
                        pltpu.VMEM((NSA_HPG * Q_BLOCK, LANES), F32)],
        compiler_params=_cparams(("parallel", "parallel", "arbitrary")),
        name="nsa_selected",
    )(q, sel, kvs, vks, bias_s, eexp, gates, egate, ocmp, owin)


def _nsa_tables(rel_bias, seq):
    nc = seq // CMP_STRIDE
    ns = seq // SLC_BLOCK
    i = np.arange(Q_BLOCK)[:, None]
    j = np.arange(nc)[None, :]
    dist_c = i - (CMP_BLOCK - 1) - CMP_STRIDE * (j - (nc - CMP_LOOKBACK))
    bias_c = _bias_table(rel_bias, dist_c, dist_c >= 0)
    cs = np.arange(nc)[:, None] * CMP_STRIDE
    ss = np.arange(ns)[None, :] * SLC_BLOCK
    ov = np.clip(np.minimum(cs + CMP_BLOCK, ss + SLC_BLOCK) - np.maximum(cs, ss), 0, None) / CMP_BLOCK
    ov2 = jnp.asarray(np.concatenate([ov, ov], axis=0).T, BF16)
    nkeys = ns * SLC_BLOCK
    nkt = nkeys // SLC_KEY_TILE
    pad_rows = (ns - 2) * SLC_BLOCK
    kt_near = max(0, pad_rows - REL_MAX_DIST) // SLC_KEY_TILE
    kr = np.arange(kt_near * SLC_KEY_TILE, nkeys)[None, :]
    dist_s = i + pad_rows - kr
    near = _bias_table(rel_bias, dist_s, dist_s >= 0)
    near = near.reshape(NSA_KV_HEADS, NSA_HPG * Q_BLOCK, nkt - kt_near, SLC_KEY_TILE).transpose(0, 2, 1, 3)
    far = jnp.broadcast_to(rel_bias.astype(F32)[REL_BUCKETS - 1].reshape(NSA_KV_HEADS, 1, NSA_HPG, 1, 1),
                           (NSA_KV_HEADS, 1, NSA_HPG, Q_BLOCK, SLC_KEY_TILE))
    bias_s = jnp.concatenate([far.reshape(NSA_KV_HEADS, 1, NSA_HPG * Q_BLOCK, SLC_KEY_TILE), near], axis=1)
    dist_w = i + WINDOW - np.arange(WINDOW + Q_BLOCK)[None, :]
    bwin = _bias_table(rel_bias, dist_w, (dist_w >= 0) & (dist_w < WINDOW))
    blk = np.arange(ns)[None, :, None]
    key = np.arange(SLC_KEY_TILE)[None, None, :]
    ktile = np.arange(nkt)[:, None, None]
    eexp = jnp.asarray(blk == ktile * SLC_BLOCKS_PER_TILE + key // SLC_BLOCK, BF16)
    col = np.arange(LANES)[:, None]
    head = (np.arange(NSA_HEADS * NSA_HEAD_DIM) // NSA_HEAD_DIM)[None, :]
    eg = np.stack([col == head * N_BRANCH + br for br in range(N_BRANCH)])
    egate = jnp.asarray(np.concatenate([eg, eg], axis=1), BF16)
    return bias_c, ov2, bias_s, bwin, eexp, egate


def _router_kernel(x_ref, g_ref, w_ref, o_ref):
    x = x_ref[...]
    ms = jnp.mean(x * x, axis=-1, keepdims=True)
    xn = x * lax.rsqrt(ms + NORM_EPS) * g_ref[...]
    logits = jnp.dot(xn, w_ref[...], preferred_element_type=F32, precision=lax.Precision.HIGHEST)
    lane = lax.broadcasted_iota(jnp.int32, logits.shape, 1).astype(F32)
    ninf = -jnp.inf
    s = jnp.where(lane < N_EXPERTS, logits, ninf)
    m1 = jnp.max(s, axis=-1, keepdims=True)
    i1 = jnp.min(jnp.where(s == m1, lane, float(LANES)), axis=-1, keepdims=True)
    s2 = jnp.where(lane == i1, ninf, s)
    m2 = jnp.max(s2, axis=-1, keepdims=True)
    i2 = jnp.min(jnp.where(s2 == m2, lane, float(LANES)), axis=-1, keepdims=True)
    e2 = jnp.exp(m2 - m1)
    w1 = 1.0 / (1.0 + e2)
    w2 = e2 * w1
    o_ref[...] = (jnp.where(lane == 0.0, i1, 0.0) + jnp.where(lane == 1.0, i2, 0.0)
                  + jnp.where(lane == 2.0, w1, 0.0) + jnp.where(lane == 3.0, w2, 0.0))


def _router(x, g, w_router, *, tm=512):
    n, d = x.shape
    tm = min(tm, n)
    wpad = jnp.zeros((d, LANES), F32).at[:, :N_EXPERTS].set(w_router.astype(F32))
    return pl.pallas_call(
        _router_kernel,
        out_shape=jax.ShapeDtypeStruct((n, LANES), F32),
        grid=(n // tm,),
        in_specs=[
            pl.BlockSpec((tm, d), lambda i: (i, 0)),
            pl.BlockSpec((1, d), lambda i: (0, 0)),
            pl.BlockSpec((d, LANES), lambda i: (0, 0)),
        ],
        out_specs=pl.BlockSpec((tm, LANES), lambda i: (i, 0)),
        compiler_params=_cparams(("parallel",)),
        name="moe_router",
    )(x, g.reshape(1, d).astype(F32), wpad)


def _moe_dispatch(route, tm):
    n = route.shape[0]
    pairs = 2 * n
    e = route[:, 0:2].astype(jnp.int32).reshape(pairs)
    onehot = (e[:, None] == jnp.arange(N_EXPERTS, dtype=jnp.int32)[None, :]).astype(jnp.int32)
    csum = jnp.cumsum(onehot, axis=0)
    rank = jnp.sum(csum * onehot, axis=1) - 1
    counts = csum[-1]
    cpad = ((counts + tm - 1) // tm) * tm
    gend = jnp.cumsum(cpad)
    dst = jnp.sum(onehot * (gend - cpad)[None, :], axis=1) + rank
    rows = pairs + N_EXPERTS * tm
    ntiles = rows // tm
    nvalid = gend[-1] // tm
    tile = jnp.arange(ntiles, dtype=jnp.int32)
    te = jnp.sum((tile[:, None] * tm >= gend[None, :]).astype(jnp.int32), axis=1)
    te = jnp.where(tile < nvalid, te, te[nvalid - 1])
    src = jnp.zeros((rows,), jnp.int32).at[dst].set(jnp.arange(pairs, dtype=jnp.int32) // 2)
    return src, dst, te.astype(jnp.int32), nvalid.reshape(1).astype(jnp.int32)


def _row_copy(src_hbm, row, dst_ref, j, sem):
    return pltpu.make_async_copy(src_hbm.at[pl.ds(row, 1), :], dst_ref.at[pl.ds(j, 1), :], sem)


def _combine_kernel(i0_ref, i1_ref, y_hbm, h_ref, route_ref, o_ref, buf_ref, sem):
    rows = h_ref.shape[0]

    def issue(g, c):
        for u in range(DMA_ISSUE_UNROLL):
            j = g * DMA_ISSUE_UNROLL + u
            _row_copy(y_hbm, i0_ref[j], buf_ref.at[0], j, sem).start(priority=0)
            _row_copy(y_hbm, i1_ref[j], buf_ref.at[1], j, sem).start(priority=1)
        return c

    lax.fori_loop(0, rows // DMA_ISSUE_UNROLL, issue, 0)
    for k in range(2):
        pltpu.make_async_copy(y_hbm.at[pl.ds(0, rows), :], buf_ref.at[k], sem).wait()
    route = route_ref[...]
    o_ref[...] = h_ref[...] + route[:, 2:3] * buf_ref[0] + route[:, 3:4] * buf_ref[1]


def _combine(h, y, dst, route, *, rows=512):
    n, d = h.shape
    assert n % rows == 0
    dst2 = dst.reshape(n, 2)
    return pl.pallas_call(
        _combine_kernel,
        out_shape=jax.ShapeDtypeStruct((n, d), F32),
        grid=(n // rows,),
        in_specs=[
            pl.BlockSpec((rows,), lambda i: (i,), memory_space=pltpu.SMEM),
            pl.BlockSpec((rows,), lambda i: (i,), memory_space=pltpu.SMEM),
            pl.BlockSpec(memory_space=pl.ANY),
            pl.BlockSpec((rows, d), lambda i: (i, 0)),
            pl.BlockSpec((rows, LANES), lambda i: (i, 0)),
        ],
        out_specs=pl.BlockSpec((rows, d), lambda i: (i, 0)),
        scratch_shapes=[pltpu.VMEM((2, rows, d), F32), pltpu.SemaphoreType.DMA],
        compiler_params=_cparams(("arbitrary",)),
        name="moe_combine",
    )(dst2[:, 0], dst2[:, 1], y, h, route)


def _ffn_grouped_kernel(te_ref, nv_ref, cur_ref, nxt_ref, h_hbm, g_ref, wg_ref, wu_ref, wd_ref, o_ref,
                        xbuf_ref, sem_ref, xn_ref, acc_ref, *, nf):
    i = pl.program_id(0)
    f = pl.program_id(1)
    tm = xbuf_ref.shape[1]
    used = i < nv_ref[0]
    slot = lax.rem(i, 2)

    part = tm // nf

    def gather(idx_ref, s, first, count):
        def issue(q, c):
            for u in range(DMA_ISSUE_UNROLL):
                j = first + q * DMA_ISSUE_UNROLL + u
                _row_copy(h_hbm, idx_ref[j], xbuf_ref.at[s], j, sem_ref.at[s]).start()
            return c

        lax.fori_loop(0, count // DMA_ISSUE_UNROLL, issue, 0)

    @pl.when(jnp.logical_and(used, jnp.logical_and(i == 0, f == 0)))
    def _():
        gather(cur_ref, 0, 0, tm)

    @pl.when(i + 1 < nv_ref[0])
    def _():
        gather(nxt_ref, 1 - slot, f * part, part)

    @pl.when(jnp.logical_and(used, f == 0))
    def _():
        pltpu.make_async_copy(h_hbm.at[pl.ds(0, tm), :], xbuf_ref.at[slot], sem_ref.at[slot]).wait()
        x = xbuf_ref[slot]
        ms = jnp.mean(x * x, axis=-1, keepdims=True)
        xn_ref[...] = (x * lax.rsqrt(ms + NORM_EPS) * g_ref[...]).astype(BF16)
        acc_ref[...] = jnp.zeros_like(acc_ref)

    @pl.when(used)
    def _():
        xn = xn_ref[...]
        gate = _dot(xn, wg_ref[...])
        up = _dot(xn, wu_ref[...])
        act = (gate * jax.nn.sigmoid(gate) * up).astype(BF16)
        acc_ref[...] += _dot(act, wd_ref[...])

    last = f == pl.num_programs(1) - 1

    @pl.when(jnp.logical_and(used, last))
    def _():
        o_ref[...] = acc_ref[...]

    @pl.when(jnp.logical_and(jnp.logical_not(used), last))
    def _():
        o_ref[...] = jnp.zeros_like(o_ref)


def _ffn_grouped(h, src, g, w_gu, w_down, te, nvalid, *, tm, tf=None):
    rows = src.shape[0]
    d = h.shape[1]
    ff = w_down.shape[1]
    if tf is None:
        tf = ff // 2 if (ff // 2) % LANES == 0 else ff
    assert rows % tm == 0 and ff % tf == 0 and tf % LANES == 0
    assert tm % ((ff // tf) * DMA_ISSUE_UNROLL) == 0
    nf = ff // tf
    ntiles = rows // tm

    def fcol(i, f, te_ref, nv_ref):
        return jnp.where(i < nv_ref[0], f, nf - 1)

    grid_spec = pltpu.PrefetchScalarGridSpec(
        num_scalar_prefetch=2,
        grid=(ntiles, nf),
        in_specs=[
            pl.BlockSpec((tm,), lambda i, f, te_ref, nv_ref: (i,), memory_space=pltpu.SMEM),
            pl.BlockSpec((tm,), lambda i, f, te_ref, nv_ref: (jnp.minimum(i + 1, ntiles - 1),),
                         memory_space=pltpu.SMEM),
            pl.BlockSpec(memory_space=pl.ANY),
            pl.BlockSpec((1, d), lambda i, f, te_ref, nv_ref: (0, 0)),
            pl.BlockSpec((None, d, tf), lambda i, f, te_ref, nv_ref: (te_ref[i], 0, fcol(i, f, te_ref, nv_ref))),
            pl.BlockSpec((None, d, tf),
                         lambda i, f, te_ref, nv_ref: (te_ref[i], 0, nf + fcol(i, f, te_ref, nv_ref))),
            pl.BlockSpec((None, tf, d), lambda i, f, te_ref, nv_ref: (te_ref[i], fcol(i, f, te_ref, nv_ref), 0)),
        ],
        out_specs=pl.BlockSpec((tm, d), lambda i, f, te_ref, nv_ref: (i, 0)),
        scratch_shapes=[pltpu.VMEM((2, tm, d), F32), pltpu.SemaphoreType.DMA((2,)),
                        pltpu.VMEM((tm, d), BF16), pltpu.VMEM((tm, d), F32)],
    )
    return pl.pallas_call(
        functools.partial(_ffn_grouped_kernel, nf=nf),
        out_shape=jax.ShapeDtypeStruct((rows, d), F32),
        grid_spec=grid_spec,
        compiler_params=_cparams(("arbitrary", "arbitrary")),
        name="expert_ffn_grouped",
    )(te, nvalid, src, src, h, g.reshape(1, d).astype(F32), w_gu, w_gu, w_down)


def _nsa_layer(h, batch, seq, norm_g, kv_norm_g, w_kv, k_norm_g, cmp_pos_k, cmp_w1_k, cmp_w2_k,
               cmp_pos_v, cmp_w1_v, cmp_w2_v, w_in, w_out, q_norm_g, rel_bias):
    n, d = h.shape
    g_heads, dh = NSA_KV_HEADS, NSA_HEAD_DIM
    nc = seq // CMP_STRIDE
    ns = seq // SLC_BLOCK
    hq = NSA_HEADS * dh

    kvproj = _norm_matmul(h, kv_norm_g, w_kv.astype(BF16), name="nsa_kv_proj")

    gw_kv = g_heads * dh

    def chunks(col0):
        t = kvproj[:, col0:col0 + gw_kv].astype(BF16).reshape(batch, seq, g_heads, dh)
        return t.transpose(0, 2, 1, 3).reshape(batch, g_heads, nc, CMP_STRIDE * dh)

    k_cmp = _compress(chunks(0), cmp_w1_k, cmp_pos_k, cmp_w2_k, k_norm_g[0],
                      apply_norm=True, name="nsa_compress_k")
    v_cmp = _compress(chunks(gw_kv), cmp_w1_v, cmp_pos_v, cmp_w2_v, k_norm_g[0],
                      apply_norm=False, name="nsa_compress_v")
    cpad = nc - CMP_LOOKBACK
    padc = lambda t: jnp.pad(t, ((0, 0), (0, 0), (cpad, 0), (0, 0)))
    kvc = padc(jnp.concatenate([k_cmp, k_cmp], axis=-1))
    vkc = padc(jnp.concatenate([v_cmp, jnp.ones_like(v_cmp)], axis=-1))

    kvs, vks = _pack_kv(kvproj, k_norm_g[1], 1, batch, seq, name="nsa_pack_selected")
    kvw, vkw = _pack_kv(kvproj, k_norm_g[2], 2, batch, seq, name="nsa_pack_window")
    spad = (ns - 2) * SLC_BLOCK
    pads = lambda t, r: jnp.pad(t, ((0, 0), (0, 0), (r, 0), (0, 0)))
    kvs, vks = pads(kvs, spad), pads(vks, spad)
    kvw, vkw = pads(kvw, WINDOW), pads(vkw, WINDOW)

    qgain = jnp.tile(q_norm_g.astype(F32), NSA_HEADS) * (dh ** -0.5)
    q = _norm_matmul(h, norm_g, w_in[:, :hq].astype(BF16), epilogue="headnorm", gain=qgain,
                     flag=jnp.ones((hq,), F32), out_dtype=BF16, name="nsa_q_proj")
    ngate = N_BRANCH * NSA_HEADS
    wg = jnp.zeros((d, LANES), BF16).at[:, :ngate].set(w_in[:, hq:].astype(BF16))
    gates = _norm_matmul(h, norm_g, wg, epilogue="sigmoid", name="nsa_gate_proj")

    bias_c, ov2, bias_s, bwin, eexp, egate = _nsa_tables(rel_bias, seq)
    q3 = q.reshape(batch, seq, hq)
    o_cmp, imp_t, o_win = _nsa_cmp_win(q3, kvc, vkc, bias_c, ov2, kvw, vkw, bwin, batch, seq)
    sel = _topk_select(imp_t)
    o = _nsa_slc(q3, sel, kvs, vks, bias_s, eexp, gates.reshape(batch, seq, LANES), egate, o_cmp, o_win,
                 batch, seq)
    return _matmul_res(o.reshape(n, hq), w_out.astype(BF16), h, name="nsa_out_proj")


def kernel(x, norm_mix_g, norm_ffn_g, a_w_in, a_w_out, a_onorm_g, lb_param, kv_norm_g, w_kv, k_norm_g,
           cmp_pos_k, cmp_w1_k, cmp_w2_k, cmp_pos_v, cmp_w1_v, cmp_w2_v, b_w_in, b_w_out, b_qnorm_g,
           rel_bias, ffn_w_gu, ffn_w_down, moe_router, moe_w_gu, moe_w_down):
    batch, seq, d = x.shape
    n = batch * seq
    h = x.reshape(n, d).astype(F32)

    lower = jnp.cumsum(jax.nn.softmax(lb_param.astype(F32), axis=0), axis=0)[0]
    proj = _norm_matmul(h, norm_mix_g[0], a_w_in[0].astype(BF16), name="hgrn_in_proj")
    og = _hgrn_mixer(proj, lower, a_onorm_g[0], batch, seq)
    h = _matmul_res(og, a_w_out[0].astype(BF16), h, name="hgrn_out_proj")
    h = _ffn(h, norm_ffn_g[0], ffn_w_gu[0].astype(BF16), ffn_w_down[0].astype(BF16), tm=1024, name="dense_ffn")

    h = _nsa_layer(h, batch, seq, norm_mix_g[1], kv_norm_g, w_kv, k_norm_g, cmp_pos_k, cmp_w1_k, cmp_w2_k,
                   cmp_pos_v, cmp_w1_v, cmp_w2_v, b_w_in[0], b_w_out[0], b_qnorm_g[0], rel_bias)
    route = _router(h, norm_ffn_g[1], moe_router[0])
    src, dst, tile_expert, tiles_used = _moe_dispatch(route, MOE_ROW_TILE)
    ys = _ffn_grouped(h, src, norm_ffn_g[1], moe_w_gu[0].astype(BF16), moe_w_down[0].astype(BF16),
                      tile_expert, tiles_used, tm=MOE_ROW_TILE)
    out = _combine(h, ys, dst, route)
    return out.reshape(batch, seq, d).astype(x.dtype)
```

```python
import functools
import math

import jax
import jax.numpy as jnp
import numpy as np
from jax import lax
from jax.experimental import pallas as pl
from jax.experimental.pallas import tpu as pltpu

F32 = jnp.float32
BF16 = jnp.bfloat16

NORM_EPS = 1e-6
NEG_INF = -1e30
FORCE_BONUS = 1e4
HGRN_HEAD_DIM = 128
NSA_HEADS = 16
NSA_KV_HEADS = 4
NSA_HPG = NSA_HEADS // NSA_KV_HEADS
NSA_HEAD_DIM = 64
N_BRANCH = 3
CMP_BLOCK = 32
CMP_STRIDE = 16
SLC_BLOCK = 64
SLC_TOP = 16
WINDOW = 512
Q_BLOCK = 128
REL_BUCKETS = 32
REL_MAX_DIST = 2048
N_EXPERTS = 8

LANES = 128
SUBLANES = 8
VMEM_LIMIT_BYTES = 56 * 1024 * 1024

HGRN_CHUNK = 128
HGRN_ROWS_PER_STEP = 512
HGRN_HEADS_PER_STEP = 8
SLC_KEY_TILE = 512
SLC_BLOCKS_PER_TILE = SLC_KEY_TILE // SLC_BLOCK
CMP_LOOKBACK = Q_BLOCK // CMP_STRIDE
CMP_QBLOCKS_PER_STEP = 2
MOE_ROW_TILE = 512
DMA_ISSUE_UNROLL = 8


def _cparams(sem):
    return pltpu.CompilerParams(dimension_semantics=sem, vmem_limit_bytes=VMEM_LIMIT_BYTES)


def _dot(a, b):
    return jnp.dot(a, b, preferred_element_type=F32)


def _dot_nt(a, b):
    return lax.dot_general(a, b, (((1,), (1,)), ((), ())), preferred_element_type=F32)


def _split_bf16(x):
    hi = x.astype(BF16)
    lo = (x - hi.astype(F32)).astype(BF16)
    return hi, lo


def _norm_matmul_kernel(x_ref, g_ref, w_ref, *rest, epilogue, tn):
    if epilogue == "headnorm":
        gain_ref, flag_ref, bd_ref, o_ref = rest
    else:
        (o_ref,) = rest
    x = x_ref[...]
    ms = jnp.mean(x * x, axis=-1, keepdims=True)
    xn = (x * lax.rsqrt(ms + NORM_EPS) * g_ref[...]).astype(BF16)
    for c in range(o_ref.shape[1] // tn):
        cols = slice(c * tn, (c + 1) * tn)
        acc = _dot(xn, w_ref[:, cols])
        if epilogue == "headnorm":
            ss = _dot((acc * acc).astype(BF16), bd_ref[...])
            normed = acc * lax.rsqrt(ss * (1.0 / NSA_HEAD_DIM) + NORM_EPS) * gain_ref[:, cols]
            acc = jnp.where(flag_ref[:, cols] > 0.0, normed, acc)
        elif epilogue == "sigmoid":
            acc = jax.nn.sigmoid(acc)
        o_ref[:, cols] = acc.astype(o_ref.dtype)


def _norm_matmul(x, g, w, *, epilogue="none", gain=None, flag=None, out_dtype=F32, tm=512, tn=512,
                 name="norm_matmul"):
    n, d = x.shape
    m = w.shape[1]
    tm = min(tm, n)
    tn = min(tn, m)
    assert n % tm == 0 and m % tn == 0, (n, tm, m, tn)
    const = lambda i: (0, 0)
    in_specs = [pl.BlockSpec((tm, d), lambda i: (i, 0)), pl.BlockSpec((1, d), const), pl.BlockSpec((d, m), const)]
    args = [x, g.reshape(1, d).astype(F32), w]
    if epilogue == "headnorm":
        assert tn % NSA_HEAD_DIM == 0
        grp = np.arange(tn) // NSA_HEAD_DIM
        bd = jnp.asarray(grp[:, None] == grp[None, :], BF16)
        in_specs += [pl.BlockSpec((1, m), const), pl.BlockSpec((1, m), const), pl.BlockSpec((tn, tn), const)]
        args += [gain.reshape(1, m).astype(F32), flag.reshape(1, m).astype(F32), bd]
    return pl.pallas_call(
        functools.partial(_norm_matmul_kernel, epilogue=epilogue, tn=tn),
        out_shape=jax.ShapeDtypeStruct((n, m), out_dtype),
        grid=(n // tm,),
        in_specs=in_specs,
        out_specs=pl.BlockSpec((tm, m), lambda i: (i, 0)),
        compiler_params=_cparams(("parallel",)),
        name=name,
    )(*args)


def _matmul_res_kernel(a_ref, w_ref, r_ref, o_ref, *, tn):
    a = a_ref[...]
    for c in range(o_ref.shape[1] // tn):
        cols = slice(c * tn, (c + 1) * tn)
        o_ref[:, cols] = r_ref[:, cols] + _dot(a, w_ref[:, cols])


def _matmul_res(a, w, res, *, tm=512, tn=512, name="matmul_res"):
    n, k = a.shape
    m = w.shape[1]
    tm = min(tm, n)
    tn = min(tn, m)
    assert n % tm == 0 and m % tn == 0
    return pl.pallas_call(
        functools.partial(_matmul_res_kernel, tn=tn),
        out_shape=jax.ShapeDtypeStruct((n, m), F32),
        grid=(n // tm,),
        in_specs=[
            pl.BlockSpec((tm, k), lambda i: (i, 0)),
            pl.BlockSpec((k, m), lambda i: (0, 0)),
            pl.BlockSpec((tm, m), lambda i: (i, 0)),
        ],
        out_specs=pl.BlockSpec((tm, m), lambda i: (i, 0)),
        compiler_params=_cparams(("parallel",)),
        name=name,
    )(a, w, res)


def _ffn_kernel(x_ref, g_ref, wg_ref, wu_ref, wd_ref, o_ref, xn_ref, acc_ref):
    f = pl.program_id(1)

    @pl.when(f == 0)
    def _():
        x = x_ref[...]
        ms = jnp.mean(x * x, axis=-1, keepdims=True)
        xn_ref[...] = (x * lax.rsqrt(ms + NORM_EPS) * g_ref[...]).astype(BF16)
        acc_ref[...] = jnp.zeros_like(acc_ref)

    xn = xn_ref[...]
    gate = _dot(xn, wg_ref[...])
    up = _dot(xn, wu_ref[...])
    act = (gate * jax.nn.sigmoid(gate) * up).astype(BF16)
    acc_ref[...] += _dot(act, wd_ref[...])

    @pl.when(f == pl.num_programs(1) - 1)
    def _():
        o_ref[...] = x_ref[...] + acc_ref[...]


def _ffn(x, g, w_gu, w_down, *, tm=512, tf=None, name="ffn"):
    n, d = x.shape
    ff = w_down.shape[0]
    if tf is None:
        tf = ff // 2 if (ff // 2) % LANES == 0 else ff
    tm = min(tm, n)
    assert n % tm == 0 and ff % tf == 0 and tf % LANES == 0
    nf = ff // tf
    return pl.pallas_call(
        _ffn_kernel,
        out_shape=jax.ShapeDtypeStruct((n, d), F32),
        grid=(n // tm, nf),
        in_specs=[
            pl.BlockSpec((tm, d), lambda i, f: (i, 0)),
            pl.BlockSpec((1, d), lambda i, f: (0, 0)),
            pl.BlockSpec((d, tf), lambda i, f: (0, f)),
            pl.BlockSpec((d, tf), lambda i, f: (0, nf + f)),
            pl.BlockSpec((tf, d), lambda i, f: (f, 0)),
        ],
        out_specs=pl.BlockSpec((tm, d), lambda i, f: (i, 0)),
        scratch_shapes=[pltpu.VMEM((tm, d), BF16), pltpu.VMEM((tm, d), F32)],
        compiler_params=_cparams(("parallel", "arbitrary")),
        name=name,
    )(x, g.reshape(1, d).astype(F32), w_gu, w_gu, w_down)


def _hgrn_decay_matrix(c):
    levels = int(math.log2(c))
    out = np.zeros(((levels + 2) * c, c), np.float32)
    for l in range(levels):
        m = c >> (l + 1)
        for r in range(c):
            mid = (r // (2 * m)) * 2 * m + m - 1
            if r % (2 * m) >= m:
                out[l * c + r, mid + 1:r + 1] = 1.0
            else:
                out[l * c + r, r + 1:mid + 1] = 1.0
    for r in range(c):
        out[levels * c + r, :r + 1] = 1.0
        out[(levels + 1) * c + r, r + 1:] = 1.0
    return out


def _hgrn_kernel(q_ref, f_ref, v_ref, g_ref, lb_ref, gn_ref, m_ref, o_ref, st_ref, *, chunk, nchunk):
    c = chunk
    dh = HGRN_HEAD_DIM
    levels = int(math.log2(c))

    @pl.when(pl.program_id(2) == 0)
    def _():
        st_ref[...] = jnp.zeros_like(st_ref)

    gn = gn_ref[...]
    row = lax.broadcasted_iota(jnp.int32, (c, c), 0)
    col = lax.broadcasted_iota(jnp.int32, (c, c), 1)
    rowv = lax.broadcasted_iota(jnp.int32, (c, dh), 0)
    nheads = st_ref.shape[0]

    def body(ci, carry):
        sl = pl.ds(pl.multiple_of(ci * c, c), c)
        lb = lb_ref[...]
        fg = lb + (1.0 - lb) * jax.nn.sigmoid(f_ref[sl, :])
        kall = 1.0 - fg
        hi, lo = _split_bf16(jnp.log(fg))
        dall = _dot(m_ref[...], jnp.concatenate([hi, lo], axis=0))
        hs = range(nheads)
        cols = [slice(hh * dh, (hh + 1) * dh) for hh in hs]
        q = [q_ref[sl, cols[hh]] for hh in hs]
        k = [kall[:, cols[hh]] for hh in hs]
        v = [v_ref[sl, cols[hh]] for hh in hs]

        a = [jnp.where(row == col, _dot_nt(q[hh].astype(BF16), k[hh].astype(BF16)), 0.0) for hh in hs]
        for l in range(levels):
            sh = levels - 1 - l
            upper = ((rowv >> sh) & 1) == 1
            same = (row >> (sh + 1)) == (col >> (sh + 1))
            for hh in hs:
                e = jnp.exp(dall[l * c:(l + 1) * c, cols[hh]])
                qe = jnp.where(upper, q[hh] * e, 0.0).astype(BF16)
                ke = jnp.where(upper, 0.0, k[hh] * e).astype(BF16)
                a[hh] = a[hh] + jnp.where(same, _dot_nt(qe, ke), 0.0)
        b = [dall[levels * c:(levels + 1) * c, cols[hh]] for hh in hs]
        st = [st_ref[hh] for hh in hs]
        o = [_dot(a[hh].astype(BF16), v[hh].astype(BF16))
             + _dot_nt((q[hh] * jnp.exp(b[hh])).astype(BF16), st[hh].astype(BF16)) for hh in hs]
        for hh in hs:
            kr = (k[hh] * jnp.exp(dall[(levels + 1) * c:, cols[hh]])).astype(BF16)
            st_ref[hh] = st[hh] * jnp.exp(b[hh][c - 1:c, :]) + _dot(v[hh].T.astype(BF16), kr)
        for hh in hs:
            ms = jnp.mean(o[hh] * o[hh], axis=-1, keepdims=True)
            on = o[hh] * lax.rsqrt(ms + NORM_EPS) * gn
            gg = g_ref[sl, cols[hh]]
            o_ref[sl, cols[hh]] = (on * (gg * jax.nn.sigmoid(gg))).astype(o_ref.dtype)
        return carry

    lax.fori_loop(0, nchunk, body, 0)


def _hgrn_mixer(proj, lb, gn, batch, seq):
    n, d4 = proj.shape
    d = d4 // 4
    heads = d // HGRN_HEAD_DIM
    rows = min(HGRN_ROWS_PER_STEP, seq)
    chunk = min(HGRN_CHUNK, rows)
    assert seq % rows == 0 and rows % chunk == 0
    nt = seq // rows
    m1 = _hgrn_decay_matrix(chunk)
    m = jnp.asarray(np.concatenate([m1, m1], axis=1), BF16)

    hps = HGRN_HEADS_PER_STEP
    assert heads % hps == 0
    groups = heads // hps
    width = hps * HGRN_HEAD_DIM

    def col_spec(s):
        return pl.BlockSpec((rows, width), lambda b, h, t, s=s: (b * nt + t, s * groups + h))

    return pl.pallas_call(
        functools.partial(_hgrn_kernel, chunk=chunk, nchunk=rows // chunk),
        out_shape=jax.ShapeDtypeStruct((n, d), BF16),
        grid=(batch, groups, nt),
        in_specs=[
            col_spec(0), col_spec(1), col_spec(2), col_spec(3),
            pl.BlockSpec((1, width), lambda b, h, t: (0, h)),
            pl.BlockSpec((1, HGRN_HEAD_DIM), lambda b, h, t: (0, 0)),
            pl.BlockSpec(m.shape, lambda b, h, t: (0, 0)),
        ],
        out_specs=pl.BlockSpec((rows, width), lambda b, h, t: (b * nt + t, h)),
        scratch_shapes=[pltpu.VMEM((hps, HGRN_HEAD_DIM, HGRN_HEAD_DIM), F32)],
        compiler_params=_cparams(("parallel", "parallel", "arbitrary")),
        name="hgrn2_recurrence",
    )(proj, proj, proj, proj, lb.reshape(1, d).astype(F32), gn.reshape(1, HGRN_HEAD_DIM).astype(F32), m)


def _compress_kernel(x_ref, w1c_ref, pos_ref, w1_ref, w2_ref, gain_ref, o_ref, *, apply_norm):
    hid = w2_ref.shape[0]
    nc = x_ref.shape[0]
    uv = _dot(x_ref[...], w1c_ref[...])
    posb = _dot(pos_ref[...], w1_ref[...])[0:1, :]
    pre = uv[:, :hid] + pltpu.roll(uv[:, hid:], nc - 1, 0) + posb
    out = _dot(jax.nn.gelu(pre).astype(BF16), w2_ref[...])
    if apply_norm:
        ms = jnp.mean(out * out, axis=-1, keepdims=True)
        out = out * lax.rsqrt(ms + NORM_EPS) * gain_ref[...]
    o_ref[...] = out


def _compress(x, w1, pos, w2, gain, *, apply_norm, name):
    b, g, nc, half = x.shape
    hid = w1.shape[1]
    dh = w2.shape[1]
    w1c = jnp.concatenate([w1[:half], w1[half:]], axis=1).astype(BF16)
    posr = jnp.broadcast_to(pos.reshape(1, -1), (SUBLANES, pos.size)).astype(BF16)
    return pl.pallas_call(
        functools.partial(_compress_kernel, apply_norm=apply_norm),
        out_shape=jax.ShapeDtypeStruct((b, g, nc, dh), F32),
        grid=(b, g),
        in_specs=[
            pl.BlockSpec((None, None, nc, half), lambda i, j: (i, j, 0, 0)),
            pl.BlockSpec((half, 2 * hid), lambda i, j: (0, 0)),
            pl.BlockSpec((SUBLANES, 2 * half), lambda i, j: (0, 0)),
            pl.BlockSpec((2 * half, hid), lambda i, j: (0, 0)),
            pl.BlockSpec((hid, dh), lambda i, j: (0, 0)),
            pl.BlockSpec((1, dh), lambda i, j: (0, 0)),
        ],
        out_specs=pl.BlockSpec((None, None, nc, dh), lambda i, j: (i, j, 0, 0)),
        compiler_params=_cparams(("parallel", "parallel")),
        name=name,
    )(x, w1c, posr, w1.astype(BF16), w2.astype(BF16), gain.reshape(1, dh).astype(F32))


def _pack_kv_kernel(k_ref, v_ref, gain_ref, kk_ref, v1_ref):
    kp = k_ref[...]
    vp = v_ref[...]
    lane = lax.broadcasted_iota(jnp.int32, kp.shape, 1)
    low = lane < NSA_HEAD_DIM
    sq = kp * kp
    ss_lo = jnp.sum(jnp.where(low, sq, 0.0), axis=-1, keepdims=True)
    ss_hi = jnp.sum(jnp.where(low, 0.0, sq), axis=-1, keepdims=True)
    ms = jnp.where(low, ss_lo, ss_hi) * (1.0 / NSA_HEAD_DIM)
    kn = kp * lax.rsqrt(ms + NORM_EPS) * gain_ref[...]
    kr = pltpu.roll(kn, NSA_HEAD_DIM, 1)
    vr = pltpu.roll(vp, NSA_HEAD_DIM, 1)
    kk_ref[0] = jnp.where(low, kn, kr).astype(kk_ref.dtype)
    kk_ref[1] = jnp.where(low, kr, kn).astype(kk_ref.dtype)
    v1_ref[0] = jnp.where(low, vp, 1.0).astype(v1_ref.dtype)
    v1_ref[1] = jnp.where(low, vr, 1.0).astype(v1_ref.dtype)


def _pack_kv(kvproj, gain, branch, batch, seq, *, tm=512, name="pack_kv"):
    tm = min(tm, seq)
    nt = seq // tm
    pairs = NSA_KV_HEADS // 2
    kcol = branch * 2 * pairs
    vcol = kcol + pairs
    gain2 = jnp.tile(gain.reshape(1, NSA_HEAD_DIM), (1, 2)).astype(F32)
    out = jax.ShapeDtypeStruct((batch, NSA_KV_HEADS, seq, LANES), BF16)
    ospec = pl.BlockSpec((None, 2, tm, LANES), lambda b, t, p: (b, p, t, 0))
    return pl.pallas_call(
        _pack_kv_kernel,
        out_shape=(out, out),
        grid=(batch, nt, pairs),
        in_specs=[
            pl.BlockSpec((tm, LANES), lambda b, t, p: (b * nt + t, kcol + p)),
            pl.BlockSpec((tm, LANES), lambda b, t, p: (b * nt + t, vcol + p)),
            pl.BlockSpec((1, LANES), lambda b, t, p: (0, 0)),
        ],
        out_specs=(ospec, ospec),
        compiler_params=_cparams(("parallel", "parallel", "parallel")),
        name=name,
    )(kvproj, kvproj, gain2)


def _rel_bucket_np(dist):
    max_exact = REL_BUCKETS // 2
    d = np.maximum(dist, 0)
    large = max_exact + (np.log(np.maximum(d, 1).astype(np.float32) / max_exact)
                         / math.log(REL_MAX_DIST / max_exact) * (REL_BUCKETS - max_exact)).astype(np.int32)
    large = np.minimum(large, REL_BUCKETS - 1)
    return np.where(d < max_exact, d, large).astype(np.int32)


def _bias_table(rel_bias, dist, valid):
    onehot = jax.nn.one_hot(_rel_bucket_np(dist), REL_BUCKETS, dtype=F32)
    vals = jnp.einsum("qkb,bh->hqk", onehot, rel_bias.astype(F32), precision=lax.Precision.HIGHEST)
    vals = jnp.where(jnp.asarray(valid)[None], vals, NEG_INF)
    return vals.reshape(NSA_KV_HEADS, NSA_HPG * dist.shape[0], dist.shape[1])


def _head_masks():
    lane = lax.broadcasted_iota(jnp.int32, (Q_BLOCK, LANES), 1)
    return lane < NSA_HEAD_DIM


def _stack_heads(q_ref, low):
    parts = []
    for pair in range(NSA_HPG // 2):
        qp = q_ref[:, pair * LANES:(pair + 1) * LANES].astype(F32)
        parts.append(jnp.where(low, qp, 0.0))
        parts.append(jnp.where(low, 0.0, qp))
    return jnp.concatenate(parts, axis=0).astype(BF16)


def _unstack_heads(x, low):
    parts = []
    for pair in range(NSA_HPG // 2):
        a = x[(2 * pair) * Q_BLOCK:(2 * pair + 1) * Q_BLOCK]
        b = x[(2 * pair + 1) * Q_BLOCK:(2 * pair + 2) * Q_BLOCK]
        parts.append(jnp.where(low, a, pltpu.roll(b, NSA_HEAD_DIM, 1)))
    return jnp.concatenate(parts, axis=1)


def _normalize_pv(pv):
    den = pltpu.roll(pv, NSA_HEAD_DIM, 1)
    return pv * jnp.where(den > 0.0, 1.0 / den, 0.0)


def _nsa_cmp_kernel(q_ref, kk_ref, v1_ref, bias_ref, ov_ref, kkw_ref, v1w_ref, bwin_ref, o_ref, imp_ref, ow_ref,
                    *, nc):
    nsub = q_ref.shape[0] // Q_BLOCK
    low = _head_masks()
    jcol = lax.broadcasted_iota(jnp.int32, (Q_BLOCK, nc), 1)
    subs = range(nsub)
    qbs = [pl.program_id(2) * nsub + u for u in subs]
    starts = [pl.multiple_of(qb * CMP_LOOKBACK, SUBLANES) for qb in qbs]
    q4 = [_stack_heads(q_ref.at[u * Q_BLOCK:(u + 1) * Q_BLOCK], low) for u in subs]
    s4 = [_dot_nt(q4[u], kk_ref[pl.ds(starts[u], nc), :].astype(BF16)) for u in subs]
    wk = WINDOW + Q_BLOCK
    wstarts = [pl.multiple_of(Q_BLOCK * qb, Q_BLOCK) for qb in qbs]
    sw = [_dot_nt(q4[u], kkw_ref[pl.ds(wstarts[u], wk), :]) for u in subs]
    wcol = lax.broadcasted_iota(jnp.int32, (Q_BLOCK, wk), 1)
    wprobs = []
    for u in subs:
        in_seq = wcol >= (WINDOW - Q_BLOCK * qbs[u])
        ps = []
        for hp in range(NSA_HPG):
            rows = slice(hp * Q_BLOCK, (hp + 1) * Q_BLOCK)
            s = jnp.where(in_seq, sw[u][rows] + bwin_ref[rows, :], NEG_INF)
            ps.append(jnp.exp(s - jnp.max(s, axis=-1, keepdims=True)).astype(BF16))
        wprobs.append(jnp.concatenate(ps, axis=0))
    probs, psums = [], []
    for u in subs:
        exists = jcol >= (nc - CMP_LOOKBACK - CMP_LOOKBACK * qbs[u])
        psum = jnp.zeros((Q_BLOCK, nc), F32)
        ps = []
        for hp in range(NSA_HPG):
            rows = slice(hp * Q_BLOCK, (hp + 1) * Q_BLOCK)
            s = jnp.where(exists, s4[u][rows] + bias_ref[rows, :], NEG_INF)
            mx = jnp.max(s, axis=-1, keepdims=True)
            p = jnp.exp(s - mx)
            den = jnp.sum(p, axis=-1, keepdims=True)
            p = p * jnp.where(mx > 0.5 * NEG_INF, 1.0 / den, 0.0)
            psum = psum + p
            ps.append(p.astype(BF16))
        probs.append(jnp.concatenate(ps, axis=0))
        psums.append(psum)
    for u in subs:
        pvw = _dot(wprobs[u], v1w_ref[pl.ds(wstarts[u], wk), :])
        ow_ref[u * Q_BLOCK:(u + 1) * Q_BLOCK, :] = _unstack_heads(_normalize_pv(pvw), low).astype(ow_ref.dtype)
    for u in subs:
        pv = _dot(probs[u], v1_ref[pl.ds(starts[u], nc), :].astype(BF16))
        o_ref[u * Q_BLOCK:(u + 1) * Q_BLOCK, :] = _unstack_heads(pv, low).astype(o_ref.dtype)
    for u in subs:
        hi, lo = _split_bf16(psums[u])
        imp_ref[:, u * Q_BLOCK:(u + 1) * Q_BLOCK] = _dot_nt(ov_ref[...], jnp.concatenate([hi, lo], axis=1))


def _nsa_cmp_win(q, kvc, vkc, bias_c, ov2, kvw, vkw, bwin, batch, seq):
    nc = seq // CMP_STRIDE
    ns = seq // SLC_BLOCK
    nqb = seq // Q_BLOCK
    rows = kvc.shape[2]
    wrows = kvw.shape[2]
    gw = NSA_HPG * NSA_HEAD_DIM
    qrows = CMP_QBLOCKS_PER_STEP * Q_BLOCK
    assert nqb % CMP_QBLOCKS_PER_STEP == 0
    slab = lambda r: pl.BlockSpec((None, None, r, LANES), lambda b, g, i: (b, g, 0, 0))
    head_out = jax.ShapeDtypeStruct((batch, seq, NSA_HEADS * NSA_HEAD_DIM), BF16)
    head_spec = pl.BlockSpec((None, qrows, gw), lambda b, g, i: (b, i, g))
    return pl.pallas_call(
        functools.partial(_nsa_cmp_kernel, nc=nc),
        out_shape=(head_out, jax.ShapeDtypeStruct((batch, NSA_KV_HEADS, ns, seq), F32), head_out),
        grid=(batch, NSA_KV_HEADS, nqb // CMP_QBLOCKS_PER_STEP),
        in_specs=[
            head_spec,
            slab(rows), slab(rows),
            pl.BlockSpec((None, NSA_HPG * Q_BLOCK, nc), lambda b, g, i: (g, 0, 0)),
            pl.BlockSpec((ns, 2 * nc), lambda b, g, i: (0, 0)),
            slab(wrows), slab(wrows),
            pl.BlockSpec((None,) + bwin.shape[1:], lambda b, g, i: (g, 0, 0)),
        ],
        out_specs=(head_spec, pl.BlockSpec((None, None, ns, qrows), lambda b, g, i: (b, g, 0, i)), head_spec),
        compiler_params=_cparams(("parallel", "parallel", "parallel")),
        name="nsa_compressed_window",
    )(q, kvc, vkc, bias_c, ov2, kvw, vkw, bwin)


def _topk_kernel(imp_ref, sel_ref, *, ns, ntop):
    toks = imp_ref.shape[1]
    t = pl.program_id(1) * toks + lax.broadcasted_iota(jnp.int32, (1, toks), 1)
    qb = t >> int(math.log2(Q_BLOCK))
    js = lax.broadcasted_iota(jnp.int32, (ns, toks), 0)
    js_first = ns - 2 - 2 * qb
    js_cur = ns - 2 + ((t & (Q_BLOCK - 1)) >> int(math.log2(SLC_BLOCK)))
    causal = (js >= js_first) & (js <= js_cur)
    forced = (js == js_first) | (js == js_cur) | (js == js_cur - 1)
    n_forced = 1 + (js_cur - 1 >= js_first).astype(jnp.int32) + (js_first < js_cur - 1).astype(jnp.int32)
    n_pick = jnp.minimum(ntop, js_cur - js_first + 1) - n_forced
    ninf = -jnp.inf
    score = jnp.where(causal & jnp.logical_not(forced), imp_ref[...], ninf)
    sel = jnp.where(causal & forced, 1.0, 0.0)
    jsf = js.astype(F32)
    for it in range(ntop - 1):
        mx = jnp.max(score, axis=0, keepdims=True)
        first = jnp.min(jnp.where(score == mx, jsf, float(ns)), axis=0, keepdims=True)
        first = jnp.where(it < n_pick, first, -1.0)
        hit = jsf == first
        sel = jnp.where(hit, 1.0, sel)
        score = jnp.where(hit, ninf, score)
    sel_ref[...] = sel.T.astype(sel_ref.dtype)


def _topk_select(imp_t, *, toks=512):
    b, g, ns, t = imp_t.shape
    toks = min(toks, t)
    assert t % toks == 0
    nt = t // toks
    sel = pl.pallas_call(
        functools.partial(_topk_kernel, ns=ns, ntop=min(SLC_TOP, ns)),
        out_shape=jax.ShapeDtypeStruct((b * g * t, ns), BF16),
        grid=(b * g, nt),
        in_specs=[pl.BlockSpec((None, ns, toks), lambda i, j: (i, 0, j))],
        out_specs=pl.BlockSpec((toks, ns), lambda i, j: (i * nt + j, 0)),
        compiler_params=_cparams(("parallel", "parallel")),
        name="nsa_topk",
    )(imp_t.reshape(b * g, ns, t))
    return sel.reshape(b, g, t, ns)


def _nsa_slc_kernel(q_ref, sel_ref, kks_ref, v1s_ref, bias_ref, eexp_ref, gates_ref, egate_ref, ocmp_ref, owin_ref,
                    o_ref, m_ref, acc_ref, p_ref, alpha_ref, *, ns, kt_near, nkt):
    qb = pl.program_id(2)
    low = _head_masks()
    kw = SLC_KEY_TILE
    kt0 = (ns - 2 - 2 * qb) // SLC_BLOCKS_PER_TILE

    m_ref[...] = jnp.full_like(m_ref, NEG_INF)
    acc_ref[...] = jnp.zeros_like(acc_ref)
    p_ref[...] = jnp.zeros_like(p_ref)
    alpha_ref[...] = jnp.ones_like(alpha_ref)
    q4 = _stack_heads(q_ref, low)
    selb = sel_ref[...]

    def row_start(kt):
        return pl.multiple_of(jnp.maximum(Q_BLOCK * qb + kw * kt, 0), Q_BLOCK)

    def accumulate(kt):
        pv = _dot(p_ref[...], v1s_ref[pl.ds(row_start(kt), kw), :])
        acc_ref[...] = alpha_ref[...] * acc_ref[...] + pv

    def body(kt):
        s4 = _dot_nt(q4, kks_ref[pl.ds(row_start(kt), kw), :])
        mb = (_dot(selb, eexp_ref[kt]) - 1.0) * (-NEG_INF)
        accumulate(kt - 1)
        bt = jnp.maximum(kt - kt_near + 1, 0)
        for hp in range(NSA_HPG):
            rows = slice(hp * Q_BLOCK, (hp + 1) * Q_BLOCK)
            s = s4[rows] + bias_ref[bt, rows, :] + mb
            m_old = m_ref[rows, :]
            m_new = jnp.maximum(m_old, jnp.max(s, axis=-1, keepdims=True))
            alpha_ref[rows, :] = jnp.exp(m_old - m_new)
            p_ref[rows, :] = jnp.exp(s - m_new[:, 0:1]).astype(BF16)
            m_ref[rows, :] = m_new

    kt_even = (kt0 // 2) * 2

    def body2(j, carry):
        body(kt_even + 2 * j)
        body(kt_even + 2 * j + 1)
        return carry

    lax.fori_loop(0, (nkt - kt_even) // 2, body2, 0)
    accumulate(nkt - 1)
    o_slc = _unstack_heads(_normalize_pv(acc_ref[...]), low)

    gh, gl = _split_bf16(gates_ref[...])
    g2 = jnp.concatenate([gh, gl], axis=1)
    o = (_dot(g2, egate_ref[0]) * ocmp_ref[...].astype(F32)
         + _dot(g2, egate_ref[1]) * o_slc
         + _dot(g2, egate_ref[2]) * owin_ref[...].astype(F32))
    o_ref[...] = o.astype(o_ref.dtype)


def _nsa_slc(q, sel, kvs, vks, bias_s, eexp, gates, egate, ocmp, owin, batch, seq):
    ns = seq // SLC_BLOCK
    nqb = seq // Q_BLOCK
    nkt = (ns * SLC_BLOCK) // SLC_KEY_TILE
    kt_near = nkt - (bias_s.shape[1] - 1)
    gw = NSA_HPG * NSA_HEAD_DIM
    srows = kvs.shape[2]
    slab = lambda r: pl.BlockSpec((None, None, r, LANES), lambda b, g, i: (b, g, 0, 0))
    head_spec = pl.BlockSpec((None, Q_BLOCK, gw), lambda b, g, i: (b, i, g))
    return pl.pallas_call(
        functools.partial(_nsa_slc_kernel, ns=ns, kt_near=kt_near, nkt=nkt),
        out_shape=jax.ShapeDtypeStruct((batch, seq, NSA_HEADS * NSA_HEAD_DIM), BF16),
        grid=(batch, NSA_KV_HEADS, nqb),
        in_specs=[
            head_spec,
            pl.BlockSpec((None, None, Q_BLOCK, ns), lambda b, g, i: (b, g, i, 0)),
            slab(srows), slab(srows),
            pl.BlockSpec((None,) + bias_s.shape[1:], lambda b, g, i: (g, 0, 0, 0)),
            pl.BlockSpec(eexp.shape, lambda b, g, i: (0, 0, 0)),
            pl.BlockSpec((None, Q_BLOCK, LANES), lambda b, g, i: (b, i, 0)),
            pl.BlockSpec((N_BRANCH, 2 * LANES, gw), lambda b, g, i: (0, 0, g)),
            head_spec, head_spec,
        ],
        out_specs=head_spec,
        scratch_shapes=[pltpu.VMEM((NSA_HPG * Q_BLOCK, LANES), F32),
                        pltpu.VMEM((NSA_HPG * Q_BLOCK, LANES), F32),
                        pltpu.VMEM((NSA_HPG * Q_BLOCK, SLC_KEY_TILE), BF16),
                        pltpu.VMEM((NSA_HPG * Q_BLOCK, LANES), F32)],
        compiler_params=_cparams(("parallel", "parallel", "arbitrary")),
        name="nsa_selected",
    )(q, sel, kvs, vks, bias_s, eexp, gates, egate, ocmp, owin)


def _nsa_tables(rel_bias, seq):
    nc = seq // CMP_STRIDE
    ns = seq // SLC_BLOCK
    i = np.arange(Q_BLOCK)[:, None]
    j = np.arange(nc)[None, :]
    dist_c = i - (CMP_BLOCK - 1) - CMP_STRIDE * (j - (nc - CMP_LOOKBACK))
    bias_c = _bias_table(rel_bias, dist_c, dist_c >= 0)
    cs = np.arange(nc)[:, None] * CMP_STRIDE
    ss = np.arange(ns)[None, :] * SLC_BLOCK
    ov = np.clip(np.minimum(cs + CMP_BLOCK, ss + SLC_BLOCK) - np.maximum(cs, ss), 0, None) / CMP_BLOCK
    ov2 = jnp.asarray(np.concatenate([ov, ov], axis=0).T, BF16)
    nkeys = ns * SLC_BLOCK
    nkt = nkeys // SLC_KEY_TILE
    pad_rows = (ns - 2) * SLC_BLOCK
    kt_near = max(0, pad_rows - REL_MAX_DIST) // SLC_KEY_TILE
    kr = np.arange(kt_near * SLC_KEY_TILE, nkeys)[None, :]
    dist_s = i + pad_rows - kr
    near = _bias_table(rel_bias, dist_s, dist_s >= 0)
    near = near.reshape(NSA_KV_HEADS, NSA_HPG * Q_BLOCK, nkt - kt_near, SLC_KEY_TILE).transpose(0, 2, 1, 3)
    far = jnp.broadcast_to(rel_bias.astype(F32)[REL_BUCKETS - 1].reshape(NSA_KV_HEADS, 1, NSA_HPG, 1, 1),
                           (NSA_KV_HEADS, 1, NSA_HPG, Q_BLOCK, SLC_KEY_TILE))
    bias_s = jnp.concatenate([far.reshape(NSA_KV_HEADS, 1, NSA_HPG * Q_BLOCK, SLC_KEY_TILE), near], axis=1)
    dist_w = i + WINDOW - np.arange(WINDOW + Q_BLOCK)[None, :]
    bwin = _bias_table(rel_bias, dist_w, (dist_w >= 0) & (dist_w < WINDOW))
    blk = np.arange(ns)[None, :, None]
    key = np.arange(SLC_KEY_TILE)[None, None, :]
    ktile = np.arange(nkt)[:, None, None]
    eexp = jnp.asarray(blk == ktile * SLC_BLOCKS_PER_TILE + key // SLC_BLOCK, BF16)
    col = np.arange(LANES)[:, None]
    head = (np.arange(NSA_HEADS * NSA_HEAD_DIM) // NSA_HEAD_DIM)[None, :]
    eg = np.stack([col == head * N_BRANCH + br for br in range(N_BRANCH)])
    egate = jnp.asarray(np.concatenate([eg, eg], axis=1), BF16)
    return bias_c, ov2, bias_s, bwin, eexp, egate


def _router_kernel(x_ref, g_ref, w_ref, o_ref):
    x = x_ref[...]
    ms = jnp.mean(x * x, axis=-1, keepdims=True)
    xn = x * lax.rsqrt(ms + NORM_EPS) * g_ref[...]
    logits = jnp.dot(xn, w_ref[...], preferred_element_type=F32, precision=lax.Precision.HIGHEST)
    lane = lax.broadcasted_iota(jnp.int32, logits.shape, 1).astype(F32)
    ninf = -jnp.inf
    s = jnp.where(lane < N_EXPERTS, logits, ninf)
    m1 = jnp.max(s, axis=-1, keepdims=True)
    i1 = jnp.min(jnp.where(s == m1, lane, float(LANES)), axis=-1, keepdims=True)
    s2 = jnp.where(lane == i1, ninf, s)
    m2 = jnp.max(s2, axis=-1, keepdims=True)
    i2 = jnp.min(jnp.where(s2 == m2, lane, float(LANES)), axis=-1, keepdims=True)
    e2 = jnp.exp(m2 - m1)
    w1 = 1.0 / (1.0 + e2)
    w2 = e2 * w1
    o_ref[...] = (jnp.where(lane == 0.0, i1, 0.0) + jnp.where(lane == 1.0, i2, 0.0)
                  + jnp.where(lane == 2.0, w1, 0.0) + jnp.where(lane == 3.0, w2, 0.0))


def _router(x, g, w_router, *, tm=512):
    n, d = x.shape
    tm = min(tm, n)
    wpad = jnp.zeros((d, LANES), F32).at[:, :N_EXPERTS].set(w_router.astype(F32))
    return pl.pallas_call(
        _router_kernel,
        out_shape=jax.ShapeDtypeStruct((n, LANES), F32),
        grid=(n // tm,),
        in_specs=[
            pl.BlockSpec((tm, d), lambda i: (i, 0)),
            pl.BlockSpec((1, d), lambda i: (0, 0)),
            pl.BlockSpec((d, LANES), lambda i: (0, 0)),
        ],
        out_specs=pl.BlockSpec((tm, LANES), lambda i: (i, 0)),
        compiler_params=_cparams(("parallel",)),
        name="moe_router",
    )(x, g.reshape(1, d).astype(F32), wpad)


def _moe_dispatch(route, tm):
    n = route.shape[0]
    pairs = 2 * n
    e = route[:, 0:2].astype(jnp.int32).reshape(pairs)
    onehot = (e[:, None] == jnp.arange(N_EXPERTS, dtype=jnp.int32)[None, :]).astype(jnp.int32)
    csum = jnp.cumsum(onehot, axis=0)
    rank = jnp.sum(csum * onehot, axis=1) - 1
    counts = csum[-1]
    cpad = ((counts + tm - 1) // tm) * tm
    gend = jnp.cumsum(cpad)
    dst = jnp.sum(onehot * (gend - cpad)[None, :], axis=1) + rank
    rows = pairs + N_EXPERTS * tm
    ntiles = rows // tm
    nvalid = gend[-1] // tm
    tile = jnp.arange(ntiles, dtype=jnp.int32)
    te = jnp.sum((tile[:, None] * tm >= gend[None, :]).astype(jnp.int32), axis=1)
    te = jnp.where(tile < nvalid, te, te[nvalid - 1])
    src = jnp.zeros((rows,), jnp.int32).at[dst].set(jnp.arange(pairs, dtype=jnp.int32) // 2)
    return src, dst, te.astype(jnp.int32), nvalid.reshape(1).astype(jnp.int32)


def _row_copy(src_hbm, row, dst_ref, j, sem):
    return pltpu.make_async_copy(src_hbm.at[pl.ds(row, 1), :], dst_ref.at[pl.ds(j, 1), :], sem)


def _combine_kernel(i0_ref, i1_ref, y_hbm, h_ref, route_ref, o_ref, buf_ref, sem):
    rows = h_ref.shape[0]

    def issue(g, c):
        for u in range(DMA_ISSUE_UNROLL):
            j = g * DMA_ISSUE_UNROLL + u
            _row_copy(y_hbm, i0_ref[j], buf_ref.at[0], j, sem).start(priority=0)
            _row_copy(y_hbm, i1_ref[j], buf_ref.at[1], j, sem).start(priority=1)
        return c

    lax.fori_loop(0, rows // DMA_ISSUE_UNROLL, issue, 0)
    for k in range(2):
        pltpu.make_async_copy(y_hbm.at[pl.ds(0, rows), :], buf_ref.at[k], sem).wait()
    route = route_ref[...]
    o_ref[...] = h_ref[...] + route[:, 2:3] * buf_ref[0] + route[:, 3:4] * buf_ref[1]


def _combine(h, y, dst, route, *, rows=512):
    n, d = h.shape
    assert n % rows == 0
    dst2 = dst.reshape(n, 2)
    return pl.pallas_call(
        _combine_kernel,
        out_shape=jax.ShapeDtypeStruct((n, d), F32),
        grid=(n // rows,),
        in_specs=[
            pl.BlockSpec((rows,), lambda i: (i,), memory_space=pltpu.SMEM),
            pl.BlockSpec((rows,), lambda i: (i,), memory_space=pltpu.SMEM),
            pl.BlockSpec(memory_space=pl.ANY),
            pl.BlockSpec((rows, d), lambda i: (i, 0)),
            pl.BlockSpec((rows, LANES), lambda i: (i, 0)),
        ],
        out_specs=pl.BlockSpec((rows, d), lambda i: (i, 0)),
        scratch_shapes=[pltpu.VMEM((2, rows, d), F32), pltpu.SemaphoreType.DMA],
        compiler_params=_cparams(("arbitrary",)),
        name="moe_combine",
    )(dst2[:, 0], dst2[:, 1], y, h, route)


def _ffn_grouped_kernel(te_ref, nv_ref, cur_ref, nxt_ref, h_hbm, g_ref, wg_ref, wu_ref, wd_ref, o_ref,
                        xbuf_ref, sem_ref, xn_ref, acc_ref, *, nf):
    i = pl.program_id(0)
    f = pl.program_id(1)
    tm = xbuf_ref.shape[1]
    used = i < nv_ref[0]
    slot = lax.rem(i, 2)

    part = tm // nf

    def gather(idx_ref, s, first, count):
        def issue(q, c):
            for u in range(DMA_ISSUE_UNROLL):
                j = first + q * DMA_ISSUE_UNROLL + u
                _row_copy(h_hbm, idx_ref[j], xbuf_ref.at[s], j, sem_ref.at[s]).start()
            return c

        lax.fori_loop(0, count // DMA_ISSUE_UNROLL, issue, 0)

    @pl.when(jnp.logical_and(used, jnp.logical_and(i == 0, f == 0)))
    def _():
        gather(cur_ref, 0, 0, tm)

    @pl.when(i + 1 < nv_ref[0])
    def _():
        gather(nxt_ref, 1 - slot, f * part, part)

    @pl.when(jnp.logical_and(used, f == 0))
    def _():
        pltpu.make_async_copy(h_hbm.at[pl.ds(0, tm), :], xbuf_ref.at[slot], sem_ref.at[slot]).wait()
        x = xbuf_ref[slot]
        ms = jnp.mean(x * x, axis=-1, keepdims=True)
        xn_ref[...] = (x * lax.rsqrt(ms + NORM_EPS) * g_ref[...]).astype(BF16)
        acc_ref[...] = jnp.zeros_like(acc_ref)

    @pl.when(used)
    def _():
        xn = xn_ref[...]
        gate = _dot(xn, wg_ref[...])
        up = _dot(xn, wu_ref[...])
        act = (gate * jax.nn.sigmoid(gate) * up).astype(BF16)
        acc_ref[...] += _dot(act, wd_ref[...])

    last = f == pl.num_programs(1) - 1

    @pl.when(jnp.logical_and(used, last))
    def _():
        o_ref[...] = acc_ref[...]

    @pl.when(jnp.logical_and(jnp.logical_not(used), last))
    def _():
        o_ref[...] = jnp.zeros_like(o_ref)


def _ffn_grouped(h, src, g, w_gu, w_down, te, nvalid, *, tm, tf=None):
    rows = src.shape[0]
    d = h.shape[1]
    ff = w_down.shape[1]
    if tf is None:
        tf = ff // 2 if (ff // 2) % LANES == 0 else ff
    assert rows % tm == 0 and ff % tf == 0 and tf % LANES == 0
    assert tm % ((ff // tf) * DMA_ISSUE_UNROLL) == 0
    nf = ff // tf
    ntiles = rows // tm

    def fcol(i, f, te_ref, nv_ref):
        return jnp.where(i < nv_ref[0], f, nf - 1)

    grid_spec = pltpu.PrefetchScalarGridSpec(
        num_scalar_prefetch=2,
        grid=(ntiles, nf),
        in_specs=[
            pl.BlockSpec((tm,), lambda i, f, te_ref, nv_ref: (i,), memory_space=pltpu.SMEM),
            pl.BlockSpec((tm,), lambda i, f, te_ref, nv_ref: (jnp.minimum(i + 1, ntiles - 1),),
                         memory_space=pltpu.SMEM),
            pl.BlockSpec(memory_space=pl.ANY),
            pl.BlockSpec((1, d), lambda i, f, te_ref, nv_ref: (0, 0)),
            pl.BlockSpec((None, d, tf), lambda i, f, te_ref, nv_ref: (te_ref[i], 0, fcol(i, f, te_ref, nv_ref))),
            pl.BlockSpec((None, d, tf),
                         lambda i, f, te_ref, nv_ref: (te_ref[i], 0, nf + fcol(i, f, te_ref, nv_ref))),
            pl.BlockSpec((None, tf, d), lambda i, f, te_ref, nv_ref: (te_ref[i], fcol(i, f, te_ref, nv_ref), 0)),
        ],
        out_specs=pl.BlockSpec((tm, d), lambda i, f, te_ref, nv_ref: (i, 0)),
        scratch_shapes=[pltpu.VMEM((2, tm, d), F32), pltpu.SemaphoreType.DMA((2,)),
                        pltpu.VMEM((tm, d), BF16), pltpu.VMEM((tm, d), F32)],
    )
    return pl.pallas_call(
        functools.partial(_ffn_grouped_kernel, nf=nf),
        out_shape=jax.ShapeDtypeStruct((rows, d), F32),
        grid_spec=grid_spec,
        compiler_params=_cparams(("arbitrary", "arbitrary")),
        name="expert_ffn_grouped",
    )(te, nvalid, src, src, h, g.reshape(1, d).astype(F32), w_gu, w_gu, w_down)


def _nsa_layer(h, batch, seq, norm_g, kv_norm_g, w_kv, k_norm_g, cmp_pos_k, cmp_w1_k, cmp_w2_k,
               cmp_pos_v, cmp_w1_v, cmp_w2_v, w_in, w_out, q_norm_g, rel_bias):
    n, d = h.shape
    g_heads, dh = NSA_KV_HEADS, NSA_HEAD_DIM
    nc = seq // CMP_STRIDE
    ns = seq // SLC_BLOCK
    hq = NSA_HEADS * dh

    kvproj = _norm_matmul(h, kv_norm_g, w_kv.astype(BF16), name="nsa_kv_proj")

    gw_kv = g_heads * dh

    def chunks(col0):
        t = kvproj[:, col0:col0 + gw_kv].astype(BF16).reshape(batch, seq, g_heads, dh)
        return t.transpose(0, 2, 1, 3).reshape(batch, g_heads, nc, CMP_STRIDE * dh)

    k_cmp = _compress(chunks(0), cmp_w1_k, cmp_pos_k, cmp_w2_k, k_norm_g[0],
                      apply_norm=True, name="nsa_compress_k")
    v_cmp = _compress(chunks(gw_kv), cmp_w1_v, cmp_pos_v, cmp_w2_v, k_norm_g[0],
                      apply_norm=False, name="nsa_compress_v")
    cpad = nc - CMP_LOOKBACK
    padc = lambda t: jnp.pad(t, ((0, 0), (0, 0), (cpad, 0), (0, 0)))
    kvc = padc(jnp.concatenate([k_cmp, k_cmp], axis=-1))
    vkc = padc(jnp.concatenate([v_cmp, jnp.ones_like(v_cmp)], axis=-1))

    kvs, vks = _pack_kv(kvproj, k_norm_g[1], 1, batch, seq, name="nsa_pack_selected")
    kvw, vkw = _pack_kv(kvproj, k_norm_g[2], 2, batch, seq, name="nsa_pack_window")
    spad = (ns - 2) * SLC_BLOCK
    pads = lambda t, r: jnp.pad(t, ((0, 0), (0, 0), (r, 0), (0, 0)))
    kvs, vks = pads(kvs, spad), pads(vks, spad)
    kvw, vkw = pads(kvw, WINDOW), pads(vkw, WINDOW)

    qgain = jnp.tile(q_norm_g.astype(F32), NSA_HEADS) * (dh ** -0.5)
    q = _norm_matmul(h, norm_g, w_in[:, :hq].astype(BF16), epilogue="headnorm", gain=qgain,
                     flag=jnp.ones((hq,), F32), out_dtype=BF16, name="nsa_q_proj")
    ngate = N_BRANCH * NSA_HEADS
    wg = jnp.zeros((d, LANES), BF16).at[:, :ngate].set(w_in[:, hq:].astype(BF16))
    gates = _norm_matmul(h, norm_g, wg, epilogue="sigmoid", name="nsa_gate_proj")

    bias_c, ov2, bias_s, bwin, eexp, egate = _nsa_tables(rel_bias, seq)
    q3 = q.reshape(batch, seq, hq)
    o_cmp, imp_t, o_win = _nsa_cmp_win(q3, kvc, vkc, bias_c, ov2, kvw, vkw, bwin, batch, seq)
    sel = _topk_select(imp_t)
    o = _nsa_slc(q3, sel, kvs, vks, bias_s, eexp, gates.reshape(batch, seq, LANES), egate, o_cmp, o_win,
                 batch, seq)
    return _matmul_res(o.reshape(n, hq), w_out.astype(BF16), h, name="nsa_out_proj")


def kernel(x, norm_mix_g, norm_ffn_g, a_w_in, a_w_out, a_onorm_g, lb_param, kv_norm_g, w_kv, k_norm_g,
           cmp_pos_k, cmp_w1_k, cmp_w2_k, cmp_pos_v, cmp_w1_v, cmp_w2_v, b_w_in, b_w_out, b_qnorm_g,
           rel_bias, ffn_w_gu, ffn_w_down, moe_router, moe_w_gu, moe_w_down):
    batch, seq, d = x.shape
    n = batch * seq
    h = x.reshape(n, d).astype(F32)

    lower = jnp.cumsum(jax.nn.softmax(lb_param.astype(F32), axis=0), axis=0)[0]
    proj = _norm_matmul(h, norm_mix_g[0], a_w_in[0].astype(BF16), name="hgrn_in_proj")
    og = _hgrn_mixer(proj, lower, a_onorm_g[0], batch, seq)
    h = _matmul_res(og, a_w_out[0].astype(BF16), h, name="hgrn_out_proj")
    h = _ffn(h, norm_ffn_g[0], ffn_w_gu[0].astype(BF16), ffn_w_down[0].astype(BF16), tm=1024, name="dense_ffn")

    h = _nsa_layer(h, batch, seq, norm_mix_g[1], kv_norm_g, w_kv, k_norm_g, cmp_pos_k, cmp_w1_k, cmp_w2_k,
                   cmp_pos_v, cmp_w1_v, cmp_w2_v, b_w_in[0], b_w_out[0], b_qnorm_g[0], rel_bias)
    route = _router(h, norm_ffn_g[1], moe_router[0])
    src, dst, tile_expert, tiles_used = _moe_dispatch(route, MOE_ROW_TILE)
    ys = _ffn_grouped(h, src, norm_ffn_g[1], moe_w_gu[0].astype(BF16), moe_w_down[0].astype(BF16),
                      tile_expert, tiles_used, tm=MOE_ROW_TILE)
    out = _combine(h, ys, dst, route)
    return out.reshape(batch, seq, d).astype(x.dtype)
```

```python
import functools
import math

import jax
import jax.numpy as jnp
import numpy as np
from jax import lax
from jax.experimental import pallas as pl
from jax.experimental.pallas import tpu as pltpu

F32 = jnp.float32
BF16 = jnp.bfloat16

NORM_EPS = 1e-6
NEG_INF = -1e30
FORCE_BONUS = 1e4
HGRN_HEAD_DIM = 128
NSA_HEADS = 16
NSA_KV_HEADS = 4
NSA_HPG = NSA_HEADS // NSA_KV_HEADS
NSA_HEAD_DIM = 64
N_BRANCH = 3
CMP_BLOCK = 32
CMP_STRIDE = 16
SLC_BLOCK = 64
SLC_TOP = 16
WINDOW = 512
Q_BLOCK = 128
REL_BUCKETS = 32
REL_MAX_DIST = 2048
N_EXPERTS = 8

LANES = 128
SUBLANES = 8
VMEM_LIMIT_BYTES = 56 * 1024 * 1024

HGRN_CHUNK = 128
HGRN_ROWS_PER_STEP = 512
HGRN_HEADS_PER_STEP = 8
SLC_KEY_TILE = 512
SLC_BLOCKS_PER_TILE = SLC_KEY_TILE // SLC_BLOCK
CMP_LOOKBACK = Q_BLOCK // CMP_STRIDE
CMP_QBLOCKS_PER_STEP = 2
MOE_ROW_TILE = 512
DMA_ISSUE_UNROLL = 16


def _cparams(sem):
    return pltpu.CompilerParams(dimension_semantics=sem, vmem_limit_bytes=VMEM_LIMIT_BYTES)


def _dot(a, b):
    return jnp.dot(a, b, preferred_element_type=F32)


def _dot_nt(a, b):
    return lax.dot_general(a, b, (((1,), (1,)), ((), ())), preferred_element_type=F32)


def _split_bf16(x):
    hi = x.astype(BF16)
    lo = (x - hi.astype(F32)).astype(BF16)
    return hi, lo


def _norm_matmul_kernel(x_ref, g_ref, w_ref, *rest, epilogue, tn):
    if epilogue == "headnorm":
        gain_ref, flag_ref, bd_ref, o_ref = rest
    else:
        (o_ref,) = rest
    x = x_ref[...]
    ms = jnp.mean(x * x, axis=-1, keepdims=True)
    xn = (x * lax.rsqrt(ms + NORM_EPS) * g_ref[...]).astype(BF16)
    for c in range(o_ref.shape[1] // tn):
        cols = slice(c * tn, (c + 1) * tn)
        acc = _dot(xn, w_ref[:, cols])
        if epilogue == "headnorm":
            ss = _dot((acc * acc).astype(BF16), bd_ref[...])
            normed = acc * lax.rsqrt(ss * (1.0 / NSA_HEAD_DIM) + NORM_EPS) * gain_ref[:, cols]
            acc = jnp.where(flag_ref[:, cols] > 0.0, normed, acc)
        elif epilogue == "sigmoid":
            acc = jax.nn.sigmoid(acc)
        o_ref[:, cols] = acc.astype(o_ref.dtype)


def _norm_matmul(x, g, w, *, epilogue="none", gain=None, flag=None, out_dtype=F32, tm=512, tn=512,
                 name="norm_matmul"):
    n, d = x.shape
    m = w.shape[1]
    tm = min(tm, n)
    tn = min(tn, m)
    assert n % tm == 0 and m % tn == 0, (n, tm, m, tn)
    const = lambda i: (0, 0)
    in_specs = [pl.BlockSpec((tm, d), lambda i: (i, 0)), pl.BlockSpec((1, d), const), pl.BlockSpec((d, m), const)]
    args = [x, g.reshape(1, d).astype(F32), w]
    if epilogue == "headnorm":
        assert tn % NSA_HEAD_DIM == 0
        grp = np.arange(tn) // NSA_HEAD_DIM
        bd = jnp.asarray(grp[:, None] == grp[None, :], BF16)
        in_specs += [pl.BlockSpec((1, m), const), pl.BlockSpec((1, m), const), pl.BlockSpec((tn, tn), const)]
        args += [gain.reshape(1, m).astype(F32), flag.reshape(1, m).astype(F32), bd]
    return pl.pallas_call(
        functools.partial(_norm_matmul_kernel, epilogue=epilogue, tn=tn),
        out_shape=jax.ShapeDtypeStruct((n, m), out_dtype),
        grid=(n // tm,),
        in_specs=in_specs,
        out_specs=pl.BlockSpec((tm, m), lambda i: (i, 0)),
        compiler_params=_cparams(("parallel",)),
        name=name,
    )(*args)


def _matmul_res_kernel(a_ref, w_ref, r_ref, o_ref, *, tn):
    a = a_ref[...]
    for c in range(o_ref.shape[1] // tn):
        cols = slice(c * tn, (c + 1) * tn)
        o_ref[:, cols] = r_ref[:, cols] + _dot(a, w_ref[:, cols])


def _matmul_res(a, w, res, *, tm=512, tn=512, name="matmul_res"):
    n, k = a.shape
    m = w.shape[1]
    tm = min(tm, n)
    tn = min(tn, m)
    assert n % tm == 0 and m % tn == 0
    return pl.pallas_call(
        functools.partial(_matmul_res_kernel, tn=tn),
        out_shape=jax.ShapeDtypeStruct((n, m), F32),
        grid=(n // tm,),
        in_specs=[
            pl.BlockSpec((tm, k), lambda i: (i, 0)),
            pl.BlockSpec((k, m), lambda i: (0, 0)),
            pl.BlockSpec((tm, m), lambda i: (i, 0)),
        ],
        out_specs=pl.BlockSpec((tm, m), lambda i: (i, 0)),
        compiler_params=_cparams(("parallel",)),
        name=name,
    )(a, w, res)


def _ffn_kernel(x_ref, g_ref, wg_ref, wu_ref, wd_ref, o_ref, xn_ref, acc_ref):
    f = pl.program_id(1)

    @pl.when(f == 0)
    def _():
        x = x_ref[...]
        ms = jnp.mean(x * x, axis=-1, keepdims=True)
        xn_ref[...] = (x * lax.rsqrt(ms + NORM_EPS) * g_ref[...]).astype(BF16)
        acc_ref[...] = jnp.zeros_like(acc_ref)

    xn = xn_ref[...]
    gate = _dot(xn, wg_ref[...])
    up = _dot(xn, wu_ref[...])
    act = (gate * jax.nn.sigmoid(gate) * up).astype(BF16)
    acc_ref[...] += _dot(act, wd_ref[...])

    @pl.when(f == pl.num_programs(1) - 1)
    def _():
        o_ref[...] = x_ref[...] + acc_ref[...]


def _ffn(x, g, w_gu, w_down, *, tm=512, tf=None, name="ffn"):
    n, d = x.shape
    ff = w_down.shape[0]
    if tf is None:
        tf = ff // 2 if (ff // 2) % LANES == 0 else ff
    tm = min(tm, n)
    assert n % tm == 0 and ff % tf == 0 and tf % LANES == 0
    nf = ff // tf
    return pl.pallas_call(
        _ffn_kernel,
        out_shape=jax.ShapeDtypeStruct((n, d), F32),
        grid=(n // tm, nf),
        in_specs=[
            pl.BlockSpec((tm, d), lambda i, f: (i, 0)),
            pl.BlockSpec((1, d), lambda i, f: (0, 0)),
            pl.BlockSpec((d, tf), lambda i, f: (0, f)),
            pl.BlockSpec((d, tf), lambda i, f: (0, nf + f)),
            pl.BlockSpec((tf, d), lambda i, f: (f, 0)),
        ],
        out_specs=pl.BlockSpec((tm, d), lambda i, f: (i, 0)),
        scratch_shapes=[pltpu.VMEM((tm, d), BF16), pltpu.VMEM((tm, d), F32)],
        compiler_params=_cparams(("parallel", "arbitrary")),
        name=name,
    )(x, g.reshape(1, d).astype(F32), w_gu, w_gu, w_down)


def _hgrn_decay_matrix(c):
    levels = int(math.log2(c))
    out = np.zeros(((levels + 2) * c, c), np.float32)
    for l in range(levels):
        m = c >> (l + 1)
        for r in range(c):
            mid = (r // (2 * m)) * 2 * m + m - 1
            if r % (2 * m) >= m:
                out[l * c + r, mid + 1:r + 1] = 1.0
            else:
                out[l * c + r, r + 1:mid + 1] = 1.0
    for r in range(c):
        out[levels * c + r, :r + 1] = 1.0
        out[(levels + 1) * c + r, r + 1:] = 1.0
    return out


def _hgrn_kernel(q_ref, f_ref, v_ref, g_ref, lb_ref, gn_ref, m_ref, o_ref, st_ref, *, chunk, nchunk):
    c = chunk
    dh = HGRN_HEAD_DIM
    levels = int(math.log2(c))

    @pl.when(pl.program_id(2) == 0)
    def _():
        st_ref[...] = jnp.zeros_like(st_ref)

    gn = gn_ref[...]
    row = lax.broadcasted_iota(jnp.int32, (c, c), 0)
    col = lax.broadcasted_iota(jnp.int32, (c, c), 1)
    rowv = lax.broadcasted_iota(jnp.int32, (c, dh), 0)
    nheads = st_ref.shape[0]

    def body(ci, carry):
        sl = pl.ds(pl.multiple_of(ci * c, c), c)
        lb = lb_ref[...]
        fg = lb + (1.0 - lb) * jax.nn.sigmoid(f_ref[sl, :])
        kall = 1.0 - fg
        hi, lo = _split_bf16(jnp.log(fg))
        dall = _dot(m_ref[...], jnp.concatenate([hi, lo], axis=0))
        hs = range(nheads)
        cols = [slice(hh * dh, (hh + 1) * dh) for hh in hs]
        q = [q_ref[sl, cols[hh]] for hh in hs]
        k = [kall[:, cols[hh]] for hh in hs]
        v = [v_ref[sl, cols[hh]] for hh in hs]

        a = [jnp.where(row == col, _dot_nt(q[hh].astype(BF16), k[hh].astype(BF16)), 0.0) for hh in hs]
        for l in range(levels):
            sh = levels - 1 - l
            upper = ((rowv >> sh) & 1) == 1
            same = (row >> (sh + 1)) == (col >> (sh + 1))
            for hh in hs:
                e = jnp.exp(dall[l * c:(l + 1) * c, cols[hh]])
                qe = jnp.where(upper, q[hh] * e, 0.0).astype(BF16)
                ke = jnp.where(upper, 0.0, k[hh] * e).astype(BF16)
                a[hh] = a[hh] + jnp.where(same, _dot_nt(qe, ke), 0.0)
        b = [dall[levels * c:(levels + 1) * c, cols[hh]] for hh in hs]
        st = [st_ref[hh] for hh in hs]
        o = [_dot(a[hh].astype(BF16), v[hh].astype(BF16))
             + _dot_nt((q[hh] * jnp.exp(b[hh])).astype(BF16), st[hh].astype(BF16)) for hh in hs]
        for hh in hs:
            kr = (k[hh] * jnp.exp(dall[(levels + 1) * c:, cols[hh]])).astype(BF16)
            st_ref[hh] = st[hh] * jnp.exp(b[hh][c - 1:c, :]) + _dot(v[hh].T.astype(BF16), kr)
        for hh in hs:
            ms = jnp.mean(o[hh] * o[hh], axis=-1, keepdims=True)
            on = o[hh] * lax.rsqrt(ms + NORM_EPS) * gn
            gg = g_ref[sl, cols[hh]]
            o_ref[sl, cols[hh]] = (on * (gg * jax.nn.sigmoid(gg))).astype(o_ref.dtype)
        return carry

    lax.fori_loop(0, nchunk, body, 0)


def _hgrn_mixer(proj, lb, gn, batch, seq):
    n, d4 = proj.shape
    d = d4 // 4
    heads = d // HGRN_HEAD_DIM
    rows = min(HGRN_ROWS_PER_STEP, seq)
    chunk = min(HGRN_CHUNK, rows)
    assert seq % rows == 0 and rows % chunk == 0
    nt = seq // rows
    m1 = _hgrn_decay_matrix(chunk)
    m = jnp.asarray(np.concatenate([m1, m1], axis=1), BF16)

    hps = HGRN_HEADS_PER_STEP
    assert heads % hps == 0
    groups = heads // hps
    width = hps * HGRN_HEAD_DIM

    def col_spec(s):
        return pl.BlockSpec((rows, width), lambda b, h, t, s=s: (b * nt + t, s * groups + h))

    return pl.pallas_call(
        functools.partial(_hgrn_kernel, chunk=chunk, nchunk=rows // chunk),
        out_shape=jax.ShapeDtypeStruct((n, d), BF16),
        grid=(batch, groups, nt),
        in_specs=[
            col_spec(0), col_spec(1), col_spec(2), col_spec(3),
            pl.BlockSpec((1, width), lambda b, h, t: (0, h)),
            pl.BlockSpec((1, HGRN_HEAD_DIM), lambda b, h, t: (0, 0)),
            pl.BlockSpec(m.shape, lambda b, h, t: (0, 0)),
        ],
        out_specs=pl.BlockSpec((rows, width), lambda b, h, t: (b * nt + t, h)),
        scratch_shapes=[pltpu.VMEM((hps, HGRN_HEAD_DIM, HGRN_HEAD_DIM), F32)],
        compiler_params=_cparams(("parallel", "parallel", "arbitrary")),
        name="hgrn2_recurrence",
    )(proj, proj, proj, proj, lb.reshape(1, d).astype(F32), gn.reshape(1, HGRN_HEAD_DIM).astype(F32), m)


def _compress_kernel(x_ref, w1c_ref, pos_ref, w1_ref, w2_ref, gain_ref, o_ref, *, apply_norm):
    hid = w2_ref.shape[0]
    nc = x_ref.shape[0]
    uv = _dot(x_ref[...], w1c_ref[...])
    posb = _dot(pos_ref[...], w1_ref[...])[0:1, :]
    pre = uv[:, :hid] + pltpu.roll(uv[:, hid:], nc - 1, 0) + posb
    out = _dot(jax.nn.gelu(pre).astype(BF16), w2_ref[...])
    if apply_norm:
        ms = jnp.mean(out * out, axis=-1, keepdims=True)
        out = out * lax.rsqrt(ms + NORM_EPS) * gain_ref[...]
    o_ref[...] = out


def _compress(x, w1, pos, w2, gain, *, apply_norm, name):
    b, g, nc, half = x.shape
    hid = w1.shape[1]
    dh = w2.shape[1]
    w1c = jnp.concatenate([w1[:half], w1[half:]], axis=1).astype(BF16)
    posr = jnp.broadcast_to(pos.reshape(1, -1), (SUBLANES, pos.size)).astype(BF16)
    return pl.pallas_call(
        functools.partial(_compress_kernel, apply_norm=apply_norm),
        out_shape=jax.ShapeDtypeStruct((b, g, nc, dh), F32),
        grid=(b, g),
        in_specs=[
            pl.BlockSpec((None, None, nc, half), lambda i, j: (i, j, 0, 0)),
            pl.BlockSpec((half, 2 * hid), lambda i, j: (0, 0)),
            pl.BlockSpec((SUBLANES, 2 * half), lambda i, j: (0, 0)),
            pl.BlockSpec((2 * half, hid), lambda i, j: (0, 0)),
            pl.BlockSpec((hid, dh), lambda i, j: (0, 0)),
            pl.BlockSpec((1, dh), lambda i, j: (0, 0)),
        ],
        out_specs=pl.BlockSpec((None, None, nc, dh), lambda i, j: (i, j, 0, 0)),
        compiler_params=_cparams(("parallel", "parallel")),
        name=name,
    )(x, w1c, posr, w1.astype(BF16), w2.astype(BF16), gain.reshape(1, dh).astype(F32))


def _pack_kv_kernel(k_ref, v_ref, gain_ref, kk_ref, v1_ref):
    kp = k_ref[...]
    vp = v_ref[...]
    lane = lax.broadcasted_iota(jnp.int32, kp.shape, 1)
    low = lane < NSA_HEAD_DIM
    sq = kp * kp
    ss_lo = jnp.sum(jnp.where(low, sq, 0.0), axis=-1, keepdims=True)
    ss_hi = jnp.sum(jnp.where(low, 0.0, sq), axis=-1, keepdims=True)
    ms = jnp.where(low, ss_lo, ss_hi) * (1.0 / NSA_HEAD_DIM)
    kn = kp * lax.rsqrt(ms + NORM_EPS) * gain_ref[...]
    kr = pltpu.roll(kn, NSA_HEAD_DIM, 1)
    vr = pltpu.roll(vp, NSA_HEAD_DIM, 1)
    kk_ref[0] = jnp.where(low, kn, kr).astype(kk_ref.dtype)
    kk_ref[1] = jnp.where(low, kr, kn).astype(kk_ref.dtype)
    v1_ref[0] = jnp.where(low, vp, 1.0).astype(v1_ref.dtype)
    v1_ref[1] = jnp.where(low, vr, 1.0).astype(v1_ref.dtype)


def _pack_kv(kvproj, gain, branch, batch, seq, *, tm=512, name="pack_kv"):
    tm = min(tm, seq)
    nt = seq // tm
    pairs = NSA_KV_HEADS // 2
    kcol = branch * 2 * pairs
    vcol = kcol + pairs
    gain2 = jnp.tile(gain.reshape(1, NSA_HEAD_DIM), (1, 2)).astype(F32)
    out = jax.ShapeDtypeStruct((batch, NSA_KV_HEADS, seq, LANES), BF16)
    ospec = pl.BlockSpec((None, 2, tm, LANES), lambda b, t, p: (b, p, t, 0))
    return pl.pallas_call(
        _pack_kv_kernel,
        out_shape=(out, out),
        grid=(batch, nt, pairs),
        in_specs=[
            pl.BlockSpec((tm, LANES), lambda b, t, p: (b * nt + t, kcol + p)),
            pl.BlockSpec((tm, LANES), lambda b, t, p: (b * nt + t, vcol + p)),
            pl.BlockSpec((1, LANES), lambda b, t, p: (0, 0)),
        ],
        out_specs=(ospec, ospec),
        compiler_params=_cparams(("parallel", "parallel", "parallel")),
        name=name,
    )(kvproj, kvproj, gain2)


def _rel_bucket_np(dist):
    max_exact = REL_BUCKETS // 2
    d = np.maximum(dist, 0)
    large = max_exact + (np.log(np.maximum(d, 1).astype(np.float32) / max_exact)
                         / math.log(REL_MAX_DIST / max_exact) * (REL_BUCKETS - max_exact)).astype(np.int32)
    large = np.minimum(large, REL_BUCKETS - 1)
    return np.where(d < max_exact, d, large).astype(np.int32)


def _bias_table(rel_bias, dist, valid):
    onehot = jax.nn.one_hot(_rel_bucket_np(dist), REL_BUCKETS, dtype=F32)
    vals = jnp.einsum("qkb,bh->hqk", onehot, rel_bias.astype(F32), precision=lax.Precision.HIGHEST)
    vals = jnp.where(jnp.asarray(valid)[None], vals, NEG_INF)
    return vals.reshape(NSA_KV_HEADS, NSA_HPG * dist.shape[0], dist.shape[1])


def _head_masks():
    lane = lax.broadcasted_iota(jnp.int32, (Q_BLOCK, LANES), 1)
    return lane < NSA_HEAD_DIM


def _stack_heads(q_ref, low):
    parts = []
    for pair in range(NSA_HPG // 2):
        qp = q_ref[:, pair * LANES:(pair + 1) * LANES].astype(F32)
        parts.append(jnp.where(low, qp, 0.0))
        parts.append(jnp.where(low, 0.0, qp))
    return jnp.concatenate(parts, axis=0).astype(BF16)


def _unstack_heads(x, low):
    parts = []
    for pair in range(NSA_HPG // 2):
        a = x[(2 * pair) * Q_BLOCK:(2 * pair + 1) * Q_BLOCK]
        b = x[(2 * pair + 1) * Q_BLOCK:(2 * pair + 2) * Q_BLOCK]
        parts.append(jnp.where(low, a, pltpu.roll(b, NSA_HEAD_DIM, 1)))
    return jnp.concatenate(parts, axis=1)


def _normalize_pv(pv):
    den = pltpu.roll(pv, NSA_HEAD_DIM, 1)
    return pv * jnp.where(den > 0.0, 1.0 / den, 0.0)


def _nsa_cmp_kernel(q_ref, kk_ref, v1_ref, bias_ref, ov_ref, kkw_ref, v1w_ref, bwin_ref, o_ref, imp_ref, ow_ref,
                    *, nc):
    nsub = q_ref.shape[0] // Q_BLOCK
    low = _head_masks()
    jcol = lax.broadcasted_iota(jnp.int32, (Q_BLOCK, nc), 1)
    subs = range(nsub)
    qbs = [pl.program_id(2) * nsub + u for u in subs]
    starts = [pl.multiple_of(qb * CMP_LOOKBACK, SUBLANES) for qb in qbs]
    q4 = [_stack_heads(q_ref.at[u * Q_BLOCK:(u + 1) * Q_BLOCK], low) for u in subs]
    s4 = [_dot_nt(q4[u], kk_ref[pl.ds(starts[u], nc), :].astype(BF16)) for u in subs]
    wk = WINDOW + Q_BLOCK
    wstarts = [pl.multiple_of(Q_BLOCK * qb, Q_BLOCK) for qb in qbs]
    sw = [_dot_nt(q4[u], kkw_ref[pl.ds(wstarts[u], wk), :]) for u in subs]
    wcol = lax.broadcasted_iota(jnp.int32, (Q_BLOCK, wk), 1)
    wprobs = []
    for u in subs:
        in_seq = wcol >= (WINDOW - Q_BLOCK * qbs[u])
        ps = []
        for hp in range(NSA_HPG):
            rows = slice(hp * Q_BLOCK, (hp + 1) * Q_BLOCK)
            s = jnp.where(in_seq, sw[u][rows] + bwin_ref[rows, :], NEG_INF)
            ps.append(jnp.exp(s - jnp.max(s, axis=-1, keepdims=True)).astype(BF16))
        wprobs.append(jnp.concatenate(ps, axis=0))
    probs, psums = [], []
    for u in subs:
        exists = jcol >= (nc - CMP_LOOKBACK - CMP_LOOKBACK * qbs[u])
        psum = jnp.zeros((Q_BLOCK, nc), F32)
        ps = []
        for hp in range(NSA_HPG):
            rows = slice(hp * Q_BLOCK, (hp + 1) * Q_BLOCK)
            s = jnp.where(exists, s4[u][rows] + bias_ref[rows, :], NEG_INF)
            mx = jnp.max(s, axis=-1, keepdims=True)
            p = jnp.exp(s - mx)
            den = jnp.sum(p, axis=-1, keepdims=True)
            p = p * jnp.where(mx > 0.5 * NEG_INF, 1.0 / den, 0.0)
            psum = psum + p
            ps.append(p.astype(BF16))
        probs.append(jnp.concatenate(ps, axis=0))
        psums.append(psum)
    for u in subs:
        pvw = _dot(wprobs[u], v1w_ref[pl.ds(wstarts[u], wk), :])
        ow_ref[u * Q_BLOCK:(u + 1) * Q_BLOCK, :] = _unstack_heads(_normalize_pv(pvw), low).astype(ow_ref.dtype)
    for u in subs:
        pv = _dot(probs[u], v1_ref[pl.ds(starts[u], nc), :].astype(BF16))
        o_ref[u * Q_BLOCK:(u + 1) * Q_BLOCK, :] = _unstack_heads(pv, low).astype(o_ref.dtype)
    for u in subs:
        hi, lo = _split_bf16(psums[u])
        imp_ref[:, u * Q_BLOCK:(u + 1) * Q_BLOCK] = _dot_nt(ov_ref[...], jnp.concatenate([hi, lo], axis=1))


def _nsa_cmp_win(q, kvc, vkc, bias_c, ov2, kvw, vkw, bwin, batch, seq):
    nc = seq // CMP_STRIDE
    ns = seq // SLC_BLOCK
    nqb = seq // Q_BLOCK
    rows = kvc.shape[2]
    wrows = kvw.shape[2]
    gw = NSA_HPG * NSA_HEAD_DIM
    qrows = CMP_QBLOCKS_PER_STEP * Q_BLOCK
    assert nqb % CMP_QBLOCKS_PER_STEP == 0
    slab = lambda r: pl.BlockSpec((None, None, r, LANES), lambda b, g, i: (b, g, 0, 0))
    head_out = jax.ShapeDtypeStruct((batch, seq, NSA_HEADS * NSA_HEAD_DIM), BF16)
    head_spec = pl.BlockSpec((None, qrows, gw), lambda b, g, i: (b, i, g))
    return pl.pallas_call(
        functools.partial(_nsa_cmp_kernel, nc=nc),
        out_shape=(head_out, jax.ShapeDtypeStruct((batch, NSA_KV_HEADS, ns, seq), F32), head_out),
        grid=(batch, NSA_KV_HEADS, nqb // CMP_QBLOCKS_PER_STEP),
        in_specs=[
            head_spec,
            slab(rows), slab(rows),
            pl.BlockSpec((None, NSA_HPG * Q_BLOCK, nc), lambda b, g, i: (g, 0, 0)),
            pl.BlockSpec((ns, 2 * nc), lambda b, g, i: (0, 0)),
            slab(wrows), slab(wrows),
            pl.BlockSpec((None,) + bwin.shape[1:], lambda b, g, i: (g, 0, 0)),
        ],
        out_specs=(head_spec, pl.BlockSpec((None, None, ns, qrows), lambda b, g, i: (b, g, 0, i)), head_spec),
        compiler_params=_cparams(("parallel", "parallel", "parallel")),
        name="nsa_compressed_window",
    )(q, kvc, vkc, bias_c, ov2, kvw, vkw, bwin)


def _topk_kernel(imp_ref, sel_ref, *, ns, ntop):
    toks = imp_ref.shape[1]
    t = pl.program_id(1) * toks + lax.broadcasted_iota(jnp.int32, (1, toks), 1)
    qb = t >> int(math.log2(Q_BLOCK))
    js = lax.broadcasted_iota(jnp.int32, (ns, toks), 0)
    js_first = ns - 2 - 2 * qb
    js_cur = ns - 2 + ((t & (Q_BLOCK - 1)) >> int(math.log2(SLC_BLOCK)))
    causal = (js >= js_first) & (js <= js_cur)
    forced = (js == js_first) | (js == js_cur) | (js == js_cur - 1)
    n_forced = 1 + (js_cur - 1 >= js_first).astype(jnp.int32) + (js_first < js_cur - 1).astype(jnp.int32)
    n_pick = jnp.minimum(ntop, js_cur - js_first + 1) - n_forced
    ninf = -jnp.inf
    score = jnp.where(causal & jnp.logical_not(forced), imp_ref[...], ninf)
    sel = jnp.where(causal & forced, 1.0, 0.0)
    jsf = js.astype(F32)
    for it in range(ntop - 1):
        mx = jnp.max(score, axis=0, keepdims=True)
        first = jnp.min(jnp.where(score == mx, jsf, float(ns)), axis=0, keepdims=True)
        first = jnp.where(it < n_pick, first, -1.0)
        hit = jsf == first
        sel = jnp.where(hit, 1.0, sel)
        score = jnp.where(hit, ninf, score)
    sel_ref[...] = sel.T.astype(sel_ref.dtype)


def _topk_select(imp_t, *, toks=512):
    b, g, ns, t = imp_t.shape
    toks = min(toks, t)
    assert t % toks == 0
    nt = t // toks
    sel = pl.pallas_call(
        functools.partial(_topk_kernel, ns=ns, ntop=min(SLC_TOP, ns)),
        out_shape=jax.ShapeDtypeStruct((b * g * t, ns), BF16),
        grid=(b * g, nt),
        in_specs=[pl.BlockSpec((None, ns, toks), lambda i, j: (i, 0, j))],
        out_specs=pl.BlockSpec((toks, ns), lambda i, j: (i * nt + j, 0)),
        compiler_params=_cparams(("parallel", "parallel")),
        name="nsa_topk",
    )(imp_t.reshape(b * g, ns, t))
    return sel.reshape(b, g, t, ns)


def _nsa_slc_kernel(q_ref, sel_ref, kks_ref, v1s_ref, bias_ref, eexp_ref, gates_ref, egate_ref, ocmp_ref, owin_ref,
                    o_ref, m_ref, acc_ref, p_ref, alpha_ref, *, ns, kt_near, nkt):
    qb = pl.program_id(2)
    low = _head_masks()
    kw = SLC_KEY_TILE
    kt0 = (ns - 2 - 2 * qb) // SLC_BLOCKS_PER_TILE

    m_ref[...] = jnp.full_like(m_ref, NEG_INF)
    acc_ref[...] = jnp.zeros_like(acc_ref)
    p_ref[...] = jnp.zeros_like(p_ref)
    alpha_ref[...] = jnp.ones_like(alpha_ref)
    q4 = _stack_heads(q_ref, low)
    selb = sel_ref[...]

    def row_start(kt):
        return pl.multiple_of(jnp.maximum(Q_BLOCK * qb + kw * kt, 0), Q_BLOCK)

    def accumulate(kt):
        pv = _dot(p_ref[...], v1s_ref[pl.ds(row_start(kt), kw), :])
        acc_ref[...] = alpha_ref[...] * acc_ref[...] + pv

    def body(kt):
        s4 = _dot_nt(q4, kks_ref[pl.ds(row_start(kt), kw), :])
        mb = (_dot(selb, eexp_ref[kt]) - 1.0) * (-NEG_INF)
        accumulate(kt - 1)
        bt = jnp.maximum(kt - kt_near + 1, 0)
        for hp in range(NSA_HPG):
            rows = slice(hp * Q_BLOCK, (hp + 1) * Q_BLOCK)
            s = s4[rows] + bias_ref[bt, rows, :] + mb
            m_old = m_ref[rows, :]
            m_new = jnp.maximum(m_old, jnp.max(s, axis=-1, keepdims=True))
            alpha_ref[rows, :] = jnp.exp(m_old - m_new)
            p_ref[rows, :] = jnp.exp(s - m_new[:, 0:1]).astype(BF16)
            m_ref[rows, :] = m_new

    kt_even = (kt0 // 2) * 2

    def body2(j, carry):
        body(kt_even + 2 * j)
        body(kt_even + 2 * j + 1)
        return carry

    lax.fori_loop(0, (nkt - kt_even) // 2, body2, 0)
    accumulate(nkt - 1)
    o_slc = _unstack_heads(_normalize_pv(acc_ref[...]), low)

    gh, gl = _split_bf16(gates_ref[...])
    g2 = jnp.concatenate([gh, gl], axis=1)
    o = (_dot(g2, egate_ref[0]) * ocmp_ref[...].astype(F32)
         + _dot(g2, egate_ref[1]) * o_slc
         + _dot(g2, egate_ref[2]) * owin_ref[...].astype(F32))
    o_ref[...] = o.astype(o_ref.dtype)


def _nsa_slc(q, sel, kvs, vks, bias_s, eexp, gates, egate, ocmp, owin, batch, seq):
    ns = seq // SLC_BLOCK
    nqb = seq // Q_BLOCK
    nkt = (ns * SLC_BLOCK) // SLC_KEY_TILE
    kt_near = nkt - (bias_s.shape[1] - 1)
    gw = NSA_HPG * NSA_HEAD_DIM
    srows = kvs.shape[2]
    slab = lambda r: pl.BlockSpec((None, None, r, LANES), lambda b, g, i: (b, g, 0, 0))
    head_spec = pl.BlockSpec((None, Q_BLOCK, gw), lambda b, g, i: (b, i, g))
    return pl.pallas_call(
        functools.partial(_nsa_slc_kernel, ns=ns, kt_near=kt_near, nkt=nkt),
        out_shape=jax.ShapeDtypeStruct((batch, seq, NSA_HEADS * NSA_HEAD_DIM), BF16),
        grid=(batch, NSA_KV_HEADS, nqb),
        in_specs=[
            head_spec,
            pl.BlockSpec((None, None, Q_BLOCK, ns), lambda b, g, i: (b, g, i, 0)),
            slab(srows), slab(srows),
            pl.BlockSpec((None,) + bias_s.shape[1:], lambda b, g, i: (g, 0, 0, 0)),
            pl.BlockSpec(eexp.shape, lambda b, g, i: (0, 0, 0)),
            pl.BlockSpec((None, Q_BLOCK, LANES), lambda b, g, i: (b, i, 0)),
            pl.BlockSpec((N_BRANCH, 2 * LANES, gw), lambda b, g, i: (0, 0, g)),
            head_spec, head_spec,
        ],
        out_specs=head_spec,
        scratch_shapes=[pltpu.VMEM((NSA_HPG * Q_BLOCK, LANES), F32),
                        pltpu.VMEM((NSA_HPG * Q_BLOCK, LANES), F32),
                        pltpu.VMEM((NSA_HPG * Q_BLOCK, SLC_KEY_TILE), BF16),
                        pltpu.VMEM((NSA_HPG * Q_BLOCK, LANES), F32)],
        compiler_params=_cparams(("parallel", "parallel", "arbitrary")),
        name="nsa_selected",
    )(q, sel, kvs, vks, bias_s, eexp, gates, egate, ocmp, owin)


def _nsa_tables(rel_bias, seq):
    nc = seq // CMP_STRIDE
    ns = seq // SLC_BLOCK
    i = np.arange(Q_BLOCK)[:, None]
    j = np.arange(nc)[None, :]
    dist_c = i - (CMP_BLOCK - 1) - CMP_STRIDE * (j - (nc - CMP_LOOKBACK))
    bias_c = _bias_table(rel_bias, dist_c, dist_c >= 0)
    cs = np.arange(nc)[:, None] * CMP_STRIDE
    ss = np.arange(ns)[None, :] * SLC_BLOCK
    ov = np.clip(np.minimum(cs + CMP_BLOCK, ss + SLC_BLOCK) - np.maximum(cs, ss), 0, None) / CMP_BLOCK
    ov2 = jnp.asarray(np.concatenate([ov, ov], axis=0).T, BF16)
    nkeys = ns * SLC_BLOCK
    nkt = nkeys // SLC_KEY_TILE
    pad_rows = (ns - 2) * SLC_BLOCK
    kt_near = max(0, pad_rows - REL_MAX_DIST) // SLC_KEY_TILE
    kr = np.arange(kt_near * SLC_KEY_TILE, nkeys)[None, :]
    dist_s = i + pad_rows - kr
    near = _bias_table(rel_bias, dist_s, dist_s >= 0)
    near = near.reshape(NSA_KV_HEADS, NSA_HPG * Q_BLOCK, nkt - kt_near, SLC_KEY_TILE).transpose(0, 2, 1, 3)
    far = jnp.broadcast_to(rel_bias.astype(F32)[REL_BUCKETS - 1].reshape(NSA_KV_HEADS, 1, NSA_HPG, 1, 1),
                           (NSA_KV_HEADS, 1, NSA_HPG, Q_BLOCK, SLC_KEY_TILE))
    bias_s = jnp.concatenate([far.reshape(NSA_KV_HEADS, 1, NSA_HPG * Q_BLOCK, SLC_KEY_TILE), near], axis=1)
    dist_w = i + WINDOW - np.arange(WINDOW + Q_BLOCK)[None, :]
    bwin = _bias_table(rel_bias, dist_w, (dist_w >= 0) & (dist_w < WINDOW))
    blk = np.arange(ns)[None, :, None]
    key = np.arange(SLC_KEY_TILE)[None, None, :]
    ktile = np.arange(nkt)[:, None, None]
    eexp = jnp.asarray(blk == ktile * SLC_BLOCKS_PER_TILE + key // SLC_BLOCK, BF16)
    col = np.arange(LANES)[:, None]
    head = (np.arange(NSA_HEADS * NSA_HEAD_DIM) // NSA_HEAD_DIM)[None, :]
    eg = np.stack([col == head * N_BRANCH + br for br in range(N_BRANCH)])
    egate = jnp.asarray(np.concatenate([eg, eg], axis=1), BF16)
    return bias_c, ov2, bias_s, bwin, eexp, egate


def _router_kernel(x_ref, g_ref, w_ref, o_ref):
    x = x_ref[...]
    ms = jnp.mean(x * x, axis=-1, keepdims=True)
    xn = x * lax.rsqrt(ms + NORM_EPS) * g_ref[...]
    logits = jnp.dot(xn, w_ref[...], preferred_element_type=F32, precision=lax.Precision.HIGHEST)
    lane = lax.broadcasted_iota(jnp.int32, logits.shape, 1).astype(F32)
    ninf = -jnp.inf
    s = jnp.where(lane < N_EXPERTS, logits, ninf)
    m1 = jnp.max(s, axis=-1, keepdims=True)
    i1 = jnp.min(jnp.where(s == m1, lane, float(LANES)), axis=-1, keepdims=True)
    s2 = jnp.where(lane == i1, ninf, s)
    m2 = jnp.max(s2, axis=-1, keepdims=True)
    i2 = jnp.min(jnp.where(s2 == m2, lane, float(LANES)), axis=-1, keepdims=True)
    e2 = jnp.exp(m2 - m1)
    w1 = 1.0 / (1.0 + e2)
    w2 = e2 * w1
    o_ref[...] = (jnp.where(lane == 0.0, i1, 0.0) + jnp.where(lane == 1.0, i2, 0.0)
                  + jnp.where(lane == 2.0, w1, 0.0) + jnp.where(lane == 3.0, w2, 0.0))


def _router(x, g, w_router, *, tm=512):
    n, d = x.shape
    tm = min(tm, n)
    wpad = jnp.zeros((d, LANES), F32).at[:, :N_EXPERTS].set(w_router.astype(F32))
    return pl.pallas_call(
        _router_kernel,
        out_shape=jax.ShapeDtypeStruct((n, LANES), F32),
        grid=(n // tm,),
        in_specs=[
            pl.BlockSpec((tm, d), lambda i: (i, 0)),
            pl.BlockSpec((1, d), lambda i: (0, 0)),
            pl.BlockSpec((d, LANES), lambda i: (0, 0)),
        ],
        out_specs=pl.BlockSpec((tm, LANES), lambda i: (i, 0)),
        compiler_params=_cparams(("parallel",)),
        name="moe_router",
    )(x, g.reshape(1, d).astype(F32), wpad)


def _moe_dispatch(route, tm):
    n = route.shape[0]
    pairs = 2 * n
    e = route[:, 0:2].astype(jnp.int32).reshape(pairs)
    onehot = (e[:, None] == jnp.arange(N_EXPERTS, dtype=jnp.int32)[None, :]).astype(jnp.int32)
    csum = jnp.cumsum(onehot, axis=0)
    rank = jnp.sum(csum * onehot, axis=1) - 1
    counts = csum[-1]
    cpad = ((counts + tm - 1) // tm) * tm
    gend = jnp.cumsum(cpad)
    dst = jnp.sum(onehot * (gend - cpad)[None, :], axis=1) + rank
    rows = pairs + N_EXPERTS * tm
    ntiles = rows // tm
    nvalid = gend[-1] // tm
    tile = jnp.arange(ntiles, dtype=jnp.int32)
    te = jnp.sum((tile[:, None] * tm >= gend[None, :]).astype(jnp.int32), axis=1)
    te = jnp.where(tile < nvalid, te, te[nvalid - 1])
    src = jnp.zeros((rows,), jnp.int32).at[dst].set(jnp.arange(pairs, dtype=jnp.int32) // 2)
    return src, dst, te.astype(jnp.int32), nvalid.reshape(1).astype(jnp.int32)


def _row_copy(src_hbm, row, dst_ref, j, sem):
    return pltpu.make_async_copy(src_hbm.at[pl.ds(row, 1), :], dst_ref.at[pl.ds(j, 1), :], sem)


def _combine_kernel(i0_ref, i1_ref, y_hbm, h_ref, route_ref, o_ref, buf_ref, sem):
    rows = h_ref.shape[0]

    def issue(g, c):
        for u in range(DMA_ISSUE_UNROLL):
            j = g * DMA_ISSUE_UNROLL + u
            _row_copy(y_hbm, i0_ref[j], buf_ref.at[0], j, sem).start(priority=0)
            _row_copy(y_hbm, i1_ref[j], buf_ref.at[1], j, sem).start(priority=1)
        return c

    lax.fori_loop(0, rows // DMA_ISSUE_UNROLL, issue, 0)
    for k in range(2):
        pltpu.make_async_copy(y_hbm.at[pl.ds(0, rows), :], buf_ref.at[k], sem).wait()
    route = route_ref[...]
    o_ref[...] = h_ref[...] + route[:, 2:3] * buf_ref[0] + route[:, 3:4] * buf_ref[1]


def _combine(h, y, dst, route, *, rows=512):
    n, d = h.shape
    assert n % rows == 0
    dst2 = dst.reshape(n, 2)
    return pl.pallas_call(
        _combine_kernel,
        out_shape=jax.ShapeDtypeStruct((n, d), F32),
        grid=(n // rows,),
        in_specs=[
            pl.BlockSpec((rows,), lambda i: (i,), memory_space=pltpu.SMEM),
            pl.BlockSpec((rows,), lambda i: (i,), memory_space=pltpu.SMEM),
            pl.BlockSpec(memory_space=pl.ANY),
            pl.BlockSpec((rows, d), lambda i: (i, 0)),
            pl.BlockSpec((rows, LANES), lambda i: (i, 0)),
        ],
        out_specs=pl.BlockSpec((rows, d), lambda i: (i, 0)),
        scratch_shapes=[pltpu.VMEM((2, rows, d), F32), pltpu.SemaphoreType.DMA],
        compiler_params=_cparams(("arbitrary",)),
        name="moe_combine",
    )(dst2[:, 0], dst2[:, 1], y, h, route)


def _ffn_grouped_kernel(te_ref, nv_ref, cur_ref, nxt_ref, h_hbm, g_ref, wg_ref, wu_ref, wd_ref, o_ref,
                        xbuf_ref, sem_ref, xn_ref, acc_ref, *, nf):
    i = pl.program_id(0)
    f = pl.program_id(1)
    tm = xbuf_ref.shape[1]
    used = i < nv_ref[0]
    slot = lax.rem(i, 2)

    part = tm // nf

    def gather(idx_ref, s, first, count):
        def issue(q, c):
            for u in range(DMA_ISSUE_UNROLL):
                j = first + q * DMA_ISSUE_UNROLL + u
                _row_copy(h_hbm, idx_ref[j], xbuf_ref.at[s], j, sem_ref.at[s]).start()
            return c

        lax.fori_loop(0, count // DMA_ISSUE_UNROLL, issue, 0)

    @pl.when(jnp.logical_and(used, jnp.logical_and(i == 0, f == 0)))
    def _():
        gather(cur_ref, 0, 0, tm)

    @pl.when(i + 1 < nv_ref[0])
    def _():
        gather(nxt_ref, 1 - slot, f * part, part)

    @pl.when(jnp.logical_and(used, f == 0))
    def _():
        pltpu.make_async_copy(h_hbm.at[pl.ds(0, tm), :], xbuf_ref.at[slot], sem_ref.at[slot]).wait()
        x = xbuf_ref[slot]
        ms = jnp.mean(x * x, axis=-1, keepdims=True)
        xn_ref[...] = (x * lax.rsqrt(ms + NORM_EPS) * g_ref[...]).astype(BF16)
        acc_ref[...] = jnp.zeros_like(acc_ref)

    @pl.when(used)
    def _():
        xn = xn_ref[...]
        gate = _dot(xn, wg_ref[...])
        up = _dot(xn, wu_ref[...])
        act = (gate * jax.nn.sigmoid(gate) * up).astype(BF16)
        acc_ref[...] += _dot(act, wd_ref[...])

    last = f == pl.num_programs(1) - 1

    @pl.when(jnp.logical_and(used, last))
    def _():
        o_ref[...] = acc_ref[...]

    @pl.when(jnp.logical_and(jnp.logical_not(used), last))
    def _():
        o_ref[...] = jnp.zeros_like(o_ref)


def _ffn_grouped(h, src, g, w_gu, w_down, te, nvalid, *, tm, tf=None):
    rows = src.shape[0]
    d = h.shape[1]
    ff = w_down.shape[1]
    if tf is None:
        tf = ff // 2 if (ff // 2) % LANES == 0 else ff
    assert rows % tm == 0 and ff % tf == 0 and tf % LANES == 0
    assert tm % ((ff // tf) * DMA_ISSUE_UNROLL) == 0
    nf = ff // tf
    ntiles = rows // tm

    def fcol(i, f, te_ref, nv_ref):
        return jnp.where(i < nv_ref[0], f, nf - 1)

    grid_spec = pltpu.PrefetchScalarGridSpec(
        num_scalar_prefetch=2,
        grid=(ntiles, nf),
        in_specs=[
            pl.BlockSpec((tm,), lambda i, f, te_ref, nv_ref: (i,), memory_space=pltpu.SMEM),
            pl.BlockSpec((tm,), lambda i, f, te_ref, nv_ref: (jnp.minimum(i + 1, ntiles - 1),),
                         memory_space=pltpu.SMEM),
            pl.BlockSpec(memory_space=pl.ANY),
            pl.BlockSpec((1, d), lambda i, f, te_ref, nv_ref: (0, 0)),
            pl.BlockSpec((None, d, tf), lambda i, f, te_ref, nv_ref: (te_ref[i], 0, fcol(i, f, te_ref, nv_ref))),
            pl.BlockSpec((None, d, tf),
                         lambda i, f, te_ref, nv_ref: (te_ref[i], 0, nf + fcol(i, f, te_ref, nv_ref))),
            pl.BlockSpec((None, tf, d), lambda i, f, te_ref, nv_ref: (te_ref[i], fcol(i, f, te_ref, nv_ref), 0)),
        ],
        out_specs=pl.BlockSpec((tm, d), lambda i, f, te_ref, nv_ref: (i, 0)),
        scratch_shapes=[pltpu.VMEM((2, tm, d), F32), pltpu.SemaphoreType.DMA((2,)),
                        pltpu.VMEM((tm, d), BF16), pltpu.VMEM((tm, d), F32)],
    )
    return pl.pallas_call(
        functools.partial(_ffn_grouped_kernel, nf=nf),
        out_shape=jax.ShapeDtypeStruct((rows, d), F32),
        grid_spec=grid_spec,
        compiler_params=_cparams(("arbitrary", "arbitrary")),
        name="expert_ffn_grouped",
    )(te, nvalid, src, src, h, g.reshape(1, d).astype(F32), w_gu, w_gu, w_down)


def _nsa_layer(h, batch, seq, norm_g, kv_norm_g, w_kv, k_norm_g, cmp_pos_k, cmp_w1_k, cmp_w2_k,
               cmp_pos_v, cmp_w1_v, cmp_w2_v, w_in, w_out, q_norm_g, rel_bias):
    n, d = h.shape
    g_heads, dh = NSA_KV_HEADS, NSA_HEAD_DIM
    nc = seq // CMP_STRIDE
    ns = seq // SLC_BLOCK
    hq = NSA_HEADS * dh

    kvproj = _norm_matmul(h, kv_norm_g, w_kv.astype(BF16), name="nsa_kv_proj")

    gw_kv = g_heads * dh

    def chunks(col0):
        t = kvproj[:, col0:col0 + gw_kv].astype(BF16).reshape(batch, seq, g_heads, dh)
        return t.transpose(0, 2, 1, 3).reshape(batch, g_heads, nc, CMP_STRIDE * dh)

    k_cmp = _compress(chunks(0), cmp_w1_k, cmp_pos_k, cmp_w2_k, k_norm_g[0],
                      apply_norm=True, name="nsa_compress_k")
    v_cmp = _compress(chunks(gw_kv), cmp_w1_v, cmp_pos_v, cmp_w2_v, k_norm_g[0],
                      apply_norm=False, name="nsa_compress_v")
    cpad = nc - CMP_LOOKBACK
    padc = lambda t: jnp.pad(t, ((0, 0), (0, 0), (cpad, 0), (0, 0)))
    kvc = padc(jnp.concatenate([k_cmp, k_cmp], axis=-1))
    vkc = padc(jnp.concatenate([v_cmp, jnp.ones_like(v_cmp)], axis=-1))

    kvs, vks = _pack_kv(kvproj, k_norm_g[1], 1, batch, seq, name="nsa_pack_selected")
    kvw, vkw = _pack_kv(kvproj, k_norm_g[2], 2, batch, seq, name="nsa_pack_window")
    spad = (ns - 2) * SLC_BLOCK
    pads = lambda t, r: jnp.pad(t, ((0, 0), (0, 0), (r, 0), (0, 0)))
    kvs, vks = pads(kvs, spad), pads(vks, spad)
    kvw, vkw = pads(kvw, WINDOW), pads(vkw, WINDOW)

    qgain = jnp.tile(q_norm_g.astype(F32), NSA_HEADS) * (dh ** -0.5)
    q = _norm_matmul(h, norm_g, w_in[:, :hq].astype(BF16), epilogue="headnorm", gain=qgain,
                     flag=jnp.ones((hq,), F32), out_dtype=BF16, name="nsa_q_proj")
    ngate = N_BRANCH * NSA_HEADS
    wg = jnp.zeros((d, LANES), BF16).at[:, :ngate].set(w_in[:, hq:].astype(BF16))
    gates = _norm_matmul(h, norm_g, wg, epilogue="sigmoid", name="nsa_gate_proj")

    bias_c, ov2, bias_s, bwin, eexp, egate = _nsa_tables(rel_bias, seq)
    q3 = q.reshape(batch, seq, hq)
    o_cmp, imp_t, o_win = _nsa_cmp_win(q3, kvc, vkc, bias_c, ov2, kvw, vkw, bwin, batch, seq)
    sel = _topk_select(imp_t)
    o = _nsa_slc(q3, sel, kvs, vks, bias_s, eexp, gates.reshape(batch, seq, LANES), egate, o_cmp, o_win,
                 batch, seq)
    return _matmul_res(o.reshape(n, hq), w_out.astype(BF16), h, name="nsa_out_proj")


def kernel(x, norm_mix_g, norm_ffn_g, a_w_in, a_w_out, a_onorm_g, lb_param, kv_norm_g, w_kv, k_norm_g,
           cmp_pos_k, cmp_w1_k, cmp_w2_k, cmp_pos_v, cmp_w1_v, cmp_w2_v, b_w_in, b_w_out, b_qnorm_g,
           rel_bias, ffn_w_gu, ffn_w_down, moe_router, moe_w_gu, moe_w_down):
    batch, seq, d = x.shape
    n = batch * seq
    h = x.reshape(n, d).astype(F32)

    lower = jnp.cumsum(jax.nn.softmax(lb_param.astype(F32), axis=0), axis=0)[0]
    proj = _norm_matmul(h, norm_mix_g[0], a_w_in[0].astype(BF16), name="hgrn_in_proj")
    og = _hgrn_mixer(proj, lower, a_onorm_g[0], batch, seq)
    h = _matmul_res(og, a_w_out[0].astype(BF16), h, name="hgrn_out_proj")
    h = _ffn(h, norm_ffn_g[0], ffn_w_gu[0].astype(BF16), ffn_w_down[0].astype(BF16), tm=1024, name="dense_ffn")

    h = _nsa_layer(h, batch, seq, norm_mix_g[1], kv_norm_g, w_kv, k_norm_g, cmp_pos_k, cmp_w1_k, cmp_w2_k,
                   cmp_pos_v, cmp_w1_v, cmp_w2_v, b_w_in[0], b_w_out[0], b_qnorm_g[0], rel_bias)
    route = _router(h, norm_ffn_g[1], moe_router[0])
    src, dst, tile_expert, tiles_used = _moe_dispatch(route, MOE_ROW_TILE)
    ys = _ffn_grouped(h, src, norm_ffn_g[1], moe_w_gu[0].astype(BF16), moe_w_down[0].astype(BF16),
                      tile_expert, tiles_used, tm=MOE_ROW_TILE)
    out = _combine(h, ys, dst, route)
    return out.reshape(batch, seq, d).astype(x.dtype)
```
